```python
import math
import jax, jax.numpy as jnp
from jax import lax
import numpy as np

D_MODEL = 1024
BATCH = 8
SEQ = 4096
DEPTH = 4

D_FF = 2816
HEAD_DIM = 64
Q_BLOCK = 128
N_DIFF_HEADS = (D_MODEL // 2) // (2 * HEAD_DIM)
N_FOX_HEADS = (D_MODEL // 2) // HEAD_DIM
DIFF_V_DIM = 2 * HEAD_DIM
N_REL_BUCKETS = 32
REL_MAX_DIST = 128
S5_GROUP = 16
S5_GROUPS = D_MODEL // S5_GROUP
S5_STATE = 64
NORM_EPS = 1e-6
N_ATTN_LAYERS = (DEPTH + 1) // 2
N_SSM_LAYERS = DEPTH // 2

DIFF_QK_COLS = N_DIFF_HEADS * 2 * HEAD_DIM
DIFF_V_COLS = N_DIFF_HEADS * DIFF_V_DIM
FOX_COLS = N_FOX_HEADS * HEAD_DIM
IN_SPLITS = (DIFF_QK_COLS, DIFF_QK_COLS, DIFF_V_COLS, FOX_COLS, FOX_COLS, FOX_COLS, N_FOX_HEADS)
IN_COLS = sum(IN_SPLITS)
MIX_COLS = DIFF_V_COLS + FOX_COLS

kernel_name = "hybrid_diff_fox_s5_macaron"


def rmsnorm(x, g):
    xf = x.astype(jnp.float32)
    y = xf * lax.rsqrt(jnp.mean(xf * xf, axis=-1, keepdims=True) + NORM_EPS)
    return (y * g.astype(jnp.float32)).astype(x.dtype)


def swiglu(h, w_gate, w_up, w_down):
    return (jax.nn.silu(h @ w_gate) * (h @ w_up)) @ w_down


def rel_bucket(q_pos, k_pos):
    n = jnp.maximum(q_pos[:, None] - k_pos[None, :], 0)
    max_exact = N_REL_BUCKETS // 2
    nf = jnp.maximum(n, 1).astype(jnp.float32)
    large = max_exact + (jnp.log(nf / max_exact) / math.log(REL_MAX_DIST / max_exact)
                         * (N_REL_BUCKETS - max_exact)).astype(jnp.int32)
    large = jnp.minimum(large, N_REL_BUCKETS - 1)
    return jnp.where(n < max_exact, n, large)


def diff_fox_mixer(h, w_in, w_out, fg_bias, dq_g, dk_g, lq1, lk1, lq2, lk2, subln_g,
                   fq_g, fk_g, rel_bias, layer_idx):
    f32 = jnp.float32
    b, s, _ = h.shape
    proj = h @ w_in
    dq, dk, dv, fq, fk, fv, fl = jnp.split(proj, np.cumsum(IN_SPLITS)[:-1], axis=-1)
    dq = rmsnorm(dq.reshape(b, s, N_DIFF_HEADS, 2, HEAD_DIM), dq_g).transpose(0, 2, 3, 1, 4)
    dk = rmsnorm(dk.reshape(b, s, N_DIFF_HEADS, 2, HEAD_DIM), dk_g).transpose(0, 2, 3, 1, 4)
    dv = dv.reshape(b, s, N_DIFF_HEADS, DIFF_V_DIM).transpose(0, 2, 1, 3)
    fq = rmsnorm(fq.reshape(b, s, N_FOX_HEADS, HEAD_DIM), fq_g).transpose(0, 2, 1, 3)
    fk = rmsnorm(fk.reshape(b, s, N_FOX_HEADS, HEAD_DIM), fk_g).transpose(0, 2, 1, 3)
    fv = fv.reshape(b, s, N_FOX_HEADS, HEAD_DIM).transpose(0, 2, 1, 3)
    log_f = jax.nn.log_sigmoid(fl.astype(f32) + fg_bias.astype(f32))
    cum = jnp.cumsum(log_f, axis=1).transpose(0, 2, 1)

    lam_init = 0.8 - 0.6 * math.exp(-0.3 * layer_idx)
    lam = (jnp.exp(jnp.sum(lq1.astype(f32) * lk1.astype(f32)))
           - jnp.exp(jnp.sum(lq2.astype(f32) * lk2.astype(f32))) + lam_init)
    scale = HEAD_DIM ** -0.5
    pos = jnp.arange(s, dtype=jnp.int32)
    od_blocks, of_blocks = [], []
    for blk in range(s // Q_BLOCK):
        q0, q1 = blk * Q_BLOCK, (blk + 1) * Q_BLOCK
        qpos, kpos = pos[q0:q1], pos[:q1]
        causal = kpos[None, :] <= qpos[:, None]
        bias = rel_bias[rel_bucket(qpos, kpos)].transpose(2, 0, 1).astype(f32)
        lg = (jnp.einsum('bhmqd,bhmkd->bhmqk', dq[:, :, :, q0:q1], dk[:, :, :, :q1]).astype(f32) * scale
              + bias[None, :, None])
        p = jax.nn.softmax(jnp.where(causal, lg, -jnp.inf), axis=-1)
        pd = p[:, :, 0] - lam * p[:, :, 1]
        od_blocks.append(jnp.einsum('bhqk,bhkv->bhqv', pd.astype(dv.dtype), dv[:, :, :q1]))
        lf = (jnp.einsum('bhqd,bhkd->bhqk', fq[:, :, q0:q1], fk[:, :, :q1]).astype(f32) * scale
              + cum[:, :, q0:q1, None] - cum[:, :, None, :q1])
        pf = jax.nn.softmax(jnp.where(causal, lf, -jnp.inf), axis=-1)
        of_blocks.append(jnp.einsum('bhqk,bhkv->bhqv', pf.astype(fv.dtype), fv[:, :, :q1]))
    od = jnp.concatenate(od_blocks, axis=2)
    od = rmsnorm(od, subln_g) * (1.0 - lam_init)
    od = od.transpose(0, 2, 1, 3).reshape(b, s, DIFF_V_COLS)
    of = jnp.concatenate(of_blocks, axis=2).transpose(0, 2, 1, 3).reshape(b, s, FOX_COLS)
    return jnp.concatenate([od, of], axis=-1) @ w_out


def _complex_linear_combine(left, right):
    ar1, ai1, br1, bi1 = left
    ar2, ai2, br2, bi2 = right
    return (ar2 * ar1 - ai2 * ai1,
            ar2 * ai1 + ai2 * ar1,
            ar2 * br1 - ai2 * bi1 + br2,
            ar2 * bi1 + ai2 * br1 + bi2)


def s5_mixer(h, a_re, a_im, log_step, b_re, b_im, c_re, c_im, d_skip, w_glu_a, w_glu_b):
    f32 = jnp.float32
    b, s, _ = h.shape
    u = h.reshape(b, s, S5_GROUPS, S5_GROUP).astype(f32)
    step = jnp.exp(log_step.astype(f32))[:, None]
    ar, ai = a_re.astype(f32), a_im.astype(f32)
    mag = jnp.exp(ar * step)
    lr, li = mag * jnp.cos(ai * step), mag * jnp.sin(ai * step)
    den = ar * ar + ai * ai
    cr = ((lr - 1.0) * ar + li * ai) / den
    ci = (li * ar - (lr - 1.0) * ai) / den
    br, bi = b_re.astype(f32), b_im.astype(f32)
    bbr = cr[..., None] * br - ci[..., None] * bi
    bbi = cr[..., None] * bi + ci[..., None] * br
    xr = jnp.einsum('bsgc,gpc->bsgp', u, bbr)
    xi = jnp.einsum('bsgc,gpc->bsgp', u, bbi)
    lr_s = jnp.broadcast_to(lr, (1, s, S5_GROUPS, S5_STATE))
    li_s = jnp.broadcast_to(li, (1, s, S5_GROUPS, S5_STATE))
    _, _, xr, xi = lax.associative_scan(_complex_linear_combine, (lr_s, li_s, xr, xi), axis=1)
    y = (jnp.einsum('bsgp,gcp->bsgc', xr, c_re.astype(f32))
         - jnp.einsum('bsgp,gcp->bsgc', xi, c_im.astype(f32)))
    y = y.reshape(b, s, D_MODEL) + d_skip.astype(f32) * h.astype(f32)
    y = jax.nn.gelu(y).astype(h.dtype)
    return (y @ w_glu_a) * jax.nn.sigmoid(y @ w_glu_b)


def setup_inputs(seed: int = 0) -> dict:
    key = jax.random.key(seed)
    keys = list(jax.random.split(key, 48))
    it = iter(keys)

    def nrm(shape, scale):
        return jax.random.normal(next(it), shape, jnp.float32) * scale

    def gain(shape):
        return 1.0 + nrm(shape, 0.05)

    L, La, Ls = DEPTH, N_ATTN_LAYERS, N_SSM_LAYERS
    x = nrm((BATCH, SEQ, D_MODEL), 1.0)
    ffn1_norm = gain((L, D_MODEL))
    ffn1_gate = nrm((L, D_MODEL, D_FF), D_MODEL ** -0.5)
    ffn1_up = nrm((L, D_MODEL, D_FF), D_MODEL ** -0.5)
    ffn1_down = nrm((L, D_FF, D_MODEL), D_FF ** -0.5)
    mix_norm = gain((L, D_MODEL))
    ffn2_norm = gain((L, D_MODEL))
    ffn2_gate = nrm((L, D_MODEL, D_FF), D_MODEL ** -0.5)
    ffn2_up = nrm((L, D_MODEL, D_FF), D_MODEL ** -0.5)
    ffn2_down = nrm((L, D_FF, D_MODEL), D_FF ** -0.5)
    attn_w_in = nrm((La, D_MODEL, IN_COLS), D_MODEL ** -0.5)
    attn_w_out = nrm((La, MIX_COLS, D_MODEL), MIX_COLS ** -0.5)
    fg_bias = jax.random.uniform(next(it), (La, N_FOX_HEADS), jnp.float32, 1.0, 5.0)
    diff_q_norm = gain((La, HEAD_DIM))
    diff_k_norm = gain((La, HEAD_DIM))
    diff_lambda_q1 = nrm((La, HEAD_DIM), 0.1)
    diff_lambda_k1 = nrm((La, HEAD_DIM), 0.1)
    diff_lambda_q2 = nrm((La, HEAD_DIM), 0.1)
    diff_lambda_k2 = nrm((La, HEAD_DIM), 0.1)
    diff_subln = gain((La, DIFF_V_DIM))
    fox_q_norm = gain((La, HEAD_DIM))
    fox_k_norm = gain((La, HEAD_DIM))
    rel_bias = nrm((N_REL_BUCKETS, N_DIFF_HEADS), 0.5)
    s5_a_re = -0.5 + nrm((Ls, S5_GROUPS, S5_STATE), 0.01)
    s5_a_im = (math.pi * jnp.arange(S5_STATE, dtype=jnp.float32))[None, None, :] \
        + nrm((Ls, S5_GROUPS, S5_STATE), 0.01)
    s5_log_step = jax.random.uniform(next(it), (Ls, S5_GROUPS), jnp.float32,
                                     math.log(1e-3), math.log(1e-1))
    s5_b_re = nrm((Ls, S5_GROUPS, S5_STATE, S5_GROUP), (2 * S5_GROUP) ** -0.5)
    s5_b_im = nrm((Ls, S5_GROUPS, S5_STATE, S5_GROUP), (2 * S5_GROUP) ** -0.5)
    s5_c_re = nrm((Ls, S5_GROUPS, S5_GROUP, S5_STATE), S5_STATE ** -0.5)
    s5_c_im = nrm((Ls, S5_GROUPS, S5_GROUP, S5_STATE), S5_STATE ** -0.5)
    s5_d = nrm((Ls, D_MODEL), 1.0)
    s5_glu_a = nrm((Ls, D_MODEL, D_MODEL), D_MODEL ** -0.5)
    s5_glu_b = nrm((Ls, D_MODEL, D_MODEL), D_MODEL ** -0.5)
    return {"x": x,
            "ffn1_norm": ffn1_norm, "ffn1_gate": ffn1_gate, "ffn1_up": ffn1_up, "ffn1_down": ffn1_down,
            "mix_norm": mix_norm,
            "ffn2_norm": ffn2_norm, "ffn2_gate": ffn2_gate, "ffn2_up": ffn2_up, "ffn2_down": ffn2_down,
            "attn_w_in": attn_w_in, "attn_w_out": attn_w_out, "fg_bias": fg_bias,
            "diff_q_norm": diff_q_norm, "diff_k_norm": diff_k_norm,
            "diff_lambda_q1": diff_lambda_q1, "diff_lambda_k1": diff_lambda_k1,
            "diff_lambda_q2": diff_lambda_q2, "diff_lambda_k2": diff_lambda_k2,
            "diff_subln": diff_subln, "fox_q_norm": fox_q_norm, "fox_k_norm": fox_k_norm,
            "rel_bias": rel_bias,
            "s5_a_re": s5_a_re, "s5_a_im": s5_a_im, "s5_log_step": s5_log_step,
            "s5_b_re": s5_b_re, "s5_b_im": s5_b_im, "s5_c_re": s5_c_re, "s5_c_im": s5_c_im,
            "s5_d": s5_d, "s5_glu_a": s5_glu_a, "s5_glu_b": s5_glu_b}


def reference(x, ffn1_norm, ffn1_gate, ffn1_up, ffn1_down, mix_norm,
              ffn2_norm, ffn2_gate, ffn2_up, ffn2_down,
              attn_w_in, attn_w_out, fg_bias, diff_q_norm, diff_k_norm,
              diff_lambda_q1, diff_lambda_k1, diff_lambda_q2, diff_lambda_k2,
              diff_subln, fox_q_norm, fox_k_norm, rel_bias,
              s5_a_re, s5_a_im, s5_log_step, s5_b_re, s5_b_im, s5_c_re, s5_c_im,
              s5_d, s5_glu_a, s5_glu_b):
    for i in range(DEPTH):
        x = x + 0.5 * swiglu(rmsnorm(x, ffn1_norm[i]), ffn1_gate[i], ffn1_up[i], ffn1_down[i])
        h = rmsnorm(x, mix_norm[i])
        j = i // 2
        if i % 2 == 0:
            x = x + diff_fox_mixer(h, attn_w_in[j], attn_w_out[j], fg_bias[j],
                                   diff_q_norm[j], diff_k_norm[j],
                                   diff_lambda_q1[j], diff_lambda_k1[j],
                                   diff_lambda_q2[j], diff_lambda_k2[j],
                                   diff_subln[j], fox_q_norm[j], fox_k_norm[j],
                                   rel_bias, i)
        else:
            x = x + s5_mixer(h, s5_a_re[j], s5_a_im[j], s5_log_step[j],
                             s5_b_re[j], s5_b_im[j], s5_c_re[j], s5_c_im[j],
                             s5_d[j], s5_glu_a[j], s5_glu_b[j])
        x = x + 0.5 * swiglu(rmsnorm(x, ffn2_norm[i]), ffn2_gate[i], ffn2_up[i], ffn2_down[i])
    return x
```

```python
import functools
import math

import jax
import jax.numpy as jnp
from jax import lax
from jax.experimental import pallas as pl
from jax.experimental.pallas import tpu as pltpu

D_MODEL = 1024
D_FF = 2816
HEAD_DIM = 64
N_DIFF_HEADS = 4
N_FOX_HEADS = 8
N_REL_BUCKETS = 32
REL_MAX_DIST = 128
S5_GROUP = 16
S5_GROUPS = D_MODEL // S5_GROUP
S5_STATE = 64
NORM_EPS = 1e-6

QKV_COLS = 3072
LANES = 128
MXU_DIM = 256
VMEM_LIMIT = 56 * 1024 * 1024

TOKEN_TILE = 512
FFN_CHUNK = 1408
ATTN_BLOCK = 512
S5_CHUNK = 16
NEG_BIG = -1e30

BF16 = jnp.bfloat16
F32 = jnp.float32


def _dot(a, b):
    return jnp.dot(a, b, preferred_element_type=F32)


def _dot_nt(a, b):
    return lax.dot_general(a, b, (((1,), (1,)), ((), ())), preferred_element_type=F32)


def _rms(x, g):
    ms = jnp.mean(x * x, axis=-1, keepdims=True)
    return x * lax.rsqrt(ms + NORM_EPS) * g


def _const_spec(shape):
    return pl.BlockSpec(shape, lambda *_: (0,) * len(shape))


def _params(n_axes):
    return pltpu.CompilerParams(dimension_semantics=("arbitrary",) * n_axes,
                                vmem_limit_bytes=VMEM_LIMIT)


def _ffn_kernel(x_ref, g_ref, gn_ref, wg_ref, wu_ref, wd_ref, o_ref, *h_out):
    x = x_ref[...]
    h = _rms(x, g_ref[...]).astype(BF16)
    acc = None
    for c in range(D_FF // FFN_CHUNK):
        sl = slice(c * FFN_CHUNK, (c + 1) * FFN_CHUNK)
        a = _dot(h, wg_ref[:, sl])
        b = _dot(h, wu_ref[:, sl])
        act = (a * jax.nn.sigmoid(a) * b).astype(BF16)
        d = _dot(act, wd_ref[sl, :])
        acc = d if acc is None else acc + d
    xo = x + 0.5 * acc
    o_ref[...] = xo
    if h_out:
        h_out[0][...] = _rms(xo, gn_ref[...]).astype(BF16)


def _ffn(x, g, wg, wu, wd, g_next=None):
    t = x.shape[0]
    tm = min(TOKEN_TILE, t)
    emit = g_next is not None
    gn = g_next if emit else g
    row = pl.BlockSpec((tm, D_MODEL), lambda i: (i, 0))
    out_shape = [jax.ShapeDtypeStruct((t, D_MODEL), F32)]
    out_specs = [row]
    if emit:
        out_shape.append(jax.ShapeDtypeStruct((t, D_MODEL), BF16))
        out_specs.append(row)
    res = pl.pallas_call(
        _ffn_kernel,
        grid=(t // tm,),
        in_specs=[row, _const_spec((1, D_MODEL)), _const_spec((1, D_MODEL)),
                  _const_spec((D_MODEL, D_FF)), _const_spec((D_MODEL, D_FF)),
                  _const_spec((D_FF, D_MODEL))],
        out_specs=out_specs,
        out_shape=out_shape,
        compiler_params=_params(1),
        name="ffn",
    )(x, g.reshape(1, D_MODEL), gn.reshape(1, D_MODEL), wg, wu, wd)
    return (res[0], res[1]) if emit else (res[0], None)


_NORM_TILES = (0, 1, 3, 4)


def _inproj_kernel(tiles_per_seq, h_ref, w_ref, wf_ref, gains_ref, ones_ref, tri_ref, fgb_ref,
                   qkv_ref, cum_ref, carry_ref):
    i = pl.program_id(0)
    h = h_ref[...]
    ones_blk = ones_ref[...]
    for c in range(QKV_COLS // 512):
        y = _dot(h, w_ref[:, c * 512:(c + 1) * 512])
        if c in _NORM_TILES:
            y2 = y * y
            hi = y2.astype(BF16)
            lo = (y2 - hi.astype(F32)).astype(BF16)
            ssq = jnp.concatenate(
                [_dot(hi[:, k * MXU_DIM:(k + 1) * MXU_DIM], ones_blk)
                 + _dot(lo[:, k * MXU_DIM:(k + 1) * MXU_DIM], ones_blk)
                 for k in range(512 // MXU_DIM)], axis=1)
            y = y * lax.rsqrt(ssq * (1.0 / HEAD_DIM) + NORM_EPS) * gains_ref[c:c + 1, :]
        qkv_ref[:, c * 512:(c + 1) * 512] = y.astype(BF16)

    fl = _dot(h, wf_ref[...]) + fgb_ref[...]
    logf = -(jnp.maximum(-fl, 0.0) + jnp.log1p(jnp.exp(-jnp.abs(fl))))

    @pl.when(i % tiles_per_seq == 0)
    def _():
        carry_ref[...] = jnp.zeros_like(carry_ref)

    p0 = logf.astype(BF16)
    r1 = logf - p0.astype(F32)
    p1 = r1.astype(BF16)
    p2 = (r1 - p1.astype(F32)).astype(BF16)
    tri = tri_ref[...]
    cum = _dot(tri, p0) + _dot(tri, p1) + _dot(tri, p2) + carry_ref[...]
    cum_ref[...] = cum
    carry_ref[...] = cum[cum.shape[0] - 1:, :]


def _inproj(h, w_qkv, w_f, gains, fg_bias_row, seq):
    t = h.shape[0]
    tm = min(TOKEN_TILE, seq)
    ones_blk = jnp.kron(jnp.eye(MXU_DIM // HEAD_DIM, dtype=F32),
                        jnp.ones((HEAD_DIM, HEAD_DIM), F32)).astype(BF16)
    tri = jnp.tril(jnp.ones((tm, tm), F32)).astype(BF16)
    return pl.pallas_call(
        functools.partial(_inproj_kernel, seq // tm),
        grid=(t // tm,),
        in_specs=[pl.BlockSpec((tm, D_MODEL), lambda i: (i, 0)),
                  _const_spec((D_MODEL, QKV_COLS)), _const_spec((D_MODEL, LANES)),
                  _const_spec((6, 512)), _const_spec((MXU_DIM, MXU_DIM)),
                  _const_spec((tm, tm)), _const_spec((1, LANES))],
        out_specs=[pl.BlockSpec((tm, QKV_COLS), lambda i: (i, 0)),
                   pl.BlockSpec((tm, LANES), lambda i: (i, 0))],
        out_shape=[jax.ShapeDtypeStruct((t, QKV_COLS), BF16),
                   jax.ShapeDtypeStruct((t, LANES), F32)],
        scratch_shapes=[pltpu.VMEM((1, LANES), F32)],
        compiler_params=_params(1),
        name="attn_inproj",
    )(h, w_qkv, w_f, gains, ones_blk, tri, fg_bias_row)


def _softmax_step(s, v_blk, idx, m_ref, l_ref, acc_ref):
    m_prev = m_ref[idx]
    m_new = jnp.maximum(m_prev, jnp.max(s, axis=1, keepdims=True))
    alpha = jnp.exp(m_prev - m_new)
    p = jnp.exp(s - m_new)
    l_ref[idx] = alpha * l_ref[idx] + jnp.sum(p, axis=1, keepdims=True)
    acc_ref[idx] = alpha * acc_ref[idx] + _dot(p.astype(BF16), v_blk)
    m_ref[idx] = m_new


def _attn_init(m_ref, l_ref, acc_ref):
    m_ref[...] = jnp.full_like(m_ref, NEG_BIG)
    l_ref[...] = jnp.zeros_like(l_ref)
    acc_ref[...] = jnp.zeros_like(acc_ref)


def _split_q(q):
    lane = lax.broadcasted_iota(jnp.int32, q.shape, 1)
    zero = jnp.zeros_like(q)
    return jnp.where(lane < HEAD_DIM, q, zero), jnp.where(lane < HEAD_DIM, zero, q)


def _causal_mask(blk):
    row = lax.broadcasted_iota(jnp.int32, (blk, blk), 0)
    col = lax.broadcasted_iota(jnp.int32, (blk, blk), 1)
    return col <= row


def _diff_attn_kernel(scal_ref, q_ref, k_ref, v_ref, bias_ref, subln_ref, o_ref,
                      m_ref, l_ref, acc_ref):
    qi = pl.program_id(2)
    blk = q_ref.shape[0]
    qs = _split_q(q_ref[...])
    _attn_init(m_ref, l_ref, acc_ref)

    def step(j, bias, mask):
        rows = pl.ds(pl.multiple_of(j * blk, blk), blk)
        k_blk = k_ref[rows, :]
        v_blk = v_ref[rows, :]
        for idx in range(2):
            s = _dot_nt(qs[idx], k_blk)
            if bias is not None:
                s = s + bias
            if mask is not None:
                s = jnp.where(mask, s, NEG_BIG)
            _softmax_step(s, v_blk, idx, m_ref, l_ref, acc_ref)

    def far(j, carry):
        step(j, None, None)
        return carry

    lax.fori_loop(0, jnp.maximum(qi - 1, 0), far, 0)

    @pl.when(qi >= 1)
    def _():
        step(qi - 1, bias_ref[1], None)

    step(qi, bias_ref[0], _causal_mask(blk))

    lam = scal_ref[0]
    out_scale = scal_ref[1]
    o = acc_ref[0] / l_ref[0] - lam * (acc_ref[1] / l_ref[1])
    o_ref[...] = (_rms(o, subln_ref[...]) * out_scale).astype(o_ref.dtype)


def _fox_attn_kernel(q_ref, k_ref, v_ref, cq_ref, ck_ref, o_ref, m_ref, l_ref, acc_ref):
    qi = pl.program_id(2)
    blk = q_ref.shape[0]
    qs = _split_q(q_ref[...])
    cq = cq_ref[...]
    _attn_init(m_ref, l_ref, acc_ref)

    def step(j, mask):
        start = pl.multiple_of(j * blk, blk)
        k_blk = k_ref[pl.ds(start, blk), :]
        v_blk = v_ref[pl.ds(start, blk), :]
        for idx in range(2):
            s = _dot_nt(qs[idx], k_blk)
            s = s + cq[:, idx:idx + 1] - ck_ref[idx:idx + 1, pl.ds(start, blk)]
            if mask is not None:
                s = jnp.where(mask, s, NEG_BIG)
            _softmax_step(s, v_blk, idx, m_ref, l_ref, acc_ref)

    def body(j, carry):
        step(j, None)
        return carry

    lax.fori_loop(0, qi, body, 0)
    step(qi, _causal_mask(blk))

    lane = lax.broadcasted_iota(jnp.int32, (blk, LANES), 1)
    o = jnp.where(lane < HEAD_DIM, acc_ref[0] / l_ref[0], acc_ref[1] / l_ref[1])
    o_ref[...] = o.astype(o_ref.dtype)


def _attn_scratch(blk):
    return [pltpu.VMEM((2, blk, 1), F32), pltpu.VMEM((2, blk, 1), F32),
            pltpu.VMEM((2, blk, LANES), F32)]


def _qkv_specs(blk, seq, qcol):
    return [pl.BlockSpec((None, blk, LANES), lambda b, s, i: (b, i, qcol + s)),
            pl.BlockSpec((None, seq, LANES), lambda b, s, i: (b, 0, qcol + 4 + s)),
            pl.BlockSpec((None, seq, LANES), lambda b, s, i: (b, 0, qcol + 8 + s))]


def _diff_attn(qkv, bias_tiles, scal, subln_row):
    bsz, seq, _ = qkv.shape
    blk = min(ATTN_BLOCK, seq)
    return pl.pallas_call(
        _diff_attn_kernel,
        grid=(bsz, N_DIFF_HEADS, seq // blk),
        in_specs=[pl.BlockSpec(memory_space=pltpu.SMEM)] + _qkv_specs(blk, seq, 0) + [
            pl.BlockSpec((None, 2, blk, blk), lambda b, s, i: (s, 0, 0, 0)),
            _const_spec((1, LANES))],
        out_specs=pl.BlockSpec((None, blk, LANES), lambda b, s, i: (b, i, s)),
        out_shape=jax.ShapeDtypeStruct((bsz, seq, N_DIFF_HEADS * LANES), BF16),
        scratch_shapes=_attn_scratch(blk),
        compiler_params=_params(3),
        name="diff_attn",
    )(scal, qkv, qkv, qkv, bias_tiles, subln_row)


def _fox_attn(qkv, cq, ck):
    bsz, seq, _ = qkv.shape
    blk = min(ATTN_BLOCK, seq)
    return pl.pallas_call(
        _fox_attn_kernel,
        grid=(bsz, N_FOX_HEADS // 2, seq // blk),
        in_specs=_qkv_specs(blk, seq, 12) + [
            pl.BlockSpec((None, None, blk, 2), lambda b, s, i: (b, s, i, 0)),
            pl.BlockSpec((None, None, 2, seq), lambda b, s, i: (b, s, 0, 0))],
        out_specs=pl.BlockSpec((None, blk, LANES), lambda b, s, i: (b, i, s)),
        out_shape=jax.ShapeDtypeStruct((bsz, seq, N_FOX_HEADS * HEAD_DIM), BF16),
        scratch_shapes=_attn_scratch(blk),
        compiler_params=_params(3),
        name="fox_attn",
    )(qkv, qkv, qkv, cq, ck)


def _outproj_kernel(x_ref, od_ref, of_ref, wd_ref, wf_ref, o_ref):
    o_ref[...] = x_ref[...] + _dot(od_ref[...], wd_ref[...]) + _dot(of_ref[...], wf_ref[...])


def _outproj(x, od, of, w_d, w_f):
    t = x.shape[0]
    tm = min(TOKEN_TILE, t)
    half = od.shape[1]
    return pl.pallas_call(
        _outproj_kernel,
        grid=(t // tm,),
        in_specs=[pl.BlockSpec((tm, D_MODEL), lambda i: (i, 0)),
                  pl.BlockSpec((tm, half), lambda i: (i, 0)),
                  pl.BlockSpec((tm, half), lambda i: (i, 0)),
                  _const_spec((half, D_MODEL)), _const_spec((half, D_MODEL))],
        out_specs=pl.BlockSpec((tm, D_MODEL), lambda i: (i, 0)),
        out_shape=jax.ShapeDtypeStruct((t, D_MODEL), F32),
        compiler_params=_params(1),
        name="attn_outproj",
    )(x, od, of, w_d, w_f)


def _rel_bias_by_distance(rel_bias, n_dist):
    n = jnp.arange(n_dist, dtype=jnp.int32)
    max_exact = N_REL_BUCKETS // 2
    nf = jnp.maximum(n, 1).astype(F32)
    large = max_exact + (jnp.log(nf / max_exact) / math.log(REL_MAX_DIST / max_exact)
                         * (N_REL_BUCKETS - max_exact)).astype(jnp.int32)
    large = jnp.minimum(large, N_REL_BUCKETS - 1)
    return rel_bias[jnp.where(n < max_exact, n, large)]


def _attn_layer(x, h, bsz, seq, layer_idx, w_in, w_out, fg_bias, dq_g, dk_g, lq1, lk1, lq2, lk2,
                subln_g, fq_g, fk_g, rel_bias):
    blk = min(ATTN_BLOCK, seq)
    assert blk >= REL_MAX_DIST, "bias tiles assume the relative bias saturates within one block"
    scale = HEAD_DIM ** -0.5
    w_qkv = w_in[:, :QKV_COLS].astype(BF16)
    w_f = jnp.pad(w_in[:, QKV_COLS:], ((0, 0), (0, LANES - N_FOX_HEADS))).astype(BF16)
    fgb = jnp.pad(fg_bias.astype(F32), (0, LANES - N_FOX_HEADS)).reshape(1, LANES)
    ones = jnp.ones((512,), F32)
    gains = jnp.stack([jnp.tile(dq_g, 8) * scale, jnp.tile(dk_g, 8), ones,
                       jnp.tile(fq_g, 8) * scale, jnp.tile(fk_g, 8), ones]).astype(F32)
    qkv, cum = _inproj(h, w_qkv, w_f, gains, fgb, seq)
    qkv = qkv.reshape(bsz, seq, QKV_COLS)

    lam_init = 0.8 - 0.6 * math.exp(-0.3 * layer_idx)
    lam = (jnp.exp(jnp.sum(lq1.astype(F32) * lk1.astype(F32)))
           - jnp.exp(jnp.sum(lq2.astype(F32) * lk2.astype(F32))) + lam_init)
    scal = jnp.stack([lam, jnp.asarray(1.0 - lam_init, F32)]).astype(F32)
    by_dist = _rel_bias_by_distance(rel_bias.astype(F32), 2 * blk)
    by_dist = by_dist - by_dist[2 * blk - 1]
    r = jnp.arange(blk)
    dist = r[:, None] - r[None, :]
    tiles = jnp.stack([by_dist[jnp.maximum(dist, 0)], by_dist[dist + blk]])
    bias_tiles = tiles.transpose(3, 0, 1, 2)
    od = _diff_attn(qkv, bias_tiles, scal, subln_g.astype(F32).reshape(1, LANES))

    cum = cum.reshape(bsz, seq, LANES)[:, :, :N_FOX_HEADS]
    cq = cum.reshape(bsz, seq, N_FOX_HEADS // 2, 2).transpose(0, 2, 1, 3)
    ck = cq.transpose(0, 1, 3, 2)
    of = _fox_attn(qkv, cq, ck)

    w_o = w_out.astype(BF16)
    half = N_DIFF_HEADS * LANES
    return _outproj(x, od.reshape(bsz * seq, half), of.reshape(bsz * seq, half),
                    w_o[:half], w_o[half:])


def _s5_kernel(n_chunks, bsz, u_ref, t_ref, p_ref, q_ref, a_ref, y_ref, z_ref, xp_ref):
    u = u_ref[...]
    half = u.shape[1] // 2
    z_ref[...] = _dot(u, p_ref[...])
    a_re = jnp.broadcast_to(a_ref[0:1, :], (bsz, LANES))
    a_im = jnp.broadcast_to(a_ref[1:2, :], (bsz, LANES))

    def scan(c, carry):
        x_re, x_im = carry
        rows = pl.ds(pl.multiple_of(c * bsz, bsz), bsz)
        xp_ref[rows, :LANES] = x_re
        xp_ref[rows, LANES:] = x_im
        z = z_ref[rows, :]
        return (a_re * x_re - a_im * x_im + z[:, :LANES],
                a_re * x_im + a_im * x_re + z[:, LANES:])

    zero = jnp.zeros((bsz, LANES), F32)
    lax.fori_loop(0, n_chunks, scan, (zero, zero))
    inter = _dot(xp_ref[...].astype(BF16), q_ref[...])
    y_ref[:, :half] = _dot(u[:, :half], t_ref[0]) + inter[:, :half]
    y_ref[:, half:] = _dot(u[:, half:], t_ref[1]) + inter[:, half:]


def _s5_mats(a_re, a_im, log_step, b_re, b_im, c_re, c_im):
    hp = lax.Precision.HIGHEST
    L = S5_CHUNK
    step = jnp.exp(log_step.astype(F32))[:, None]
    ar, ai = a_re.astype(F32), a_im.astype(F32)
    mag = jnp.exp(ar * step)
    lr, li = mag * jnp.cos(ai * step), mag * jnp.sin(ai * step)
    den = ar * ar + ai * ai
    cr = ((lr - 1.0) * ar + li * ai) / den
    ci = (li * ar - (lr - 1.0) * ai) / den
    br, bi = b_re.astype(F32), b_im.astype(F32)
    bbr = cr[..., None] * br - ci[..., None] * bi
    bbi = cr[..., None] * bi + ci[..., None] * br
    pr, pi = [jnp.ones_like(lr)], [jnp.zeros_like(li)]
    for _ in range(L):
        pr, pi = pr + [pr[-1] * lr - pi[-1] * li], pi + [pr[-1] * li + pi[-1] * lr]
    pr, pi = jnp.stack(pr), jnp.stack(pi)
    wr = pr[:L, :, :, None] * bbr - pi[:L, :, :, None] * bbi
    wi = pr[:L, :, :, None] * bbi + pi[:L, :, :, None] * bbr
    cre, cim = c_re.astype(F32), c_im.astype(F32)
    kern = (jnp.einsum('gcp,ngpd->ngcd', cre, wr, precision=hp)
            - jnp.einsum('gcp,ngpd->ngcd', cim, wi, precision=hp))
    s = jnp.arange(L)
    lag = s[None, :] - s[:, None]
    tk = kern[jnp.maximum(lag, 0)]
    tk = jnp.where((lag >= 0)[:, :, None, None, None], tk, 0.0)
    t_mat = tk.transpose(2, 0, 4, 1, 3).reshape(S5_GROUPS, L * S5_GROUP, L * S5_GROUP)
    p_re = wr[::-1].transpose(1, 0, 3, 2).reshape(S5_GROUPS, L * S5_GROUP, S5_STATE)
    p_im = wi[::-1].transpose(1, 0, 3, 2).reshape(S5_GROUPS, L * S5_GROUP, S5_STATE)
    q_re = (cre[None] * pr[1:, :, None, :] - cim[None] * pi[1:, :, None, :])
    q_im = -(cre[None] * pi[1:, :, None, :] + cim[None] * pr[1:, :, None, :])
    q_re = q_re.transpose(1, 3, 0, 2).reshape(S5_GROUPS, S5_STATE, L * S5_GROUP)
    q_im = q_im.transpose(1, 3, 0, 2).reshape(S5_GROUPS, S5_STATE, L * S5_GROUP)
    return t_mat, p_re, p_im, q_re, q_im, pr[L], pi[L]


def _pair_blockdiag(m):
    g, r, c = m.shape
    m = m.reshape(g // 2, 2, r, c)
    z = jnp.zeros_like(m[:, 0])
    return jnp.concatenate([jnp.concatenate([m[:, 0], z], axis=2),
                            jnp.concatenate([z, m[:, 1]], axis=2)], axis=1)


def _s5_core(h, bsz, seq, a_re, a_im, log_step, b_re, b_im, c_re, c_im):
    L = S5_CHUNK
    n_chunks = seq // L
    pairs = S5_GROUPS // 2
    width = 2 * L * S5_GROUP
    t_mat, p_re, p_im, q_re, q_im, a16r, a16i = _s5_mats(a_re, a_im, log_step, b_re, b_im,
                                                          c_re, c_im)
    p_pair = jnp.concatenate([_pair_blockdiag(p_re), _pair_blockdiag(p_im)], axis=2)
    q_pair = jnp.concatenate([_pair_blockdiag(q_re), _pair_blockdiag(q_im)], axis=1)
    a_pair = jnp.stack([a16r.reshape(pairs, 2 * S5_STATE), a16i.reshape(pairs, 2 * S5_STATE)],
                       axis=1)
    u = h.reshape(bsz, n_chunks, L, pairs, 2, S5_GROUP).transpose(3, 1, 0, 4, 2, 5)
    u = u.reshape(pairs, n_chunks * bsz, width)
    rows = n_chunks * bsz
    y = pl.pallas_call(
        functools.partial(_s5_kernel, n_chunks, bsz),
        grid=(pairs,),
        in_specs=[pl.BlockSpec((None, rows, width), lambda g: (g, 0, 0)),
                  pl.BlockSpec((2, width // 2, width // 2), lambda g: (g, 0, 0)),
                  pl.BlockSpec((None, width, 4 * S5_STATE), lambda g: (g, 0, 0)),
                  pl.BlockSpec((None, 4 * S5_STATE, width), lambda g: (g, 0, 0)),
                  pl.BlockSpec((None, 2, 2 * S5_STATE), lambda g: (g, 0, 0))],
        out_specs=pl.BlockSpec((None, rows, width), lambda g: (g, 0, 0)),
        out_shape=jax.ShapeDtypeStruct((pairs, rows, width), F32),
        scratch_shapes=[pltpu.VMEM((rows, 4 * S5_STATE), F32),
                        pltpu.VMEM((rows, 4 * S5_STATE), F32)],
        compiler_params=_params(1),
        name="s5_core",
    )(u, t_mat.astype(BF16), p_pair.astype(BF16), q_pair.astype(BF16), a_pair)
    y = y.reshape(pairs, n_chunks, bsz, 2, L, S5_GROUP).transpose(2, 1, 4, 0, 3, 5)
    return y.reshape(bsz * seq, D_MODEL)


def _glu_kernel(x_ref, y_ref, g_ref, d_ref, wa_ref, wb_ref, o_ref):
    x = x_ref[...]
    y = y_ref[...] + d_ref[...] * _rms(x, g_ref[...])
    y = jax.nn.gelu(y).astype(BF16)
    o_ref[...] = x + _dot(y, wa_ref[...]) * jax.nn.sigmoid(_dot(y, wb_ref[...]))


def _s5_layer(x, h, bsz, seq, mix_g, a_re, a_im, log_step, b_re, b_im, c_re, c_im, d_skip,
              w_a, w_b):
    t = x.shape[0]
    tm = min(TOKEN_TILE, t)
    y = _s5_core(h, bsz, seq, a_re, a_im, log_step, b_re, b_im, c_re, c_im)
    row = pl.BlockSpec((tm, D_MODEL), lambda i: (i, 0))
    return pl.pallas_call(
        _glu_kernel,
        grid=(t // tm,),
        in_specs=[row, row, _const_spec((1, D_MODEL)), _const_spec((1, D_MODEL)),
                  _const_spec((D_MODEL, D_MODEL)), _const_spec((D_MODEL, D_MODEL))],
        out_specs=row,
        out_shape=jax.ShapeDtypeStruct((t, D_MODEL), F32),
        compiler_params=_params(1),
        name="s5_glu",
    )(x, y, mix_g.astype(F32).reshape(1, D_MODEL), d_skip.astype(F32).reshape(1, D_MODEL),
      w_a.astype(BF16), w_b.astype(BF16))


def kernel(x, ffn1_norm, ffn1_gate, ffn1_up, ffn1_down, mix_norm, ffn2_norm, ffn2_gate, ffn2_up, ffn2_down, attn_w_in, attn_w_out, fg_bias, diff_q_norm, diff_k_norm, diff_lambda_q1, diff_lambda_k1, diff_lambda_q2, diff_lambda_k2, diff_subln, fox_q_norm, fox_k_norm, rel_bias, s5_a_re, s5_a_im, s5_log_step, s5_b_re, s5_b_im, s5_c_re, s5_c_im, s5_d, s5_glu_a, s5_glu_b):
    bsz, seq, _ = x.shape
    depth = ffn1_norm.shape[0]
    xt = x.reshape(bsz * seq, D_MODEL).astype(F32)
    for i in range(depth):
        xt, h = _ffn(xt, ffn1_norm[i], ffn1_gate[i].astype(BF16), ffn1_up[i].astype(BF16),
                     ffn1_down[i].astype(BF16), g_next=mix_norm[i])
        j = i // 2
        if i % 2 == 0:
            xt = _attn_layer(xt, h, bsz, seq, i, attn_w_in[j], attn_w_out[j], fg_bias[j],
                             diff_q_norm[j], diff_k_norm[j], diff_lambda_q1[j], diff_lambda_k1[j],
                             diff_lambda_q2[j], diff_lambda_k2[j], diff_subln[j], fox_q_norm[j],
                             fox_k_norm[j], rel_bias)
        else:
            xt = _s5_layer(xt, h, bsz, seq, mix_norm[i], s5_a_re[j], s5_a_im[j], s5_log_step[j],
                           s5_b_re[j], s5_b_im[j], s5_c_re[j], s5_c_im[j], s5_d[j],
                           s5_glu_a[j], s5_glu_b[j])
        xt, _ = _ffn(xt, ffn2_norm[i], ffn2_gate[i].astype(BF16), ffn2_up[i].astype(BF16),
                     ffn2_down[i].astype(BF16))
    return xt.reshape(bsz, seq, D_MODEL).astype(x.dtype)
```

```python
import functools
import math

import jax
import jax.numpy as jnp
import numpy as np
from jax import lax
from jax.experimental import pallas as pl
from jax.experimental.pallas import tpu as pltpu

D_MODEL = 1024
D_FF = 2816
HEAD_DIM = 64
N_DIFF_HEADS = 4
N_FOX_HEADS = 8
N_REL_BUCKETS = 32
REL_MAX_DIST = 128
S5_GROUP = 16
S5_GROUPS = D_MODEL // S5_GROUP
S5_STATE = 64
NORM_EPS = 1e-6

QK_COLS = 2048
V_COLS = 1024
AUG_COLS = 1024
LANES = 128
MXU_DIM = 256
VMEM_LIMIT = 56 * 1024 * 1024

TOKEN_TILE = 512
FFN_CHUNK = 1408
ATTN_Q = 256
ATTN_K = 512
S5_CHUNK = 16
NEG_BIG = -1e30
LOG2E = 1.4426950408889634

BF16 = jnp.bfloat16
F32 = jnp.float32


def _dot(a, b):
    return jnp.dot(a, b, preferred_element_type=F32)


def _dot_nt(a, b):
    return lax.dot_general(a, b, (((1,), (1,)), ((), ())), preferred_element_type=F32)


def _rms(x, g):
    ms = jnp.mean(x * x, axis=-1, keepdims=True)
    return x * lax.rsqrt(ms + NORM_EPS) * g


def _const_spec(shape):
    return pl.BlockSpec(shape, lambda *_: (0,) * len(shape))


def _params(n_axes):
    return pltpu.CompilerParams(dimension_semantics=("arbitrary",) * n_axes,
                                vmem_limit_bytes=VMEM_LIMIT)


def _split3(x):
    p0 = x.astype(BF16)
    r1 = x - p0.astype(F32)
    p1 = r1.astype(BF16)
    p2 = (r1 - p1.astype(F32)).astype(BF16)
    return p0, p1, p2


def _ffn_kernel(x_ref, g_ref, gn_ref, wg_ref, wu_ref, wd_ref, o_ref, *h_out):
    x = x_ref[...]
    h = _rms(x, g_ref[...]).astype(BF16)
    acc = None
    for c in range(D_FF // FFN_CHUNK):
        sl = slice(c * FFN_CHUNK, (c + 1) * FFN_CHUNK)
        a = _dot(h, wg_ref[:, sl])
        b = _dot(h, wu_ref[:, sl])
        act = (a * jax.nn.sigmoid(a) * b).astype(BF16)
        d = _dot(act, wd_ref[sl, :])
        acc = d if acc is None else acc + d
    xo = x + 0.5 * acc
    o_ref[...] = xo
    if h_out:
        h_out[0][...] = _rms(xo, gn_ref[...]).astype(BF16)


def _ffn(x, g, wg, wu, wd, g_next=None):
    t = x.shape[0]
    tm = min(TOKEN_TILE, t)
    emit = g_next is not None
    gn = g_next if emit else g
    row = pl.BlockSpec((tm, D_MODEL), lambda i: (i, 0))
    out_shape = [jax.ShapeDtypeStruct((t, D_MODEL), F32)]
    out_specs = [row]
    if emit:
        out_shape.append(jax.ShapeDtypeStruct((t, D_MODEL), BF16))
        out_specs.append(row)
    res = pl.pallas_call(
        _ffn_kernel,
        grid=(t // tm,),
        in_specs=[row, _const_spec((1, D_MODEL)), _const_spec((1, D_MODEL)),
                  _const_spec((D_MODEL, D_FF)), _const_spec((D_MODEL, D_FF)),
                  _const_spec((D_FF, D_MODEL))],
        out_specs=out_specs,
        out_shape=out_shape,
        compiler_params=_params(1),
        name="ffn",
    )(x, g.reshape(1, D_MODEL), gn.reshape(1, D_MODEL), wg, wu, wd)
    return (res[0], res[1]) if emit else (res[0], None)


def _inproj_kernel(tiles_per_seq, h_ref, wqk_ref, wv_ref, wf_ref, gains_ref, ones_ref, tri_ref,
                   fgb_ref, sel_ref, aug1_ref, qk_ref, vt_ref, aug_ref, carry_ref):
    i = pl.program_id(0)
    h = h_ref[...]
    ones_blk = ones_ref[...]
    for c in range(QK_COLS // 512):
        y = _dot(h, wqk_ref[:, c * 512:(c + 1) * 512])
        y2 = y * y
        hi = y2.astype(BF16)
        lo = (y2 - hi.astype(F32)).astype(BF16)
        ssq = jnp.concatenate(
            [_dot(hi[:, k * MXU_DIM:(k + 1) * MXU_DIM], ones_blk)
             + _dot(lo[:, k * MXU_DIM:(k + 1) * MXU_DIM], ones_blk)
             for k in range(512 // MXU_DIM)], axis=1)
        y = y * lax.rsqrt(ssq * (1.0 / HEAD_DIM) + NORM_EPS) * gains_ref[c:c + 1, :]
        qk_ref[:, c * 512:(c + 1) * 512] = y.astype(BF16)
    for c in range(V_COLS // 512):
        vt = _dot_nt(wv_ref[c * 512:(c + 1) * 512, :], h)
        vt_ref[c * 512:(c + 1) * 512, :] = vt.astype(BF16)

    fl = _dot(h, wf_ref[...]) + fgb_ref[...]
    logf = -(jnp.maximum(-fl, 0.0) + jnp.log1p(jnp.exp(-jnp.abs(fl))))

    @pl.when(i % tiles_per_seq == 0)
    def _():
        carry_ref[...] = jnp.zeros_like(carry_ref)

    tri = tri_ref[...]
    cum = sum(_dot(tri, p) for p in _split3(logf)) + carry_ref[...]
    carry_ref[...] = cum[cum.shape[0] - 1:, :]
    aug = sum(_dot(p, sel_ref[n]) for n, p in enumerate(_split3(cum * LOG2E))) + aug1_ref[...]
    aug_ref[...] = aug.astype(BF16)


def _decay_lane_maps():
    sel = np.zeros((3, LANES, AUG_COLS), np.float32)
    ones = np.zeros((1, AUG_COLS), np.float32)
    k_off = AUG_COLS // 2
    for head in range(N_FOX_HEADS):
        base = (head // 2) * LANES + (head % 2) * 6
        for n in range(3):
            sel[n, head, base + n] = 1.0
            sel[n, head, k_off + base + 3 + n] = -1.0
            ones[0, base + 3 + n] = 1.0
            ones[0, k_off + base + n] = 1.0
    return jnp.asarray(sel, BF16), jnp.asarray(ones, F32)


def _inproj(h, w_qk, w_v, w_f, gains, fg_bias_row, bsz, seq):
    t = h.shape[0]
    tm = min(TOKEN_TILE, seq)
    tps = seq // tm
    ones_blk = jnp.kron(jnp.eye(MXU_DIM // HEAD_DIM, dtype=F32),
                        jnp.ones((HEAD_DIM, HEAD_DIM), F32)).astype(BF16)
    tri = jnp.tril(jnp.ones((tm, tm), F32)).astype(BF16)
    sel, aug_ones = _decay_lane_maps()
    return pl.pallas_call(
        functools.partial(_inproj_kernel, tps),
        grid=(t // tm,),
        in_specs=[pl.BlockSpec((tm, D_MODEL), lambda i: (i, 0)),
                  _const_spec((D_MODEL, QK_COLS)), _const_spec((D_MODEL, V_COLS)),
                  _const_spec((D_MODEL, LANES)),
                  _const_spec((QK_COLS // 512, 512)), _const_spec((MXU_DIM, MXU_DIM)),
                  _const_spec((tm, tm)), _const_spec((1, LANES)),
                  _const_spec((3, LANES, AUG_COLS)), _const_spec((1, AUG_COLS))],
        out_specs=[pl.BlockSpec((tm, QK_COLS), lambda i: (i, 0)),
                   pl.BlockSpec((None, V_COLS, tm), lambda i: (i // tps, 0, i % tps)),
                   pl.BlockSpec((tm, AUG_COLS), lambda i: (i, 0))],
        out_shape=[jax.ShapeDtypeStruct((t, QK_COLS), BF16),
                   jax.ShapeDtypeStruct((bsz, V_COLS, seq), BF16),
                   jax.ShapeDtypeStruct((t, AUG_COLS), BF16)],
        scratch_shapes=[pltpu.VMEM((1, LANES), F32)],
        compiler_params=_params(1),
        name="attn_inproj",
    )(h, w_qk, w_v, w_f, gains, ones_blk, tri, fg_bias_row, sel, aug_ones)


def _attn_sweep(q_both, k_ref, kaug_ref, vt_ref, bias_ref, n_far, m_ref, l_ref, acc_ref):
    qi = pl.program_id(2)
    tq = q_both.shape[0] // 2
    m_ref[...] = jnp.full_like(m_ref, NEG_BIG)
    l_ref[...] = jnp.zeros_like(l_ref)
    acc_ref[...] = jnp.zeros_like(acc_ref)

    def step(start, size, bias, masked):
        rows = pl.ds(pl.multiple_of(start, tq), size)
        k_blk = k_ref[rows, :]
        if kaug_ref is not None:
            k_blk = jnp.concatenate([k_blk, kaug_ref[rows, :]], axis=1)
        s = _dot_nt(k_blk, q_both)
        if bias is not None:
            s = s + bias
        if masked:
            key = lax.broadcasted_iota(jnp.int32, s.shape, 0)
            qry = lax.broadcasted_iota(jnp.int32, s.shape, 1) & (tq - 1)
            s = jnp.where(key <= qry, s, NEG_BIG)
        m_prev = m_ref[...]
        m_new = jnp.maximum(m_prev, jnp.max(s, axis=0, keepdims=True))
        alpha = jnp.exp2(m_prev - m_new)
        p = jnp.exp2(s - m_new)
        l_ref[...] = alpha * l_ref[...] + jnp.sum(p, axis=0, keepdims=True)
        acc_ref[...] = alpha * acc_ref[...] + _dot(vt_ref[:, rows], p.astype(BF16))
        m_ref[...] = m_new

    ratio = ATTN_K // tq

    def far(j, carry):
        step(j * ATTN_K, ATTN_K, None, False)
        return carry

    lax.fori_loop(0, n_far // ratio, far, 0)
    for r in range(ratio - 1):
        @pl.when(n_far % ratio > r)
        def _():
            step((n_far // ratio) * ATTN_K + r * tq, tq, None, False)

    if bias_ref is not None:
        @pl.when(qi >= 1)
        def _():
            step((qi - 1) * tq, tq, bias_ref[1], False)
    step(qi * tq, tq, None if bias_ref is None else bias_ref[0], True)
    tq_sl = (slice(None), slice(0, tq)), (slice(None), slice(tq, 2 * tq))
    return [acc_ref[sl] / l_ref[sl] for sl in tq_sl]


def _diff_attn_kernel(scal_ref, q_ref, k_ref, vt_ref, bias_ref, subln_ref, o_ref,
                      m_ref, l_ref, acc_ref):
    qi = pl.program_id(2)
    q = q_ref[...]
    lane = lax.broadcasted_iota(jnp.int32, q.shape, 1)
    zero = jnp.zeros_like(q)
    q_both = jnp.concatenate([jnp.where(lane < HEAD_DIM, q, zero),
                              jnp.where(lane < HEAD_DIM, zero, q)], axis=0)
    o1, o2 = _attn_sweep(q_both, k_ref, None, vt_ref, bias_ref, jnp.maximum(qi - 1, 0),
                         m_ref, l_ref, acc_ref)
    lam = scal_ref[0]
    out_scale = scal_ref[1]
    o = o1 - lam * o2
    ms = jnp.mean(o * o, axis=0, keepdims=True)
    o = o * lax.rsqrt(ms + NORM_EPS) * subln_ref[...] * out_scale
    o_ref[...] = o.T.astype(o_ref.dtype)


def _fox_attn_kernel(q_ref, qaug_ref, k_ref, kaug_ref, vt_ref, o_ref, m_ref, l_ref, acc_ref):
    qi = pl.program_id(2)
    q = jnp.concatenate([q_ref[...], qaug_ref[...]], axis=1)
    lane = lax.broadcasted_iota(jnp.int32, q.shape, 1)
    zero = jnp.zeros_like(q)
    in_a = (lane < HEAD_DIM) | ((lane >= LANES) & (lane < LANES + 6))
    in_b = ((lane >= HEAD_DIM) & (lane < LANES)) | ((lane >= LANES + 6) & (lane < LANES + 12))
    q_both = jnp.concatenate([jnp.where(in_a, q, zero), jnp.where(in_b, q, zero)], axis=0)
    o1, o2 = _attn_sweep(q_both, k_ref, kaug_ref, vt_ref, None, qi, m_ref, l_ref, acc_ref)
    row = lax.broadcasted_iota(jnp.int32, o1.shape, 0)
    o_ref[...] = jnp.where(row < HEAD_DIM, o1, o2).T.astype(o_ref.dtype)


def _attn_scratch(tq):
    return [pltpu.VMEM((1, 2 * tq), F32), pltpu.VMEM((1, 2 * tq), F32),
            pltpu.VMEM((LANES, 2 * tq), F32)]


def _diff_attn(qk, vt, bias_tiles, scal, subln_col):
    bsz, seq, _ = qk.shape
    tq = min(ATTN_Q, seq)
    return pl.pallas_call(
        _diff_attn_kernel,
        grid=(bsz, N_DIFF_HEADS, seq // tq),
        in_specs=[pl.BlockSpec(memory_space=pltpu.SMEM),
                  pl.BlockSpec((None, tq, LANES), lambda b, s, i: (b, i, s)),
                  pl.BlockSpec((None, seq, LANES), lambda b, s, i: (b, 0, 4 + s)),
                  pl.BlockSpec((None, LANES, seq), lambda b, s, i: (b, s, 0)),
                  pl.BlockSpec((None, 2, tq, 2 * tq), lambda b, s, i: (s, 0, 0, 0)),
                  _const_spec((LANES, 1))],
        out_specs=pl.BlockSpec((None, tq, LANES), lambda b, s, i: (b, i, s)),
        out_shape=jax.ShapeDtypeStruct((bsz, seq, N_DIFF_HEADS * LANES), BF16),
        scratch_shapes=_attn_scratch(tq),
        compiler_params=_params(3),
        name="diff_attn",
    )(scal, qk, qk, vt, bias_tiles, subln_col)


def _fox_attn(qk, aug, vt):
    bsz, seq, _ = qk.shape
    tq = min(ATTN_Q, seq)
    return pl.pallas_call(
        _fox_attn_kernel,
        grid=(bsz, N_FOX_HEADS // 2, seq // tq),
        in_specs=[pl.BlockSpec((None, tq, LANES), lambda b, s, i: (b, i, 8 + s)),
                  pl.BlockSpec((None, tq, LANES), lambda b, s, i: (b, i, s)),
                  pl.BlockSpec((None, seq, LANES), lambda b, s, i: (b, 0, 12 + s)),
                  pl.BlockSpec((None, seq, LANES), lambda b, s, i: (b, 0, 4 + s)),
                  pl.BlockSpec((None, LANES, seq), lambda b, s, i: (b, 4 + s, 0))],
        out_specs=pl.BlockSpec((None, tq, LANES), lambda b, s, i: (b, i, s)),
        out_shape=jax.ShapeDtypeStruct((bsz, seq, N_FOX_HEADS * HEAD_DIM), BF16),
        scratch_shapes=_attn_scratch(tq),
        compiler_params=_params(3),
        name="fox_attn",
    )(qk, aug, qk, aug, vt)


def _outproj_kernel(x_ref, od_ref, of_ref, wd_ref, wf_ref, o_ref):
    o_ref[...] = x_ref[...] + _dot(od_ref[...], wd_ref[...]) + _dot(of_ref[...], wf_ref[...])


def _outproj(x, od, of, w_d, w_f):
    t = x.shape[0]
    tm = min(TOKEN_TILE, t)
    half = od.shape[1]
    return pl.pallas_call(
        _outproj_kernel,
        grid=(t // tm,),
        in_specs=[pl.BlockSpec((tm, D_MODEL), lambda i: (i, 0)),
                  pl.BlockSpec((tm, half), lambda i: (i, 0)),
                  pl.BlockSpec((tm, half), lambda i: (i, 0)),
                  _const_spec((half, D_MODEL)), _const_spec((half, D_MODEL))],
        out_specs=pl.BlockSpec((tm, D_MODEL), lambda i: (i, 0)),
        out_shape=jax.ShapeDtypeStruct((t, D_MODEL), F32),
        compiler_params=_params(1),
        name="attn_outproj",
    )(x, od, of, w_d, w_f)


def _rel_bias_by_distance(rel_bias, n_dist):
    n = jnp.arange(n_dist, dtype=jnp.int32)
    max_exact = N_REL_BUCKETS // 2
    nf = jnp.maximum(n, 1).astype(F32)
    large = max_exact + (jnp.log(nf / max_exact) / math.log(REL_MAX_DIST / max_exact)
                         * (N_REL_BUCKETS - max_exact)).astype(jnp.int32)
    large = jnp.minimum(large, N_REL_BUCKETS - 1)
    return rel_bias[jnp.where(n < max_exact, n, large)]


def _toeplitz(w, n):
    heads, period = w.shape
    flat = jnp.tile(w, (1, n))[:, :n * (period - 1)]
    return flat.reshape(heads, n, period - 1)[:, :, :n]


def _diff_bias_tiles(rel_bias, tq):
    by_dist = _rel_bias_by_distance(rel_bias.astype(F32), 2 * tq)
    by_dist = ((by_dist - by_dist[2 * tq - 1]) * LOG2E).T
    diag = _toeplitz(by_dist, tq)
    prev = _toeplitz(jnp.roll(by_dist, -tq, axis=1), tq)
    tiles = jnp.stack([diag, prev], axis=1)
    return jnp.concatenate([tiles, tiles], axis=-1)


def _attn_layer(x, h, bsz, seq, layer_idx, w_in, w_out, fg_bias, dq_g, dk_g, lq1, lk1, lq2, lk2,
                subln_g, fq_g, fk_g, bias_tiles):
    tq = min(ATTN_Q, seq)
    q_scale = HEAD_DIM ** -0.5 * LOG2E
    w_bf = w_in.astype(BF16)
    w_qk = jnp.concatenate([w_bf[:, 0:1024], w_bf[:, 1536:2560]], axis=1)
    w_v = jnp.concatenate([w_bf[:, 1024:1536], w_bf[:, 2560:3072]], axis=1).T
    w_f = jnp.pad(w_bf[:, 3072:], ((0, 0), (0, LANES - N_FOX_HEADS)))
    fgb = jnp.pad(fg_bias.astype(F32), (0, LANES - N_FOX_HEADS)).reshape(1, LANES)
    gains = jnp.stack([jnp.tile(dq_g, 8) * q_scale, jnp.tile(dk_g, 8),
                       jnp.tile(fq_g, 8) * q_scale, jnp.tile(fk_g, 8)]).astype(F32)
    qk, vt, aug = _inproj(h, w_qk, w_v, w_f, gains, fgb, bsz, seq)
    qk = qk.reshape(bsz, seq, QK_COLS)

    lam_init = 0.8 - 0.6 * math.exp(-0.3 * layer_idx)
    lam = (jnp.exp(jnp.sum(lq1.astype(F32) * lk1.astype(F32)))
           - jnp.exp(jnp.sum(lq2.astype(F32) * lk2.astype(F32))) + lam_init)
    scal = jnp.stack([lam, jnp.asarray(1.0 - lam_init, F32)]).astype(F32)
    od = _diff_attn(qk, vt, bias_tiles, scal, subln_g.astype(F32).reshape(LANES, 1))

    of = _fox_attn(qk, aug.reshape(bsz, seq, AUG_COLS), vt)

    w_o = w_out.astype(BF16)
    half = N_DIFF_HEADS * LANES
    return _outproj(x, od.reshape(bsz * seq, half), of.reshape(bsz * seq, half),
                    w_o[:half], w_o[half:])


def _s5_kernel(n_chunks, bsz, u_ref, t_ref, p_ref, q_ref, a_ref, y_ref, z_ref, xp_ref):
    u = u_ref[...]
    half = u.shape[1] // 2
    z_ref[...] = _dot(u, p_ref[...])
    a_re = jnp.broadcast_to(a_ref[0:1, :], (bsz, LANES))
    a_im = jnp.broadcast_to(a_ref[1:2, :], (bsz, LANES))

    def scan(c, carry):
        x_re, x_im = carry
        rows = pl.ds(pl.multiple_of(c * bsz, bsz), bsz)
        xp_ref[rows, :LANES] = x_re
        xp_ref[rows, LANES:] = x_im
        z = z_ref[rows, :]
        return (a_re * x_re - a_im * x_im + z[:, :LANES],
                a_re * x_im + a_im * x_re + z[:, LANES:])

    zero = jnp.zeros((bsz, LANES), F32)
    lax.fori_loop(0, n_chunks, scan, (zero, zero))
    inter = _dot(xp_ref[...].astype(BF16), q_ref[...])
    y_ref[:, :half] = _dot(u[:, :half], t_ref[0]) + inter[:, :half]
    y_ref[:, half:] = _dot(u[:, half:], t_ref[1]) + inter[:, half:]


def _s5_mats(a_re, a_im, log_step, b_re, b_im, c_re, c_im):
    hp = lax.Precision.HIGHEST
    L = S5_CHUNK
    step = jnp.exp(log_step.astype(F32))[:, None]
    ar, ai = a_re.astype(F32), a_im.astype(F32)
    mag = jnp.exp(ar * step)
    lr, li = mag * jnp.cos(ai * step), mag * jnp.sin(ai * step)
    den = ar * ar + ai * ai
    cr = ((lr - 1.0) * ar + li * ai) / den
    ci = (li * ar - (lr - 1.0) * ai) / den
    br, bi = b_re.astype(F32), b_im.astype(F32)
    bbr = cr[..., None] * br - ci[..., None] * bi
    bbi = cr[..., None] * bi + ci[..., None] * br
    pr, pi = [jnp.ones_like(lr)], [jnp.zeros_like(li)]
    for _ in range(L):
        pr, pi = pr + [pr[-1] * lr - pi[-1] * li], pi + [pr[-1] * li + pi[-1] * lr]
    pr, pi = jnp.stack(pr), jnp.stack(pi)
    wr = pr[:L, :, :, None] * bbr - pi[:L, :, :, None] * bbi
    wi = pr[:L, :, :, None] * bbi + pi[:L, :, :, None] * bbr
    cre, cim = c_re.astype(F32), c_im.astype(F32)
    kern = (jnp.einsum('gcp,ngpd->ngcd', cre, wr, precision=hp)
            - jnp.einsum('gcp,ngpd->ngcd', cim, wi, precision=hp))
    s = jnp.arange(L)
    lag = s[None, :] - s[:, None]
    tk = kern[jnp.maximum(lag, 0)]
    tk = jnp.where((lag >= 0)[:, :, None, None, None], tk, 0.0)
    t_mat = tk.transpose(2, 0, 4, 1, 3).reshape(S5_GROUPS, L * S5_GROUP, L * S5_GROUP)
    p_re = wr[::-1].transpose(1, 0, 3, 2).reshape(S5_GROUPS, L * S5_GROUP, S5_STATE)
    p_im = wi[::-1].transpose(1, 0, 3, 2).reshape(S5_GROUPS, L * S5_GROUP, S5_STATE)
    q_re = (cre[None] * pr[1:, :, None, :] - cim[None] * pi[1:, :, None, :])
    q_im = -(cre[None] * pi[1:, :, None, :] + cim[None] * pr[1:, :, None, :])
    q_re = q_re.transpose(1, 3, 0, 2).reshape(S5_GROUPS, S5_STATE, L * S5_GROUP)
    q_im = q_im.transpose(1, 3, 0, 2).reshape(S5_GROUPS, S5_STATE, L * S5_GROUP)
    return t_mat, p_re, p_im, q_re, q_im, pr[L], pi[L]


def _pair_blockdiag(m):
    g, r, c = m.shape
    m = m.reshape(g // 2, 2, r, c)
    z = jnp.zeros_like(m[:, 0])
    return jnp.concatenate([jnp.concatenate([m[:, 0], z], axis=2),
                            jnp.concatenate([z, m[:, 1]], axis=2)], axis=1)


def _s5_core(h, bsz, seq, a_re, a_im, log_step, b_re, b_im, c_re, c_im):
    L = S5_CHUNK
    n_chunks = seq // L
    pairs = S5_GROUPS // 2
    width = 2 * L * S5_GROUP
    t_mat, p_re, p_im, q_re, q_im, a16r, a16i = _s5_mats(a_re, a_im, log_step, b_re, b_im,
                                                          c_re, c_im)
    p_pair = jnp.concatenate([_pair_blockdiag(p_re), _pair_blockdiag(p_im)], axis=2)
    q_pair = jnp.concatenate([_pair_blockdiag(q_re), _pair_blockdiag(q_im)], axis=1)
    a_pair = jnp.stack([a16r.reshape(pairs, 2 * S5_STATE), a16i.reshape(pairs, 2 * S5_STATE)],
                       axis=1)
    u = h.reshape(bsz, n_chunks, L, pairs, 2, S5_GROUP).transpose(3, 1, 0, 4, 2, 5)
    u = u.reshape(pairs, n_chunks * bsz, width)
    rows = n_chunks * bsz
    y = pl.pallas_call(
        functools.partial(_s5_kernel, n_chunks, bsz),
        grid=(pairs,),
        in_specs=[pl.BlockSpec((None, rows, width), lambda g: (g, 0, 0)),
                  pl.BlockSpec((2, width // 2, width // 2), lambda g: (g, 0, 0)),
                  pl.BlockSpec((None, width, 4 * S5_STATE), lambda g: (g, 0, 0)),
                  pl.BlockSpec((None, 4 * S5_STATE, width), lambda g: (g, 0, 0)),
                  pl.BlockSpec((None, 2, 2 * S5_STATE), lambda g: (g, 0, 0))],
        out_specs=pl.BlockSpec((None, rows, width), lambda g: (g, 0, 0)),
        out_shape=jax.ShapeDtypeStruct((pairs, rows, width), F32),
        scratch_shapes=[pltpu.VMEM((rows, 4 * S5_STATE), F32),
                        pltpu.VMEM((rows, 4 * S5_STATE), F32)],
        compiler_params=_params(1),
        name="s5_core",
    )(u, t_mat.astype(BF16), p_pair.astype(BF16), q_pair.astype(BF16), a_pair)
    y = y.reshape(pairs, n_chunks, bsz, 2, L, S5_GROUP).transpose(2, 1, 4, 0, 3, 5)
    return y.reshape(bsz * seq, D_MODEL)


def _glu_kernel(x_ref, y_ref, g_ref, d_ref, wa_ref, wb_ref, o_ref):
    x = x_ref[...]
    y = y_ref[...] + d_ref[...] * _rms(x, g_ref[...])
    y = jax.nn.gelu(y).astype(BF16)
    o_ref[...] = x + _dot(y, wa_ref[...]) * jax.nn.sigmoid(_dot(y, wb_ref[...]))


def _s5_layer(x, h, bsz, seq, mix_g, a_re, a_im, log_step, b_re, b_im, c_re, c_im, d_skip,
              w_a, w_b):
    t = x.shape[0]
    tm = min(TOKEN_TILE, t)
    y = _s5_core(h, bsz, seq, a_re, a_im, log_step, b_re, b_im, c_re, c_im)
    row = pl.BlockSpec((tm, D_MODEL), lambda i: (i, 0))
    return pl.pallas_call(
        _glu_kernel,
        grid=(t // tm,),
        in_specs=[row, row, _const_spec((1, D_MODEL)), _const_spec((1, D_MODEL)),
                  _const_spec((D_MODEL, D_MODEL)), _const_spec((D_MODEL, D_MODEL))],
        out_specs=row,
        out_shape=jax.ShapeDtypeStruct((t, D_MODEL), F32),
        compiler_params=_params(1),
        name="s5_glu",
    )(x, y, mix_g.astype(F32).reshape(1, D_MODEL), d_skip.astype(F32).reshape(1, D_MODEL),
      w_a.astype(BF16), w_b.astype(BF16))


def kernel(x, ffn1_norm, ffn1_gate, ffn1_up, ffn1_down, mix_norm, ffn2_norm, ffn2_gate, ffn2_up, ffn2_down, attn_w_in, attn_w_out, fg_bias, diff_q_norm, diff_k_norm, diff_lambda_q1, diff_lambda_k1, diff_lambda_q2, diff_lambda_k2, diff_subln, fox_q_norm, fox_k_norm, rel_bias, s5_a_re, s5_a_im, s5_log_step, s5_b_re, s5_b_im, s5_c_re, s5_c_im, s5_d, s5_glu_a, s5_glu_b):
    bsz, seq, _ = x.shape
    depth = ffn1_norm.shape[0]
    assert min(ATTN_Q, seq) >= REL_MAX_DIST, "bias tiles assume the bias saturates within a block"
    bias_tiles = _diff_bias_tiles(rel_bias, min(ATTN_Q, seq))
    xt = x.reshape(bsz * seq, D_MODEL).astype(F32)
    for i in range(depth):
        xt, h = _ffn(xt, ffn1_norm[i], ffn1_gate[i].astype(BF16), ffn1_up[i].astype(BF16),
                     ffn1_down[i].astype(BF16), g_next=mix_norm[i])
        j = i // 2
        if i % 2 == 0:
            xt = _attn_layer(xt, h, bsz, seq, i, attn_w_in[j], attn_w_out[j], fg_bias[j],
                             diff_q_norm[j], diff_k_norm[j], diff_lambda_q1[j], diff_lambda_k1[j],
                             diff_lambda_q2[j], diff_lambda_k2[j], diff_subln[j], fox_q_norm[j],
                             fox_k_norm[j], bias_tiles)
        else:
            xt = _s5_layer(xt, h, bsz, seq, mix_norm[i], s5_a_re[j], s5_a_im[j], s5_log_step[j],
                           s5_b_re[j], s5_b_im[j], s5_c_re[j], s5_c_im[j], s5_d[j],
                           s5_glu_a[j], s5_glu_b[j])
        xt, _ = _ffn(xt, ffn2_norm[i], ffn2_gate[i].astype(BF16), ffn2_up[i].astype(BF16),
                     ffn2_down[i].astype(BF16))
    return xt.reshape(bsz, seq, D_MODEL).astype(x.dtype)
```

```python
import functools
import math

import jax
import jax.numpy as jnp
import numpy as np
from jax import lax
from jax.experimental import pallas as pl
from jax.experimental.pallas import tpu as pltpu

D_MODEL = 1024
D_FF = 2816
HEAD_DIM = 64
N_DIFF_HEADS = 4
N_FOX_HEADS = 8
N_REL_BUCKETS = 32
REL_MAX_DIST = 128
S5_GROUP = 16
S5_GROUPS = D_MODEL // S5_GROUP
S5_STATE = 64
NORM_EPS = 1e-6

QK_COLS = 2048
V_COLS = 1024
AUG_COLS = 1024
LANES = 128
MXU_DIM = 256
VMEM_LIMIT = 56 * 1024 * 1024

TOKEN_TILE = 512
FFN_CHUNK = 1408
ATTN_Q = 256
ATTN_K = 512
S5_CHUNK = 16
S5_ROWS = 512
NEG_BIG = -1e30
LOG2E = 1.4426950408889634

BF16 = jnp.bfloat16
F32 = jnp.float32


def _dot(a, b):
    return jnp.dot(a, b, preferred_element_type=F32)


def _dot_nt(a, b):
    return lax.dot_general(a, b, (((1,), (1,)), ((), ())), preferred_element_type=F32)


def _rms(x, g):
    ms = jnp.mean(x * x, axis=-1, keepdims=True)
    return x * lax.rsqrt(ms + NORM_EPS) * g


def _const_spec(shape):
    return pl.BlockSpec(shape, lambda *_: (0,) * len(shape))


def _params(n_axes):
    return pltpu.CompilerParams(dimension_semantics=("arbitrary",) * n_axes,
                                vmem_limit_bytes=VMEM_LIMIT)


def _split3(x):
    p0 = x.astype(BF16)
    r1 = x - p0.astype(F32)
    p1 = r1.astype(BF16)
    p2 = (r1 - p1.astype(F32)).astype(BF16)
    return p0, p1, p2


def _ffn_kernel(x_ref, g_ref, gn_ref, wg_ref, wu_ref, wd_ref, o_ref, *h_out):
    x = x_ref[...]
    h = _rms(x, g_ref[...]).astype(BF16)
    acc = None
    for c in range(D_FF // FFN_CHUNK):
        sl = slice(c * FFN_CHUNK, (c + 1) * FFN_CHUNK)
        a = _dot(h, wg_ref[:, sl])
        b = _dot(h, wu_ref[:, sl])
        act = (a * jax.nn.sigmoid(a) * b).astype(BF16)
        d = _dot(act, wd_ref[sl, :])
        acc = d if acc is None else acc + d
    xo = x + 0.5 * acc
    o_ref[...] = xo
    if h_out:
        h_out[0][...] = _rms(xo, gn_ref[...]).astype(BF16)


def _ffn(x, g, wg, wu, wd, g_next=None):
    t = x.shape[0]
    tm = min(TOKEN_TILE, t)
    emit = g_next is not None
    gn = g_next if emit else g
    row = pl.BlockSpec((tm, D_MODEL), lambda i: (i, 0))
    out_shape = [jax.ShapeDtypeStruct((t, D_MODEL), F32)]
    out_specs = [row]
    if emit:
        out_shape.append(jax.ShapeDtypeStruct((t, D_MODEL), BF16))
        out_specs.append(row)
    res = pl.pallas_call(
        _ffn_kernel,
        grid=(t // tm,),
        in_specs=[row, _const_spec((1, D_MODEL)), _const_spec((1, D_MODEL)),
                  _const_spec((D_MODEL, D_FF)), _const_spec((D_MODEL, D_FF)),
                  _const_spec((D_FF, D_MODEL))],
        out_specs=out_specs,
        out_shape=out_shape,
        compiler_params=_params(1),
        name="ffn",
    )(x, g.reshape(1, D_MODEL), gn.reshape(1, D_MODEL), wg, wu, wd)
    return (res[0], res[1]) if emit else (res[0], None)


def _inproj_kernel(tiles_per_seq, h_ref, wqk_ref, wv_ref, wf_ref, gains_ref, ones_ref, tri_ref,
                   fgb_ref, sel_ref, aug1_ref, qk_ref, vt_ref, aug_ref, carry_ref):
    i = pl.program_id(0)
    h = h_ref[...]
    ones_blk = ones_ref[...]
    for c in range(QK_COLS // 512):
        y = _dot(h, wqk_ref[:, c * 512:(c + 1) * 512])
        y2 = y * y
        hi = y2.astype(BF16)
        lo = (y2 - hi.astype(F32)).astype(BF16)
        ssq = jnp.concatenate(
            [_dot(hi[:, k * MXU_DIM:(k + 1) * MXU_DIM], ones_blk)
             + _dot(lo[:, k * MXU_DIM:(k + 1) * MXU_DIM], ones_blk)
             for k in range(512 // MXU_DIM)], axis=1)
        y = y * lax.rsqrt(ssq * (1.0 / HEAD_DIM) + NORM_EPS) * gains_ref[c:c + 1, :]
        qk_ref[:, c * 512:(c + 1) * 512] = y.astype(BF16)
    for c in range(V_COLS // 512):
        vt = _dot_nt(wv_ref[c * 512:(c + 1) * 512, :], h)
        vt_ref[c * 512:(c + 1) * 512, :] = vt.astype(BF16)

    fl = _dot(h, wf_ref[...]) + fgb_ref[...]
    logf = -(jnp.maximum(-fl, 0.0) + jnp.log1p(jnp.exp(-jnp.abs(fl))))

    @pl.when(i % tiles_per_seq == 0)
    def _():
        carry_ref[...] = jnp.zeros_like(carry_ref)

    tri = tri_ref[...]
    cum = sum(_dot(tri, p) for p in _split3(logf)) + carry_ref[...]
    carry_ref[...] = cum[cum.shape[0] - 1:, :]
    aug = sum(_dot(p, sel_ref[n]) for n, p in enumerate(_split3(cum * LOG2E))) + aug1_ref[...]
    aug_ref[...] = aug.astype(BF16)


def _decay_lane_maps():
    sel = np.zeros((3, LANES, AUG_COLS), np.float32)
    ones = np.zeros((1, AUG_COLS), np.float32)
    k_off = AUG_COLS // 2
    for head in range(N_FOX_HEADS):
        base = (head // 2) * LANES + (head % 2) * 6
        for n in range(3):
            sel[n, head, base + n] = 1.0
            sel[n, head, k_off + base + 3 + n] = -1.0
            ones[0, base + 3 + n] = 1.0
            ones[0, k_off + base + n] = 1.0
    return jnp.asarray(sel, BF16), jnp.asarray(ones, F32)


def _inproj(h, w_qk, w_v, w_f, gains, fg_bias_row, bsz, seq):
    t = h.shape[0]
    tm = min(TOKEN_TILE, seq)
    tps = seq // tm
    ones_blk = jnp.kron(jnp.eye(MXU_DIM // HEAD_DIM, dtype=F32),
                        jnp.ones((HEAD_DIM, HEAD_DIM), F32)).astype(BF16)
    tri = jnp.tril(jnp.ones((tm, tm), F32)).astype(BF16)
    sel, aug_ones = _decay_lane_maps()
    return pl.pallas_call(
        functools.partial(_inproj_kernel, tps),
        grid=(t // tm,),
        in_specs=[pl.BlockSpec((tm, D_MODEL), lambda i: (i, 0)),
                  _const_spec((D_MODEL, QK_COLS)), _const_spec((D_MODEL, V_COLS)),
                  _const_spec((D_MODEL, LANES)),
                  _const_spec((QK_COLS // 512, 512)), _const_spec((MXU_DIM, MXU_DIM)),
                  _const_spec((tm, tm)), _const_spec((1, LANES)),
                  _const_spec((3, LANES, AUG_COLS)), _const_spec((1, AUG_COLS))],
        out_specs=[pl.BlockSpec((tm, QK_COLS), lambda i: (i, 0)),
                   pl.BlockSpec((None, V_COLS, tm), lambda i: (i // tps, 0, i % tps)),
                   pl.BlockSpec((tm, AUG_COLS), lambda i: (i, 0))],
        out_shape=[jax.ShapeDtypeStruct((t, QK_COLS), BF16),
                   jax.ShapeDtypeStruct((bsz, V_COLS, seq), BF16),
                   jax.ShapeDtypeStruct((t, AUG_COLS), BF16)],
        scratch_shapes=[pltpu.VMEM((1, LANES), F32)],
        compiler_params=_params(1),
        name="attn_inproj",
    )(h, w_qk, w_v, w_f, gains, ones_blk, tri, fg_bias_row, sel, aug_ones)


def _attn_sweep(q_both, k_ref, kaug_ref, vt_ref, bias_ref, n_far, m_ref, l_ref, acc_ref):
    qi = pl.program_id(2)
    tq = q_both.shape[0] // 2
    m_ref[...] = jnp.full_like(m_ref, NEG_BIG)
    l_ref[...] = jnp.zeros_like(l_ref)
    acc_ref[...] = jnp.zeros_like(acc_ref)

    def step(start, size, bias, masked):
        rows = pl.ds(pl.multiple_of(start, tq), size)
        k_blk = k_ref[rows, :]
        if kaug_ref is not None:
            k_blk = jnp.concatenate([k_blk, kaug_ref[rows, :]], axis=1)
        s = _dot_nt(k_blk, q_both)
        if bias is not None:
            s = s + bias
        if masked:
            key = lax.broadcasted_iota(jnp.int32, s.shape, 0)
            qry = lax.broadcasted_iota(jnp.int32, s.shape, 1) & (tq - 1)
            s = jnp.where(key <= qry, s, NEG_BIG)
        m_prev = m_ref[...]
        m_new = jnp.maximum(m_prev, jnp.max(s, axis=0, keepdims=True))
        alpha = jnp.exp2(m_prev - m_new)
        p = jnp.exp2(s - m_new)
        l_ref[...] = alpha * l_ref[...] + jnp.sum(p, axis=0, keepdims=True)
        acc_ref[...] = alpha * acc_ref[...] + _dot(vt_ref[:, rows], p.astype(BF16))
        m_ref[...] = m_new

    ratio = ATTN_K // tq

    def far(j, carry):
        step(j * ATTN_K, ATTN_K, None, False)
        return carry

    lax.fori_loop(0, n_far // ratio, far, 0)
    for r in range(ratio - 1):
        @pl.when(n_far % ratio > r)
        def _():
            step((n_far // ratio) * ATTN_K + r * tq, tq, None, False)

    if bias_ref is not None:
        @pl.when(qi >= 1)
        def _():
            step((qi - 1) * tq, tq, bias_ref[1], False)
    step(qi * tq, tq, None if bias_ref is None else bias_ref[0], True)
    tq_sl = (slice(None), slice(0, tq)), (slice(None), slice(tq, 2 * tq))
    return [acc_ref[sl] / l_ref[sl] for sl in tq_sl]


def _diff_attn_kernel(scal_ref, q_ref, k_ref, vt_ref, bias_ref, subln_ref, o_ref,
                      m_ref, l_ref, acc_ref):
    qi = pl.program_id(2)
    q = q_ref[...]
    lane = lax.broadcasted_iota(jnp.int32, q.shape, 1)
    zero = jnp.zeros_like(q)
    q_both = jnp.concatenate([jnp.where(lane < HEAD_DIM, q, zero),
                              jnp.where(lane < HEAD_DIM, zero, q)], axis=0)
    o1, o2 = _attn_sweep(q_both, k_ref, None, vt_ref, bias_ref, jnp.maximum(qi - 1, 0),
                         m_ref, l_ref, acc_ref)
    lam = scal_ref[0]
    out_scale = scal_ref[1]
    o = o1 - lam * o2
    ms = jnp.mean(o * o, axis=0, keepdims=True)
    o = o * lax.rsqrt(ms + NORM_EPS) * subln_ref[...] * out_scale
    o_ref[...] = o.T.astype(o_ref.dtype)


def _fox_attn_kernel(q_ref, qaug_ref, k_ref, kaug_ref, vt_ref, o_ref, m_ref, l_ref, acc_ref):
    qi = pl.program_id(2)
    q = jnp.concatenate([q_ref[...], qaug_ref[...]], axis=1)
    lane = lax.broadcasted_iota(jnp.int32, q.shape, 1)
    zero = jnp.zeros_like(q)
    in_a = (lane < HEAD_DIM) | ((lane >= LANES) & (lane < LANES + 6))
    in_b = ((lane >= HEAD_DIM) & (lane < LANES)) | ((lane >= LANES + 6) & (lane < LANES + 12))
    q_both = jnp.concatenate([jnp.where(in_a, q, zero), jnp.where(in_b, q, zero)], axis=0)
    o1, o2 = _attn_sweep(q_both, k_ref, kaug_ref, vt_ref, None, qi, m_ref, l_ref, acc_ref)
    row = lax.broadcasted_iota(jnp.int32, o1.shape, 0)
    o_ref[...] = jnp.where(row < HEAD_DIM, o1, o2).T.astype(o_ref.dtype)


def _attn_scratch(tq):
    return [pltpu.VMEM((1, 2 * tq), F32), pltpu.VMEM((1, 2 * tq), F32),
            pltpu.VMEM((LANES, 2 * tq), F32)]


def _diff_attn(qk, vt, bias_tiles, scal, subln_col):
    bsz, seq, _ = qk.shape
    tq = min(ATTN_Q, seq)
    return pl.pallas_call(
        _diff_attn_kernel,
        grid=(bsz, N_DIFF_HEADS, seq // tq),
        in_specs=[pl.BlockSpec(memory_space=pltpu.SMEM),
                  pl.BlockSpec((None, tq, LANES), lambda b, s, i: (b, i, s)),
                  pl.BlockSpec((None, seq, LANES), lambda b, s, i: (b, 0, 4 + s)),
                  pl.BlockSpec((None, LANES, seq), lambda b, s, i: (b, s, 0)),
                  pl.BlockSpec((None, 2, tq, 2 * tq), lambda b, s, i: (s, 0, 0, 0)),
                  _const_spec((LANES, 1))],
        out_specs=pl.BlockSpec((None, tq, LANES), lambda b, s, i: (b, i, s)),
        out_shape=jax.ShapeDtypeStruct((bsz, seq, N_DIFF_HEADS * LANES), BF16),
        scratch_shapes=_attn_scratch(tq),
        compiler_params=_params(3),
        name="diff_attn",
    )(scal, qk, qk, vt, bias_tiles, subln_col)


def _fox_attn(qk, aug, vt):
    bsz, seq, _ = qk.shape
    tq = min(ATTN_Q, seq)
    return pl.pallas_call(
        _fox_attn_kernel,
        grid=(bsz, N_FOX_HEADS // 2, seq // tq),
        in_specs=[pl.BlockSpec((None, tq, LANES), lambda b, s, i: (b, i, 8 + s)),
                  pl.BlockSpec((None, tq, LANES), lambda b, s, i: (b, i, s)),
                  pl.BlockSpec((None, seq, LANES), lambda b, s, i: (b, 0, 12 + s)),
                  pl.BlockSpec((None, seq, LANES), lambda b, s, i: (b, 0, 4 + s)),
                  pl.BlockSpec((None, LANES, seq), lambda b, s, i: (b, 4 + s, 0))],
        out_specs=pl.BlockSpec((None, tq, LANES), lambda b, s, i: (b, i, s)),
        out_shape=jax.ShapeDtypeStruct((bsz, seq, N_FOX_HEADS * HEAD_DIM), BF16),
        scratch_shapes=_attn_scratch(tq),
        compiler_params=_params(3),
        name="fox_attn",
    )(qk, aug, qk, aug, vt)


def _outproj_kernel(x_ref, od_ref, of_ref, wd_ref, wf_ref, o_ref):
    o_ref[...] = x_ref[...] + _dot(od_ref[...], wd_ref[...]) + _dot(of_ref[...], wf_ref[...])


def _outproj(x, od, of, w_d, w_f):
    t = x.shape[0]
    tm = min(TOKEN_TILE, t)
    half = od.shape[1]
    return pl.pallas_call(
        _outproj_kernel,
        grid=(t // tm,),
        in_specs=[pl.BlockSpec((tm, D_MODEL), lambda i: (i, 0)),
                  pl.BlockSpec((tm, half), lambda i: (i, 0)),
                  pl.BlockSpec((tm, half), lambda i: (i, 0)),
                  _const_spec((half, D_MODEL)), _const_spec((half, D_MODEL))],
        out_specs=pl.BlockSpec((tm, D_MODEL), lambda i: (i, 0)),
        out_shape=jax.ShapeDtypeStruct((t, D_MODEL), F32),
        compiler_params=_params(1),
        name="attn_outproj",
    )(x, od, of, w_d, w_f)


def _rel_bias_by_distance(rel_bias, n_dist):
    n = jnp.arange(n_dist, dtype=jnp.int32)
    max_exact = N_REL_BUCKETS // 2
    nf = jnp.maximum(n, 1).astype(F32)
    large = max_exact + (jnp.log(nf / max_exact) / math.log(REL_MAX_DIST / max_exact)
                         * (N_REL_BUCKETS - max_exact)).astype(jnp.int32)
    large = jnp.minimum(large, N_REL_BUCKETS - 1)
    return rel_bias[jnp.where(n < max_exact, n, large)]


def _toeplitz(w, n):
    heads, period = w.shape
    flat = jnp.tile(w, (1, n))[:, :n * (period - 1)]
    return flat.reshape(heads, n, period - 1)[:, :, :n]


def _diff_bias_tiles(rel_bias, tq):
    by_dist = _rel_bias_by_distance(rel_bias.astype(F32), 2 * tq)
    by_dist = ((by_dist - by_dist[2 * tq - 1]) * LOG2E).T
    diag = _toeplitz(by_dist, tq)
    prev = _toeplitz(jnp.roll(by_dist, -tq, axis=1), tq)
    tiles = jnp.stack([diag, prev], axis=1)
    return jnp.concatenate([tiles, tiles], axis=-1)


def _attn_layer(x, h, bsz, seq, layer_idx, w_in, w_out, fg_bias, dq_g, dk_g, lq1, lk1, lq2, lk2,
                subln_g, fq_g, fk_g, bias_tiles):
    tq = min(ATTN_Q, seq)
    q_scale = HEAD_DIM ** -0.5 * LOG2E
    w_bf = w_in.astype(BF16)
    w_qk = jnp.concatenate([w_bf[:, 0:1024], w_bf[:, 1536:2560]], axis=1)
    w_v = jnp.concatenate([w_bf[:, 1024:1536], w_bf[:, 2560:3072]], axis=1).T
    w_f = jnp.pad(w_bf[:, 3072:], ((0, 0), (0, LANES - N_FOX_HEADS)))
    fgb = jnp.pad(fg_bias.astype(F32), (0, LANES - N_FOX_HEADS)).reshape(1, LANES)
    gains = jnp.stack([jnp.tile(dq_g, 8) * q_scale, jnp.tile(dk_g, 8),
                       jnp.tile(fq_g, 8) * q_scale, jnp.tile(fk_g, 8)]).astype(F32)
    qk, vt, aug = _inproj(h, w_qk, w_v, w_f, gains, fgb, bsz, seq)
    qk = qk.reshape(bsz, seq, QK_COLS)

    lam_init = 0.8 - 0.6 * math.exp(-0.3 * layer_idx)
    lam = (jnp.exp(jnp.sum(lq1.astype(F32) * lk1.astype(F32)))
           - jnp.exp(jnp.sum(lq2.astype(F32) * lk2.astype(F32))) + lam_init)
    scal = jnp.stack([lam, jnp.asarray(1.0 - lam_init, F32)]).astype(F32)
    od = _diff_attn(qk, vt, bias_tiles, scal, subln_g.astype(F32).reshape(LANES, 1))

    of = _fox_attn(qk, aug.reshape(bsz, seq, AUG_COLS), vt)

    w_o = w_out.astype(BF16)
    half = N_DIFF_HEADS * LANES
    return _outproj(x, od.reshape(bsz * seq, half), of.reshape(bsz * seq, half),
                    w_o[:half], w_o[half:])


def _s5_kernel(n_chunks, *refs):
    x_refs = refs[:S5_CHUNK]
    t_ref, p_ref, q_ref, a_ref, y_ref, z_ref, xp_ref = refs[S5_CHUNK:]
    x = jnp.concatenate([r[...] for r in x_refs], axis=1)
    n_seq = x.shape[0] // n_chunks
    half = z_ref.shape[1] // 2
    z_ref[...] = _dot(x, p_ref[...])
    a_re = a_ref[0:1, :]
    a_im = a_ref[1:2, :]

    def scan(c, carry):
        out = []
        for b in range(n_seq):
            x_re, x_im = carry[2 * b], carry[2 * b + 1]
            row = pl.ds(b * n_chunks + c, 1)
            xp_ref[row, :half] = x_re
            xp_ref[row, half:] = x_im
            z = z_ref[row, :]
            out += [a_re * x_re - a_im * x_im + z[:, :half],
                    a_re * x_im + a_im * x_re + z[:, half:]]
        return tuple(out)

    zero = jnp.zeros((1, half), F32)
    lax.fori_loop(0, n_chunks, scan, (zero,) * (2 * n_seq))
    y = _dot(x, t_ref[...]) + _dot(xp_ref[...].astype(BF16), q_ref[...])
    for t in range(S5_CHUNK):
        y_ref[t] = y[:, t * LANES:(t + 1) * LANES]


def _s5_mats(a_re, a_im, log_step, b_re, b_im, c_re, c_im):
    hp = lax.Precision.HIGHEST
    L = S5_CHUNK
    step = jnp.exp(log_step.astype(F32))[:, None]
    ar, ai = a_re.astype(F32), a_im.astype(F32)
    mag = jnp.exp(ar * step)
    lr, li = mag * jnp.cos(ai * step), mag * jnp.sin(ai * step)
    den = ar * ar + ai * ai
    cr = ((lr - 1.0) * ar + li * ai) / den
    ci = (li * ar - (lr - 1.0) * ai) / den
    br, bi = b_re.astype(F32), b_im.astype(F32)
    bbr = cr[..., None] * br - ci[..., None] * bi
    bbi = cr[..., None] * bi + ci[..., None] * br
    pr, pi = [jnp.ones_like(lr)], [jnp.zeros_like(li)]
    for _ in range(L):
        pr, pi = pr + [pr[-1] * lr - pi[-1] * li], pi + [pr[-1] * li + pi[-1] * lr]
    pr, pi = jnp.stack(pr), jnp.stack(pi)
    wr = pr[:L, :, :, None] * bbr - pi[:L, :, :, None] * bbi
    wi = pr[:L, :, :, None] * bbi + pi[:L, :, :, None] * bbr
    cre, cim = c_re.astype(F32), c_im.astype(F32)
    kern = (jnp.einsum('gcp,ngpd->ngcd', cre, wr, precision=hp)
            - jnp.einsum('gcp,ngpd->ngcd', cim, wi, precision=hp))
    s = jnp.arange(L)
    lag = s[None, :] - s[:, None]
    tk = kern[jnp.maximum(lag, 0)]
    tk = jnp.where((lag >= 0)[:, :, None, None, None], tk, 0.0)
    t_mat = tk.transpose(2, 0, 4, 1, 3).reshape(S5_GROUPS, L * S5_GROUP, L * S5_GROUP)
    p_re = wr[::-1].transpose(1, 0, 3, 2).reshape(S5_GROUPS, L * S5_GROUP, S5_STATE)
    p_im = wi[::-1].transpose(1, 0, 3, 2).reshape(S5_GROUPS, L * S5_GROUP, S5_STATE)
    q_re = (cre[None] * pr[1:, :, None, :] - cim[None] * pi[1:, :, None, :])
    q_im = -(cre[None] * pi[1:, :, None, :] + cim[None] * pr[1:, :, None, :])
    q_re = q_re.transpose(1, 3, 0, 2).reshape(S5_GROUPS, S5_STATE, L * S5_GROUP)
    q_im = q_im.transpose(1, 3, 0, 2).reshape(S5_GROUPS, S5_STATE, L * S5_GROUP)
    return t_mat, p_re, p_im, q_re, q_im, pr[L], pi[L]


def _scatter_groups(m, rows_first):
    g, r, c = m.shape
    kb, gb = g // 8, 8
    eye = jnp.eye(gb, dtype=m.dtype)
    L = S5_CHUNK
    if rows_first == "tt":
        m6 = m.reshape(kb, gb, L, r // L, L, c // L)
        out = jnp.einsum('kgsatc,gh->ksgathc', m6, eye)
    elif rows_first == "ts":
        m5 = m.reshape(kb, gb, L, r // L, c)
        out = jnp.einsum('kgsap,gh->ksgahp', m5, eye)
    else:
        m5 = m.reshape(kb, gb, r, L, c // L)
        out = jnp.einsum('kgptc,gh->kgpthc', m5, eye)
    return out.reshape(kb, gb * r, gb * c)


def _s5_core(h, bsz, seq, a_re, a_im, log_step, b_re, b_im, c_re, c_im):
    L = S5_CHUNK
    n_chunks = seq // L
    rows = bsz * n_chunks
    kb = D_MODEL // LANES
    rb = min(S5_ROWS, rows)
    assert rb % n_chunks == 0
    t_mat, p_re, p_im, q_re, q_im, a16r, a16i = _s5_mats(a_re, a_im, log_step, b_re, b_im,
                                                          c_re, c_im)
    t_big = _scatter_groups(t_mat, "tt").astype(BF16)
    p_big = jnp.concatenate([_scatter_groups(p_re, "ts"), _scatter_groups(p_im, "ts")],
                            axis=2).astype(BF16)
    q_big = jnp.concatenate([_scatter_groups(q_re, "st"), _scatter_groups(q_im, "st")],
                            axis=1).astype(BF16)
    n_state = 8 * S5_STATE
    a_big = jnp.stack([a16r.reshape(kb, n_state), a16i.reshape(kb, n_state)], axis=1)
    h2 = h.reshape(rows, L * D_MODEL)
    x_specs = [pl.BlockSpec((rb, LANES), lambda k, r, t=t: (r, t * kb + k)) for t in range(L)]
    return pl.pallas_call(
        functools.partial(_s5_kernel, n_chunks),
        grid=(kb, rows // rb),
        in_specs=x_specs + [
            pl.BlockSpec((None, L * LANES, L * LANES), lambda k, r: (k, 0, 0)),
            pl.BlockSpec((None, L * LANES, 2 * n_state), lambda k, r: (k, 0, 0)),
            pl.BlockSpec((None, 2 * n_state, L * LANES), lambda k, r: (k, 0, 0)),
            pl.BlockSpec((None, 2, n_state), lambda k, r: (k, 0, 0))],
        out_specs=pl.BlockSpec((L, rb, LANES), lambda k, r: (0, r, k)),
        out_shape=jax.ShapeDtypeStruct((L, rows, D_MODEL), F32),
        scratch_shapes=[pltpu.VMEM((rb, 2 * n_state), F32), pltpu.VMEM((rb, 2 * n_state), F32)],
        compiler_params=_params(2),
        name="s5_core",
    )(*([h2] * L), t_big, p_big, q_big, a_big)


def _glu_kernel(x_ref, y_ref, g_ref, d_ref, wa_ref, wb_ref, o_ref):
    x = x_ref[...]
    y = y_ref[...] + d_ref[...] * _rms(x, g_ref[...])
    y = jax.nn.gelu(y).astype(BF16)
    o_ref[...] = x + _dot(y, wa_ref[...]) * jax.nn.sigmoid(_dot(y, wb_ref[...]))


def _s5_layer(x, h, bsz, seq, mix_g, a_re, a_im, log_step, b_re, b_im, c_re, c_im, d_skip,
              w_a, w_b):
    L = S5_CHUNK
    rows = x.shape[0] // L
    rg = min(TOKEN_TILE, rows)
    y = _s5_core(h, bsz, seq, a_re, a_im, log_step, b_re, b_im, c_re, c_im)
    tile = pl.BlockSpec((rg, D_MODEL), lambda r, t: (r, t))
    out = pl.pallas_call(
        _glu_kernel,
        grid=(rows // rg, L),
        in_specs=[tile, pl.BlockSpec((None, rg, D_MODEL), lambda r, t: (t, r, 0)),
                  _const_spec((1, D_MODEL)), _const_spec((1, D_MODEL)),
                  _const_spec((D_MODEL, D_MODEL)), _const_spec((D_MODEL, D_MODEL))],
        out_specs=tile,
        out_shape=jax.ShapeDtypeStruct((rows, L * D_MODEL), F32),
        compiler_params=_params(2),
        name="s5_glu",
    )(x.reshape(rows, L * D_MODEL), y, mix_g.astype(F32).reshape(1, D_MODEL),
      d_skip.astype(F32).reshape(1, D_MODEL), w_a.astype(BF16), w_b.astype(BF16))
    return out.reshape(rows * L, D_MODEL)


def kernel(x, ffn1_norm, ffn1_gate, ffn1_up, ffn1_down, mix_norm, ffn2_norm, ffn2_gate, ffn2_up, ffn2_down, attn_w_in, attn_w_out, fg_bias, diff_q_norm, diff_k_norm, diff_lambda_q1, diff_lambda_k1, diff_lambda_q2, diff_lambda_k2, diff_subln, fox_q_norm, fox_k_norm, rel_bias, s5_a_re, s5_a_im, s5_log_step, s5_b_re, s5_b_im, s5_c_re, s5_c_im, s5_d, s5_glu_a, s5_glu_b):
    bsz, seq, _ = x.shape
    depth = ffn1_norm.shape[0]
    assert min(ATTN_Q, seq) >= REL_MAX_DIST, "bias tiles assume the bias saturates within a block"
    bias_tiles = _diff_bias_tiles(rel_bias, min(ATTN_Q, seq))
    xt = x.reshape(bsz * seq, D_MODEL).astype(F32)
    for i in range(depth):
        xt, h = _ffn(xt, ffn1_norm[i], ffn1_gate[i].astype(BF16), ffn1_up[i].astype(BF16),
                     ffn1_down[i].astype(BF16), g_next=mix_norm[i])
        j = i // 2
        if i % 2 == 0:
            xt = _attn_layer(xt, h, bsz, seq, i, attn_w_in[j], attn_w_out[j], fg_bias[j],
                             diff_q_norm[j], diff_k_norm[j], diff_lambda_q1[j], diff_lambda_k1[j],
                             diff_lambda_q2[j], diff_lambda_k2[j], diff_subln[j], fox_q_norm[j],
                             fox_k_norm[j], bias_tiles)
        else:
            xt = _s5_layer(xt, h, bsz, seq, mix_norm[i], s5_a_re[j], s5_a_im[j], s5_log_step[j],
                           s5_b_re[j], s5_b_im[j], s5_c_re[j], s5_c_im[j], s5_d[j],
                           s5_glu_a[j], s5_glu_b[j])
        xt, _ = _ffn(xt, ffn2_norm[i], ffn2_gate[i].astype(BF16), ffn2_up[i].astype(BF16),
                     ffn2_down[i].astype(BF16))
    return xt.reshape(bsz, seq, D_MODEL).astype(x.dtype)
```

```python
import functools
import math

import jax
import jax.numpy as jnp
import numpy as np
from jax import lax
from jax.experimental import pallas as pl
from jax.experimental.pallas import tpu as pltpu

D_MODEL = 1024
D_FF = 2816
HEAD_DIM = 64
N_DIFF_HEADS = 4
N_FOX_HEADS = 8
N_REL_BUCKETS = 32
REL_MAX_DIST = 128
S5_GROUP = 16
S5_GROUPS = D_MODEL // S5_GROUP
S5_STATE = 64
NORM_EPS = 1e-6

QK_COLS = 2048
V_COLS = 1024
AUG_COLS = 1024
LANES = 128
MXU_DIM = 256
VMEM_LIMIT = 56 * 1024 * 1024

TOKEN_TILE = 512
FFN_CHUNK = 1408
ATTN_Q = 512
S5_CHUNK = 16
S5_ROWS = 512
NEG_BIG = -1e30
LOG2E = 1.4426950408889634

BF16 = jnp.bfloat16
F32 = jnp.float32


def _dot(a, b):
    return jnp.dot(a, b, preferred_element_type=F32)


def _dot_nt(a, b):
    return lax.dot_general(a, b, (((1,), (1,)), ((), ())), preferred_element_type=F32)


def _rms(x, g):
    ms = jnp.mean(x * x, axis=-1, keepdims=True)
    return x * lax.rsqrt(ms + NORM_EPS) * g


def _const_spec(shape):
    return pl.BlockSpec(shape, lambda *_: (0,) * len(shape))


def _params(n_axes):
    return pltpu.CompilerParams(dimension_semantics=("arbitrary",) * n_axes,
                                vmem_limit_bytes=VMEM_LIMIT)


def _split3(x):
    p0 = x.astype(BF16)
    r1 = x - p0.astype(F32)
    p1 = r1.astype(BF16)
    p2 = (r1 - p1.astype(F32)).astype(BF16)
    return p0, p1, p2


def _ffn_kernel(x_ref, g_ref, gn_ref, wg_ref, wu_ref, wd_ref, o_ref, *h_out):
    x = x_ref[...]
    h = _rms(x, g_ref[...]).astype(BF16)
    acc = None
    for c in range(D_FF // FFN_CHUNK):
        sl = slice(c * FFN_CHUNK, (c + 1) * FFN_CHUNK)
        a = _dot(h, wg_ref[:, sl])
        b = _dot(h, wu_ref[:, sl])
        act = (a * jax.nn.sigmoid(a) * b).astype(BF16)
        d = _dot(act, wd_ref[sl, :])
        acc = d if acc is None else acc + d
    xo = x + 0.5 * acc
    o_ref[...] = xo
    if h_out:
        h_out[0][...] = _rms(xo, gn_ref[...]).astype(BF16)


def _ffn(x, g, wg, wu, wd, g_next=None):
    t = x.shape[0]
    tm = min(TOKEN_TILE, t)
    emit = g_next is not None
    gn = g_next if emit else g
    row = pl.BlockSpec((tm, D_MODEL), lambda i: (i, 0))
    out_shape = [jax.ShapeDtypeStruct((t, D_MODEL), F32)]
    out_specs = [row]
    if emit:
        out_shape.append(jax.ShapeDtypeStruct((t, D_MODEL), BF16))
        out_specs.append(row)
    res = pl.pallas_call(
        _ffn_kernel,
        grid=(t // tm,),
        in_specs=[row, _const_spec((1, D_MODEL)), _const_spec((1, D_MODEL)),
                  _const_spec((D_MODEL, D_FF)), _const_spec((D_MODEL, D_FF)),
                  _const_spec((D_FF, D_MODEL))],
        out_specs=out_specs,
        out_shape=out_shape,
        compiler_params=_params(1),
        name="ffn",
    )(x, g.reshape(1, D_MODEL), gn.reshape(1, D_MODEL), wg, wu, wd)
    return (res[0], res[1]) if emit else (res[0], None)


def _inproj_kernel(tiles_per_seq, h_ref, wqk_ref, wv_ref, wf_ref, gains_ref, ones_ref, tri_ref,
                   fgb_ref, sel_ref, aug1_ref, qk_ref, vt_ref, aug_ref, carry_ref):
    i = pl.program_id(0)
    h = h_ref[...]
    ones_blk = ones_ref[...]
    for c in range(QK_COLS // 512):
        y = _dot(h, wqk_ref[:, c * 512:(c + 1) * 512])
        y2 = y * y
        hi = y2.astype(BF16)
        lo = (y2 - hi.astype(F32)).astype(BF16)
        ssq = jnp.concatenate(
            [_dot(hi[:, k * MXU_DIM:(k + 1) * MXU_DIM], ones_blk)
             + _dot(lo[:, k * MXU_DIM:(k + 1) * MXU_DIM], ones_blk)
             for k in range(512 // MXU_DIM)], axis=1)
        y = y * lax.rsqrt(ssq * (1.0 / HEAD_DIM) + NORM_EPS) * gains_ref[c:c + 1, :]
        qk_ref[:, c * 512:(c + 1) * 512] = y.astype(BF16)
    for c in range(V_COLS // 512):
        vt = _dot_nt(wv_ref[c * 512:(c + 1) * 512, :], h)
        vt_ref[c * 512:(c + 1) * 512, :] = vt.astype(BF16)

    fl = _dot(h, wf_ref[...]) + fgb_ref[...]
    logf = -(jnp.maximum(-fl, 0.0) + jnp.log1p(jnp.exp(-jnp.abs(fl))))

    @pl.when(i % tiles_per_seq == 0)
    def _():
        carry_ref[...] = jnp.zeros_like(carry_ref)

    tri = tri_ref[...]
    cum = sum(_dot(tri, p) for p in _split3(logf)) + carry_ref[...]
    carry_ref[...] = cum[cum.shape[0] - 1:, :]
    aug = sum(_dot(p, sel_ref[n]) for n, p in enumerate(_split3(cum * LOG2E))) + aug1_ref[...]
    aug_ref[...] = aug.astype(BF16)


def _decay_lane_maps():
    sel = np.zeros((3, LANES, AUG_COLS), np.float32)
    ones = np.zeros((1, AUG_COLS), np.float32)
    k_off = AUG_COLS // 2
    for head in range(N_FOX_HEADS):
        base = (head // 2) * LANES + (head % 2) * 6
        for n in range(3):
            sel[n, head, base + n] = 1.0
            sel[n, head, k_off + base + 3 + n] = -1.0
            ones[0, base + 3 + n] = 1.0
            ones[0, k_off + base + n] = 1.0
    return jnp.asarray(sel, BF16), jnp.asarray(ones, F32)


def _inproj(h, w_qk, w_v, w_f, gains, fg_bias_row, bsz, seq):
    t = h.shape[0]
    tm = min(TOKEN_TILE, seq)
    tps = seq // tm
    ones_blk = jnp.kron(jnp.eye(MXU_DIM // HEAD_DIM, dtype=F32),
                        jnp.ones((HEAD_DIM, HEAD_DIM), F32)).astype(BF16)
    tri = jnp.tril(jnp.ones((tm, tm), F32)).astype(BF16)
    sel, aug_ones = _decay_lane_maps()
    return pl.pallas_call(
        functools.partial(_inproj_kernel, tps),
        grid=(t // tm,),
        in_specs=[pl.BlockSpec((tm, D_MODEL), lambda i: (i, 0)),
                  _const_spec((D_MODEL, QK_COLS)), _const_spec((D_MODEL, V_COLS)),
                  _const_spec((D_MODEL, LANES)),
                  _const_spec((QK_COLS // 512, 512)), _const_spec((MXU_DIM, MXU_DIM)),
                  _const_spec((tm, tm)), _const_spec((1, LANES)),
                  _const_spec((3, LANES, AUG_COLS)), _const_spec((1, AUG_COLS))],
        out_specs=[pl.BlockSpec((tm, QK_COLS), lambda i: (i, 0)),
                   pl.BlockSpec((None, V_COLS, tm), lambda i: (i // tps, 0, i % tps)),
                   pl.BlockSpec((tm, AUG_COLS), lambda i: (i, 0))],
        out_shape=[jax.ShapeDtypeStruct((t, QK_COLS), BF16),
                   jax.ShapeDtypeStruct((bsz, V_COLS, seq), BF16),
                   jax.ShapeDtypeStruct((t, AUG_COLS), BF16)],
        scratch_shapes=[pltpu.VMEM((1, LANES), F32)],
        compiler_params=_params(1),
        name="attn_inproj",
    )(h, w_qk, w_v, w_f, gains, ones_blk, tri, fg_bias_row, sel, aug_ones)


def _attn_sweep(q_both, k_ref, kaug_ref, vt_ref, bias_ref, m_ref, l_ref, acc_ref):
    qi = pl.program_id(2)
    blk = q_both.shape[0] // 2
    m_ref[...] = jnp.full_like(m_ref, NEG_BIG)
    l_ref[...] = jnp.zeros_like(l_ref)
    acc_ref[...] = jnp.zeros_like(acc_ref)

    def run(items):
        rows = [pl.ds(pl.multiple_of(j * blk, blk), blk) for j, _, _ in items]
        scores = []
        for r in rows:
            k_blk = k_ref[r, :]
            if kaug_ref is not None:
                k_blk = jnp.concatenate([k_blk, kaug_ref[r, :]], axis=1)
            scores.append(_dot_nt(k_blk, q_both))
        probs = []
        for s, (_, bias_idx, masked) in zip(scores, items):
            if bias_idx is not None and bias_ref is not None:
                s = s + bias_ref[bias_idx]
            if masked:
                key = lax.broadcasted_iota(jnp.int32, s.shape, 0)
                qry = lax.broadcasted_iota(jnp.int32, s.shape, 1) & (blk - 1)
                s = jnp.where(key <= qry, s, NEG_BIG)
            m_prev = m_ref[...]
            m_new = jnp.maximum(m_prev, jnp.max(s, axis=0, keepdims=True))
            alpha = jnp.exp2(m_prev - m_new)
            p = jnp.exp2(s - m_new)
            l_ref[...] = alpha * l_ref[...] + jnp.sum(p, axis=0, keepdims=True)
            m_ref[...] = m_new
            probs.append((p.astype(BF16), alpha))
        for r, (p, alpha) in zip(rows, probs):
            acc_ref[...] = alpha * acc_ref[...] + _dot(vt_ref[:, r], p)

    n_far = jnp.maximum(qi - 1, 0)

    def far_pair(j, carry):
        run([(2 * j, None, False), (2 * j + 1, None, False)])
        return carry

    lax.fori_loop(0, n_far // 2, far_pair, 0)
    prev, diag = (qi - 1, 1, False), (qi, 0, True)

    @pl.when(qi == 0)
    def _():
        run([diag])

    @pl.when((qi >= 1) & (n_far % 2 == 0))
    def _():
        run([prev, diag])

    @pl.when(n_far % 2 == 1)
    def _():
        run([(n_far - 1, None, False), prev, diag])

    halves = (slice(None), slice(0, blk)), (slice(None), slice(blk, 2 * blk))
    return [acc_ref[sl] / l_ref[sl] for sl in halves]


def _diff_attn_kernel(scal_ref, q_ref, k_ref, vt_ref, bias_ref, subln_ref, o_ref,
                      m_ref, l_ref, acc_ref):
    q = q_ref[...]
    lane = lax.broadcasted_iota(jnp.int32, q.shape, 1)
    zero = jnp.zeros_like(q)
    q_both = jnp.concatenate([jnp.where(lane < HEAD_DIM, q, zero),
                              jnp.where(lane < HEAD_DIM, zero, q)], axis=0)
    o1, o2 = _attn_sweep(q_both, k_ref, None, vt_ref, bias_ref, m_ref, l_ref, acc_ref)
    lam = scal_ref[0]
    out_scale = scal_ref[1]
    o = o1 - lam * o2
    ms = jnp.mean(o * o, axis=0, keepdims=True)
    o = o * lax.rsqrt(ms + NORM_EPS) * subln_ref[...] * out_scale
    o_ref[...] = o.T.astype(o_ref.dtype)


def _fox_attn_kernel(q_ref, qaug_ref, k_ref, kaug_ref, vt_ref, o_ref, m_ref, l_ref, acc_ref):
    q = jnp.concatenate([q_ref[...], qaug_ref[...]], axis=1)
    lane = lax.broadcasted_iota(jnp.int32, q.shape, 1)
    zero = jnp.zeros_like(q)
    in_a = (lane < HEAD_DIM) | ((lane >= LANES) & (lane < LANES + 6))
    in_b = ((lane >= HEAD_DIM) & (lane < LANES)) | ((lane >= LANES + 6) & (lane < LANES + 12))
    q_both = jnp.concatenate([jnp.where(in_a, q, zero), jnp.where(in_b, q, zero)], axis=0)
    o1, o2 = _attn_sweep(q_both, k_ref, kaug_ref, vt_ref, None, m_ref, l_ref, acc_ref)
    row = lax.broadcasted_iota(jnp.int32, o1.shape, 0)
    o_ref[...] = jnp.where(row < HEAD_DIM, o1, o2).T.astype(o_ref.dtype)


def _attn_scratch(tq):
    return [pltpu.VMEM((1, 2 * tq), F32), pltpu.VMEM((1, 2 * tq), F32),
            pltpu.VMEM((LANES, 2 * tq), F32)]


def _diff_attn(qk, vt, bias_tiles, scal, subln_col):
    bsz, seq, _ = qk.shape
    tq = min(ATTN_Q, seq)
    return pl.pallas_call(
        _diff_attn_kernel,
        grid=(bsz, N_DIFF_HEADS, seq // tq),
        in_specs=[pl.BlockSpec(memory_space=pltpu.SMEM),
                  pl.BlockSpec((None, tq, LANES), lambda b, s, i: (b, i, s)),
                  pl.BlockSpec((None, seq, LANES), lambda b, s, i: (b, 0, 4 + s)),
                  pl.BlockSpec((None, LANES, seq), lambda b, s, i: (b, s, 0)),
                  pl.BlockSpec((None, 2, tq, 2 * tq), lambda b, s, i: (s, 0, 0, 0)),
                  _const_spec((LANES, 1))],
        out_specs=pl.BlockSpec((None, tq, LANES), lambda b, s, i: (b, i, s)),
        out_shape=jax.ShapeDtypeStruct((bsz, seq, N_DIFF_HEADS * LANES), BF16),
        scratch_shapes=_attn_scratch(tq),
        compiler_params=_params(3),
        name="diff_attn",
    )(scal, qk, qk, vt, bias_tiles, subln_col)


def _fox_attn(qk, aug, vt):
    bsz, seq, _ = qk.shape
    tq = min(ATTN_Q, seq)
    return pl.pallas_call(
        _fox_attn_kernel,
        grid=(bsz, N_FOX_HEADS // 2, seq // tq),
        in_specs=[pl.BlockSpec((None, tq, LANES), lambda b, s, i: (b, i, 8 + s)),
                  pl.BlockSpec((None, tq, LANES), lambda b, s, i: (b, i, s)),
                  pl.BlockSpec((None, seq, LANES), lambda b, s, i: (b, 0, 12 + s)),
                  pl.BlockSpec((None, seq, LANES), lambda b, s, i: (b, 0, 4 + s)),
                  pl.BlockSpec((None, LANES, seq), lambda b, s, i: (b, 4 + s, 0))],
        out_specs=pl.BlockSpec((None, tq, LANES), lambda b, s, i: (b, i, s)),
        out_shape=jax.ShapeDtypeStruct((bsz, seq, N_FOX_HEADS * HEAD_DIM), BF16),
        scratch_shapes=_attn_scratch(tq),
        compiler_params=_params(3),
        name="fox_attn",
    )(qk, aug, qk, aug, vt)


def _outproj_kernel(x_ref, od_ref, of_ref, wd_ref, wf_ref, o_ref):
    o_ref[...] = x_ref[...] + _dot(od_ref[...], wd_ref[...]) + _dot(of_ref[...], wf_ref[...])


def _outproj(x, od, of, w_d, w_f):
    t = x.shape[0]
    tm = min(TOKEN_TILE, t)
    half = od.shape[1]
    return pl.pallas_call(
        _outproj_kernel,
        grid=(t // tm,),
        in_specs=[pl.BlockSpec((tm, D_MODEL), lambda i: (i, 0)),
                  pl.BlockSpec((tm, half), lambda i: (i, 0)),
                  pl.BlockSpec((tm, half), lambda i: (i, 0)),
                  _const_spec((half, D_MODEL)), _const_spec((half, D_MODEL))],
        out_specs=pl.BlockSpec((tm, D_MODEL), lambda i: (i, 0)),
        out_shape=jax.ShapeDtypeStruct((t, D_MODEL), F32),
        compiler_params=_params(1),
        name="attn_outproj",
    )(x, od, of, w_d, w_f)


def _rel_bias_by_distance(rel_bias, n_dist):
    n = jnp.arange(n_dist, dtype=jnp.int32)
    max_exact = N_REL_BUCKETS // 2
    nf = jnp.maximum(n, 1).astype(F32)
    large = max_exact + (jnp.log(nf / max_exact) / math.log(REL_MAX_DIST / max_exact)
                         * (N_REL_BUCKETS - max_exact)).astype(jnp.int32)
    large = jnp.minimum(large, N_REL_BUCKETS - 1)
    return rel_bias[jnp.where(n < max_exact, n, large)]


def _toeplitz(w, n):
    heads, period = w.shape
    flat = jnp.tile(w, (1, n))[:, :n * (period - 1)]
    return flat.reshape(heads, n, period - 1)[:, :, :n]


def _diff_bias_tiles(rel_bias, tq):
    by_dist = _rel_bias_by_distance(rel_bias.astype(F32), 2 * tq)
    by_dist = ((by_dist - by_dist[2 * tq - 1]) * LOG2E).T
    diag = _toeplitz(by_dist, tq)
    prev = _toeplitz(jnp.roll(by_dist, -tq, axis=1), tq)
    tiles = jnp.stack([diag, prev], axis=1)
    return jnp.concatenate([tiles, tiles], axis=-1)


def _attn_layer(x, h, bsz, seq, layer_idx, w_in, w_out, fg_bias, dq_g, dk_g, lq1, lk1, lq2, lk2,
                subln_g, fq_g, fk_g, bias_tiles):
    tq = min(ATTN_Q, seq)
    q_scale = HEAD_DIM ** -0.5 * LOG2E
    w_bf = w_in.astype(BF16)
    w_qk = jnp.concatenate([w_bf[:, 0:1024], w_bf[:, 1536:2560]], axis=1)
    w_v = jnp.concatenate([w_bf[:, 1024:1536], w_bf[:, 2560:3072]], axis=1).T
    w_f = jnp.pad(w_bf[:, 3072:], ((0, 0), (0, LANES - N_FOX_HEADS)))
    fgb = jnp.pad(fg_bias.astype(F32), (0, LANES - N_FOX_HEADS)).reshape(1, LANES)
    gains = jnp.stack([jnp.tile(dq_g, 8) * q_scale, jnp.tile(dk_g, 8),
                       jnp.tile(fq_g, 8) * q_scale, jnp.tile(fk_g, 8)]).astype(F32)
    qk, vt, aug = _inproj(h, w_qk, w_v, w_f, gains, fgb, bsz, seq)
    qk = qk.reshape(bsz, seq, QK_COLS)

    lam_init = 0.8 - 0.6 * math.exp(-0.3 * layer_idx)
    lam = (jnp.exp(jnp.sum(lq1.astype(F32) * lk1.astype(F32)))
           - jnp.exp(jnp.sum(lq2.astype(F32) * lk2.astype(F32))) + lam_init)
    scal = jnp.stack([lam, jnp.asarray(1.0 - lam_init, F32)]).astype(F32)
    od = _diff_attn(qk, vt, bias_tiles, scal, subln_g.astype(F32).reshape(LANES, 1))

    of = _fox_attn(qk, aug.reshape(bsz, seq, AUG_COLS), vt)

    w_o = w_out.astype(BF16)
    half = N_DIFF_HEADS * LANES
    return _outproj(x, od.reshape(bsz * seq, half), of.reshape(bsz * seq, half),
                    w_o[:half], w_o[half:])


def _s5_kernel(n_chunks, *refs):
    x_refs = refs[:S5_CHUNK]
    t_ref, p_ref, q_ref, a_ref, y_ref, z_ref, xp_ref = refs[S5_CHUNK:]
    x = jnp.concatenate([r[...] for r in x_refs], axis=1)
    n_seq = x.shape[0] // n_chunks
    half = z_ref.shape[1] // 2
    z_ref[...] = _dot(x, p_ref[...])
    a_re = a_ref[0:1, :]
    a_im = a_ref[1:2, :]

    def scan(c, carry):
        out = []
        for b in range(n_seq):
            x_re, x_im = carry[2 * b], carry[2 * b + 1]
            row = pl.ds(b * n_chunks + c, 1)
            xp_ref[row, :half] = x_re
            xp_ref[row, half:] = x_im
            z = z_ref[row, :]
            out += [a_re * x_re - a_im * x_im + z[:, :half],
                    a_re * x_im + a_im * x_re + z[:, half:]]
        return tuple(out)

    zero = jnp.zeros((1, half), F32)
    lax.fori_loop(0, n_chunks, scan, (zero,) * (2 * n_seq))
    y = _dot(x, t_ref[...]) + _dot(xp_ref[...].astype(BF16), q_ref[...])
    for t in range(S5_CHUNK):
        y_ref[t] = y[:, t * LANES:(t + 1) * LANES]


def _s5_mats(a_re, a_im, log_step, b_re, b_im, c_re, c_im):
    hp = lax.Precision.HIGHEST
    L = S5_CHUNK
    step = jnp.exp(log_step.astype(F32))[:, None]
    ar, ai = a_re.astype(F32), a_im.astype(F32)
    mag = jnp.exp(ar * step)
    lr, li = mag * jnp.cos(ai * step), mag * jnp.sin(ai * step)
    den = ar * ar + ai * ai
    cr = ((lr - 1.0) * ar + li * ai) / den
    ci = (li * ar - (lr - 1.0) * ai) / den
    br, bi = b_re.astype(F32), b_im.astype(F32)
    bbr = cr[..., None] * br - ci[..., None] * bi
    bbi = cr[..., None] * bi + ci[..., None] * br
    pr, pi = [jnp.ones_like(lr)], [jnp.zeros_like(li)]
    for _ in range(L):
        pr, pi = pr + [pr[-1] * lr - pi[-1] * li], pi + [pr[-1] * li + pi[-1] * lr]
    pr, pi = jnp.stack(pr), jnp.stack(pi)
    wr = pr[:L, :, :, None] * bbr - pi[:L, :, :, None] * bbi
    wi = pr[:L, :, :, None] * bbi + pi[:L, :, :, None] * bbr
    cre, cim = c_re.astype(F32), c_im.astype(F32)
    kern = (jnp.einsum('gcp,ngpd->ngcd', cre, wr, precision=hp)
            - jnp.einsum('gcp,ngpd->ngcd', cim, wi, precision=hp))
    s = jnp.arange(L)
    lag = s[None, :] - s[:, None]
    tk = kern[jnp.maximum(lag, 0)]
    tk = jnp.where((lag >= 0)[:, :, None, None, None], tk, 0.0)
    t_mat = tk.transpose(2, 0, 4, 1, 3).reshape(S5_GROUPS, L * S5_GROUP, L * S5_GROUP)
    p_re = wr[::-1].transpose(1, 0, 3, 2).reshape(S5_GROUPS, L * S5_GROUP, S5_STATE)
    p_im = wi[::-1].transpose(1, 0, 3, 2).reshape(S5_GROUPS, L * S5_GROUP, S5_STATE)
    q_re = (cre[None] * pr[1:, :, None, :] - cim[None] * pi[1:, :, None, :])
    q_im = -(cre[None] * pi[1:, :, None, :] + cim[None] * pr[1:, :, None, :])
    q_re = q_re.transpose(1, 3, 0, 2).reshape(S5_GROUPS, S5_STATE, L * S5_GROUP)
    q_im = q_im.transpose(1, 3, 0, 2).reshape(S5_GROUPS, S5_STATE, L * S5_GROUP)
    return t_mat, p_re, p_im, q_re, q_im, pr[L], pi[L]


def _scatter_groups(m, rows_first):
    g, r, c = m.shape
    kb, gb = g // 8, 8
    L = S5_CHUNK
    m = m.astype(BF16)
    if rows_first[0] == "t":
        m = m.reshape(kb, gb, L, r // L, c).transpose(0, 2, 1, 3, 4).reshape(kb, gb * r, c)
        row_group = (lax.broadcasted_iota(jnp.int32, (gb * r, 1), 0) // S5_GROUP) % gb
    else:
        m = m.reshape(kb, gb * r, c)
        row_group = lax.broadcasted_iota(jnp.int32, (gb * r, 1), 0) // r
    if rows_first[1] == "t":
        src = lax.broadcasted_iota(jnp.int32, (c, gb * c), 0)
        dst = lax.broadcasted_iota(jnp.int32, (c, gb * c), 1)
        rep = ((src // S5_GROUP == dst // (gb * S5_GROUP)) & (src % S5_GROUP == dst % S5_GROUP))
        m = jnp.dot(m, rep.astype(BF16), preferred_element_type=BF16)
        col_group = (lax.broadcasted_iota(jnp.int32, (1, gb * c), 1) // S5_GROUP) % gb
    else:
        m = jnp.tile(m, (1, 1, gb))
        col_group = lax.broadcasted_iota(jnp.int32, (1, gb * c), 1) // c
    return jnp.where(row_group == col_group, m, jnp.zeros_like(m))


def _s5_core(h, bsz, seq, a_re, a_im, log_step, b_re, b_im, c_re, c_im):
    L = S5_CHUNK
    n_chunks = seq // L
    rows = bsz * n_chunks
    kb = D_MODEL // LANES
    rb = min(S5_ROWS, rows)
    assert rb % n_chunks == 0
    t_mat, p_re, p_im, q_re, q_im, a16r, a16i = _s5_mats(a_re, a_im, log_step, b_re, b_im,
                                                          c_re, c_im)
    t_big = _scatter_groups(t_mat, "tt").astype(BF16)
    p_big = jnp.concatenate([_scatter_groups(p_re, "ts"), _scatter_groups(p_im, "ts")],
                            axis=2).astype(BF16)
    q_big = jnp.concatenate([_scatter_groups(q_re, "st"), _scatter_groups(q_im, "st")],
                            axis=1).astype(BF16)
    n_state = 8 * S5_STATE
    a_big = jnp.stack([a16r.reshape(kb, n_state), a16i.reshape(kb, n_state)], axis=1)
    h2 = h.reshape(rows, L * D_MODEL)
    x_specs = [pl.BlockSpec((rb, LANES), lambda k, r, t=t: (r, t * kb + k)) for t in range(L)]
    return pl.pallas_call(
        functools.partial(_s5_kernel, n_chunks),
        grid=(kb, rows // rb),
        in_specs=x_specs + [
            pl.BlockSpec((None, L * LANES, L * LANES), lambda k, r: (k, 0, 0)),
            pl.BlockSpec((None, L * LANES, 2 * n_state), lambda k, r: (k, 0, 0)),
            pl.BlockSpec((None, 2 * n_state, L * LANES), lambda k, r: (k, 0, 0)),
            pl.BlockSpec((None, 2, n_state), lambda k, r: (k, 0, 0))],
        out_specs=pl.BlockSpec((L, rb, LANES), lambda k, r: (0, r, k)),
        out_shape=jax.ShapeDtypeStruct((L, rows, D_MODEL), F32),
        scratch_shapes=[pltpu.VMEM((rb, 2 * n_state), F32), pltpu.VMEM((rb, 2 * n_state), F32)],
        compiler_params=_params(2),
        name="s5_core",
    )(*([h2] * L), t_big, p_big, q_big, a_big)


def _glu_kernel(x_ref, y_ref, g_ref, d_ref, wa_ref, wb_ref, o_ref):
    x = x_ref[...]
    y = y_ref[...] + d_ref[...] * _rms(x, g_ref[...])
    y = jax.nn.gelu(y).astype(BF16)
    o_ref[...] = x + _dot(y, wa_ref[...]) * jax.nn.sigmoid(_dot(y, wb_ref[...]))


def _s5_layer(x, h, bsz, seq, mix_g, a_re, a_im, log_step, b_re, b_im, c_re, c_im, d_skip,
              w_a, w_b):
    L = S5_CHUNK
    rows = x.shape[0] // L
    rg = min(TOKEN_TILE, rows)
    y = _s5_core(h, bsz, seq, a_re, a_im, log_step, b_re, b_im, c_re, c_im)
    tile = pl.BlockSpec((rg, D_MODEL), lambda r, t: (r, t))
    out = pl.pallas_call(
        _glu_kernel,
        grid=(rows // rg, L),
        in_specs=[tile, pl.BlockSpec((None, rg, D_MODEL), lambda r, t: (t, r, 0)),
                  _const_spec((1, D_MODEL)), _const_spec((1, D_MODEL)),
                  _const_spec((D_MODEL, D_MODEL)), _const_spec((D_MODEL, D_MODEL))],
        out_specs=tile,
        out_shape=jax.ShapeDtypeStruct((rows, L * D_MODEL), F32),
        compiler_params=_params(2),
        name="s5_glu",
    )(x.reshape(rows, L * D_MODEL), y, mix_g.astype(F32).reshape(1, D_MODEL),
      d_skip.astype(F32).reshape(1, D_MODEL), w_a.astype(BF16), w_b.astype(BF16))
    return out.reshape(rows * L, D_MODEL)


def kernel(x, ffn1_norm, ffn1_gate, ffn1_up, ffn1_down, mix_norm, ffn2_norm, ffn2_gate, ffn2_up, ffn2_down, attn_w_in, attn_w_out, fg_bias, diff_q_norm, diff_k_norm, diff_lambda_q1, diff_lambda_k1, diff_lambda_q2, diff_lambda_k2, diff_subln, fox_q_norm, fox_k_norm, rel_bias, s5_a_re, s5_a_im, s5_log_step, s5_b_re, s5_b_im, s5_c_re, s5_c_im, s5_d, s5_glu_a, s5_glu_b):
    bsz, seq, _ = x.shape
    depth = ffn1_norm.shape[0]
    assert min(ATTN_Q, seq) >= REL_MAX_DIST, "bias tiles assume the bias saturates within a block"
    bias_tiles = _diff_bias_tiles(rel_bias, min(ATTN_Q, seq))
    xt = x.reshape(bsz * seq, D_MODEL).astype(F32)
    for i in range(depth):
        xt, h = _ffn(xt, ffn1_norm[i], ffn1_gate[i].astype(BF16), ffn1_up[i].astype(BF16),
                     ffn1_down[i].astype(BF16), g_next=mix_norm[i])
        j = i // 2
        if i % 2 == 0:
            xt = _attn_layer(xt, h, bsz, seq, i, attn_w_in[j], attn_w_out[j], fg_bias[j],
                             diff_q_norm[j], diff_k_norm[j], diff_lambda_q1[j], diff_lambda_k1[j],
                             diff_lambda_q2[j], diff_lambda_k2[j], diff_subln[j], fox_q_norm[j],
                             fox_k_norm[j], bias_tiles)
        else:
            xt = _s5_layer(xt, h, bsz, seq, mix_norm[i], s5_a_re[j], s5_a_im[j], s5_log_step[j],
                           s5_b_re[j], s5_b_im[j], s5_c_re[j], s5_c_im[j], s5_d[j],
                           s5_glu_a[j], s5_glu_b[j])
        xt, _ = _ffn(xt, ffn2_norm[i], ffn2_gate[i].astype(BF16), ffn2_up[i].astype(BF16),
                     ffn2_down[i].astype(BF16))
    return xt.reshape(bsz, seq, D_MODEL).astype(x.dtype)
```

```python
import functools
import math

import jax
import jax.numpy as jnp
import numpy as np
from jax import lax
from jax.experimental import pallas as pl
from jax.experimental.pallas import tpu as pltpu

D_MODEL = 1024
D_FF = 2816
HEAD_DIM = 64
N_DIFF_HEADS = 4
N_FOX_HEADS = 8
N_REL_BUCKETS = 32
REL_MAX_DIST = 128
S5_GROUP = 16
S5_GROUPS = D_MODEL // S5_GROUP
S5_STATE = 64
NORM_EPS = 1e-6

QK_COLS = 2048
V_COLS = 1024
AUG_COLS = 1024
LANES = 128
MXU_DIM = 256
VMEM_LIMIT = 56 * 1024 * 1024

TOKEN_TILE = 512
FFN_TILE = 1024
FFN_SPLITS = (0, 6 * MXU_DIM, D_FF)
ATTN_Q = 512
ATTN_LANES = 256
S5_CHUNK = 16
S5_ROWS = 512
NEG_BIG = -1e30
LOG2E = 1.4426950408889634

BF16 = jnp.bfloat16
F32 = jnp.float32


def _dot(a, b):
    return jnp.dot(a, b, preferred_element_type=F32)


def _dot_nt(a, b):
    return lax.dot_general(a, b, (((1,), (1,)), ((), ())), preferred_element_type=F32)


def _rms(x, g):
    ms = jnp.mean(x * x, axis=-1, keepdims=True)
    return x * lax.rsqrt(ms + NORM_EPS) * g


def _const_spec(shape):
    return pl.BlockSpec(shape, lambda *_: (0,) * len(shape), pipeline_mode=pl.Buffered(1))


def _params(n_axes):
    return pltpu.CompilerParams(dimension_semantics=("arbitrary",) * n_axes,
                                vmem_limit_bytes=VMEM_LIMIT)


def _split3(x):
    p0 = x.astype(BF16)
    r1 = x - p0.astype(F32)
    p1 = r1.astype(BF16)
    p2 = (r1 - p1.astype(F32)).astype(BF16)
    return p0, p1, p2


def _ffn_kernel(x_ref, g_ref, gn_ref, wg_ref, wu_ref, wd_ref, o_ref, *h_out):
    x = x_ref[...]
    h = _rms(x, g_ref[...]).astype(BF16)
    acc = None
    for lo, hi in zip(FFN_SPLITS[:-1], FFN_SPLITS[1:]):
        sl = slice(lo, hi)
        a = _dot(h, wg_ref[:, sl])
        b = _dot(h, wu_ref[:, sl])
        act = (a * jax.nn.sigmoid(a) * b).astype(BF16)
        d = _dot(act, wd_ref[sl, :])
        acc = d if acc is None else acc + d
    xo = x + 0.5 * acc
    o_ref[...] = xo
    if h_out:
        h_out[0][...] = _rms(xo, gn_ref[...]).astype(BF16)


def _ffn(x, g, wg, wu, wd, g_next=None):
    t = x.shape[0]
    tm = min(FFN_TILE, t)
    emit = g_next is not None
    gn = g_next if emit else g
    row = pl.BlockSpec((tm, D_MODEL), lambda i: (i, 0))
    out_shape = [jax.ShapeDtypeStruct((t, D_MODEL), F32)]
    out_specs = [row]
    if emit:
        out_shape.append(jax.ShapeDtypeStruct((t, D_MODEL), BF16))
        out_specs.append(row)
    res = pl.pallas_call(
        _ffn_kernel,
        grid=(t // tm,),
        in_specs=[row, _const_spec((1, D_MODEL)), _const_spec((1, D_MODEL)),
                  _const_spec((D_MODEL, D_FF)), _const_spec((D_MODEL, D_FF)),
                  _const_spec((D_FF, D_MODEL))],
        out_specs=out_specs,
        out_shape=out_shape,
        compiler_params=_params(1),
        name="ffn",
    )(x, g.reshape(1, D_MODEL), gn.reshape(1, D_MODEL), wg, wu, wd)
    return (res[0], res[1]) if emit else (res[0], None)


def _inproj_kernel(tiles_per_seq, h_ref, wqk_ref, wv_ref, wf_ref, gains_ref, ones_ref, tri_ref,
                   fgb_ref, sel_ref, aug1_ref, qk_ref, vt_ref, aug_ref, carry_ref):
    i = pl.program_id(0)
    h = h_ref[...]
    ones_blk = ones_ref[...]
    for c in range(QK_COLS // 512):
        y = _dot(h, wqk_ref[:, c * 512:(c + 1) * 512])
        y2 = y * y
        hi = y2.astype(BF16)
        lo = (y2 - hi.astype(F32)).astype(BF16)
        ssq = jnp.concatenate(
            [_dot(hi[:, k * MXU_DIM:(k + 1) * MXU_DIM], ones_blk)
             + _dot(lo[:, k * MXU_DIM:(k + 1) * MXU_DIM], ones_blk)
             for k in range(512 // MXU_DIM)], axis=1)
        y = y * lax.rsqrt(ssq * (1.0 / HEAD_DIM) + NORM_EPS) * gains_ref[c:c + 1, :]
        qk_ref[:, c * 512:(c + 1) * 512] = y.astype(BF16)
    for c in range(V_COLS // 512):
        vt = _dot_nt(wv_ref[c * 512:(c + 1) * 512, :], h)
        vt_ref[c * 512:(c + 1) * 512, :] = vt.astype(BF16)

    fl = _dot(h, wf_ref[...]) + fgb_ref[...]
    logf = -(jnp.maximum(-fl, 0.0) + jnp.log1p(jnp.exp(-jnp.abs(fl))))

    @pl.when(i % tiles_per_seq == 0)
    def _():
        carry_ref[...] = jnp.zeros_like(carry_ref)

    tri = tri_ref[...]
    cum = sum(_dot(tri, p) for p in _split3(logf)) + carry_ref[...]
    carry_ref[...] = cum[cum.shape[0] - 1:, :]
    aug = sum(_dot(p, sel_ref[n]) for n, p in enumerate(_split3(cum * LOG2E))) + aug1_ref[...]
    aug_ref[...] = aug.astype(BF16)


def _decay_lane_maps():
    sel = np.zeros((3, LANES, AUG_COLS), np.float32)
    ones = np.zeros((1, AUG_COLS), np.float32)
    k_off = AUG_COLS // 2
    for head in range(N_FOX_HEADS):
        base = (head // 2) * LANES + (head % 2) * 6
        for n in range(3):
            sel[n, head, base + n] = 1.0
            sel[n, head, k_off + base + 3 + n] = -1.0
            ones[0, base + 3 + n] = 1.0
            ones[0, k_off + base + n] = 1.0
    return jnp.asarray(sel, BF16), jnp.asarray(ones, F32)


def _inproj(h, w_qk, w_v, w_f, gains, fg_bias_row, bsz, seq):
    t = h.shape[0]
    tm = min(TOKEN_TILE, seq)
    tps = seq // tm
    ones_blk = jnp.kron(jnp.eye(MXU_DIM // HEAD_DIM, dtype=F32),
                        jnp.ones((HEAD_DIM, HEAD_DIM), F32)).astype(BF16)
    tri = jnp.tril(jnp.ones((tm, tm), F32)).astype(BF16)
    sel, aug_ones = _decay_lane_maps()
    return pl.pallas_call(
        functools.partial(_inproj_kernel, tps),
        grid=(t // tm,),
        in_specs=[pl.BlockSpec((tm, D_MODEL), lambda i: (i, 0)),
                  _const_spec((D_MODEL, QK_COLS)), _const_spec((D_MODEL, V_COLS)),
                  _const_spec((D_MODEL, LANES)),
                  _const_spec((QK_COLS // 512, 512)), _const_spec((MXU_DIM, MXU_DIM)),
                  _const_spec((tm, tm)), _const_spec((1, LANES)),
                  _const_spec((3, LANES, AUG_COLS)), _const_spec((1, AUG_COLS))],
        out_specs=[pl.BlockSpec((tm, QK_COLS), lambda i: (i, 0)),
                   pl.BlockSpec((None, V_COLS, tm), lambda i: (i // tps, 0, i % tps)),
                   pl.BlockSpec((tm, AUG_COLS), lambda i: (i, 0))],
        out_shape=[jax.ShapeDtypeStruct((t, QK_COLS), BF16),
                   jax.ShapeDtypeStruct((bsz, V_COLS, seq), BF16),
                   jax.ShapeDtypeStruct((t, AUG_COLS), BF16)],
        scratch_shapes=[pltpu.VMEM((1, LANES), F32)],
        compiler_params=_params(1),
        name="attn_inproj",
    )(h, w_qk, w_v, w_f, gains, ones_blk, tri, fg_bias_row, sel, aug_ones)


def _attn_sweep(q_both, k_ref, kaug_ref, vt_ref, bias_ref, m_ref, l_ref, acc_ref):
    qi = pl.program_id(2)
    blk = q_both.shape[0] // 2
    m_ref[...] = jnp.full_like(m_ref, NEG_BIG)
    l_ref[...] = jnp.zeros_like(l_ref)
    acc_ref[...] = jnp.zeros_like(acc_ref)

    def run(items):
        rows = [pl.ds(pl.multiple_of(j * blk, blk), blk) for j, _, _ in items]
        lanes = [slice(c * ATTN_LANES, (c + 1) * ATTN_LANES)
                 for c in range(2 * blk // ATTN_LANES)]
        scores = {}
        for n, r in enumerate(rows):
            k_blk = k_ref[r, :]
            if kaug_ref is not None:
                k_blk = jnp.concatenate([k_blk, kaug_ref[r, :]], axis=1)
            for c, ls in enumerate(lanes):
                scores[n, c] = _dot_nt(k_blk, q_both[ls, :])
        for n, (_, bias_idx, masked) in enumerate(items):
            for c, ls in enumerate(lanes):
                s = scores.pop((n, c))
                if bias_idx is not None and bias_ref is not None:
                    s = s + bias_ref[bias_idx, :, ls]
                if masked:
                    key = lax.broadcasted_iota(jnp.int32, s.shape, 0)
                    qry = (lax.broadcasted_iota(jnp.int32, s.shape, 1) + ls.start) & (blk - 1)
                    s = jnp.where(key <= qry, s, NEG_BIG)
                m_prev = m_ref[:, ls]
                m_new = jnp.maximum(m_prev, jnp.max(s, axis=0, keepdims=True))
                alpha = jnp.exp2(m_prev - m_new)
                p = jnp.exp2(s - m_new)
                l_ref[:, ls] = alpha * l_ref[:, ls] + jnp.sum(p, axis=0, keepdims=True)
                m_ref[:, ls] = m_new
                acc_ref[:, ls] = alpha * acc_ref[:, ls] + _dot(vt_ref[:, rows[n]], p.astype(BF16))

    n_far = jnp.maximum(qi - 1, 0)

    def far_pair(j, carry):
        run([(2 * j, None, False), (2 * j + 1, None, False)])
        return carry

    lax.fori_loop(0, n_far // 2, far_pair, 0)
    prev, diag = (qi - 1, 1, False), (qi, 0, True)

    @pl.when(qi == 0)
    def _():
        run([diag])

    @pl.when((qi >= 1) & (n_far % 2 == 0))
    def _():
        run([prev, diag])

    @pl.when(n_far % 2 == 1)
    def _():
        run([(n_far - 1, None, False), prev, diag])

    halves = (slice(None), slice(0, blk)), (slice(None), slice(blk, 2 * blk))
    return [acc_ref[sl] / l_ref[sl] for sl in halves]


def _diff_attn_kernel(scal_ref, q_ref, k_ref, vt_ref, bias_ref, subln_ref, o_ref,
                      m_ref, l_ref, acc_ref):
    q = q_ref[...]
    lane = lax.broadcasted_iota(jnp.int32, q.shape, 1)
    zero = jnp.zeros_like(q)
    q_both = jnp.concatenate([jnp.where(lane < HEAD_DIM, q, zero),
                              jnp.where(lane < HEAD_DIM, zero, q)], axis=0)
    o1, o2 = _attn_sweep(q_both, k_ref, None, vt_ref, bias_ref, m_ref, l_ref, acc_ref)
    lam = scal_ref[0]
    out_scale = scal_ref[1]
    o = o1 - lam * o2
    ms = jnp.mean(o * o, axis=0, keepdims=True)
    o = o * lax.rsqrt(ms + NORM_EPS) * subln_ref[...] * out_scale
    o_ref[...] = o.T.astype(o_ref.dtype)


def _fox_attn_kernel(q_ref, qaug_ref, k_ref, kaug_ref, vt_ref, o_ref, m_ref, l_ref, acc_ref):
    q = jnp.concatenate([q_ref[...], qaug_ref[...]], axis=1)
    lane = lax.broadcasted_iota(jnp.int32, q.shape, 1)
    zero = jnp.zeros_like(q)
    in_a = (lane < HEAD_DIM) | ((lane >= LANES) & (lane < LANES + 6))
    in_b = ((lane >= HEAD_DIM) & (lane < LANES)) | ((lane >= LANES + 6) & (lane < LANES + 12))
    q_both = jnp.concatenate([jnp.where(in_a, q, zero), jnp.where(in_b, q, zero)], axis=0)
    o1, o2 = _attn_sweep(q_both, k_ref, kaug_ref, vt_ref, None, m_ref, l_ref, acc_ref)
    row = lax.broadcasted_iota(jnp.int32, o1.shape, 0)
    o_ref[...] = jnp.where(row < HEAD_DIM, o1, o2).T.astype(o_ref.dtype)


def _attn_scratch(tq):
    return [pltpu.VMEM((1, 2 * tq), F32), pltpu.VMEM((1, 2 * tq), F32),
            pltpu.VMEM((LANES, 2 * tq), F32)]


def _diff_attn(qk, vt, bias_tiles, scal, subln_col):
    bsz, seq, _ = qk.shape
    tq = min(ATTN_Q, seq)
    return pl.pallas_call(
        _diff_attn_kernel,
        grid=(bsz, N_DIFF_HEADS, seq // tq),
        in_specs=[pl.BlockSpec(memory_space=pltpu.SMEM),
                  pl.BlockSpec((None, tq, LANES), lambda b, s, i: (b, i, s)),
                  pl.BlockSpec((None, seq, LANES), lambda b, s, i: (b, 0, 4 + s)),
                  pl.BlockSpec((None, LANES, seq), lambda b, s, i: (b, s, 0)),
                  pl.BlockSpec((None, 2, tq, 2 * tq), lambda b, s, i: (s, 0, 0, 0)),
                  _const_spec((LANES, 1))],
        out_specs=pl.BlockSpec((None, tq, LANES), lambda b, s, i: (b, i, s)),
        out_shape=jax.ShapeDtypeStruct((bsz, seq, N_DIFF_HEADS * LANES), BF16),
        scratch_shapes=_attn_scratch(tq),
        compiler_params=_params(3),
        name="diff_attn",
    )(scal, qk, qk, vt, bias_tiles, subln_col)


def _fox_attn(qk, aug, vt):
    bsz, seq, _ = qk.shape
    tq = min(ATTN_Q, seq)
    return pl.pallas_call(
        _fox_attn_kernel,
        grid=(bsz, N_FOX_HEADS // 2, seq // tq),
        in_specs=[pl.BlockSpec((None, tq, LANES), lambda b, s, i: (b, i, 8 + s)),
                  pl.BlockSpec((None, tq, LANES), lambda b, s, i: (b, i, s)),
                  pl.BlockSpec((None, seq, LANES), lambda b, s, i: (b, 0, 12 + s)),
                  pl.BlockSpec((None, seq, LANES), lambda b, s, i: (b, 0, 4 + s)),
                  pl.BlockSpec((None, LANES, seq), lambda b, s, i: (b, 4 + s, 0))],
        out_specs=pl.BlockSpec((None, tq, LANES), lambda b, s, i: (b, i, s)),
        out_shape=jax.ShapeDtypeStruct((bsz, seq, N_FOX_HEADS * HEAD_DIM), BF16),
        scratch_shapes=_attn_scratch(tq),
        compiler_params=_params(3),
        name="fox_attn",
    )(qk, aug, qk, aug, vt)


def _outproj_kernel(x_ref, od_ref, of_ref, wd_ref, wf_ref, o_ref):
    o_ref[...] = x_ref[...] + _dot(od_ref[...], wd_ref[...]) + _dot(of_ref[...], wf_ref[...])


def _outproj(x, od, of, w_d, w_f):
    t = x.shape[0]
    tm = min(TOKEN_TILE, t)
    half = od.shape[1]
    return pl.pallas_call(
        _outproj_kernel,
        grid=(t // tm,),
        in_specs=[pl.BlockSpec((tm, D_MODEL), lambda i: (i, 0)),
                  pl.BlockSpec((tm, half), lambda i: (i, 0)),
                  pl.BlockSpec((tm, half), lambda i: (i, 0)),
                  _const_spec((half, D_MODEL)), _const_spec((half, D_MODEL))],
        out_specs=pl.BlockSpec((tm, D_MODEL), lambda i: (i, 0)),
        out_shape=jax.ShapeDtypeStruct((t, D_MODEL), F32),
        compiler_params=_params(1),
        name="attn_outproj",
    )(x, od, of, w_d, w_f)


def _rel_bias_by_distance(rel_bias, n_dist):
    n = jnp.arange(n_dist, dtype=jnp.int32)
    max_exact = N_REL_BUCKETS // 2
    nf = jnp.maximum(n, 1).astype(F32)
    large = max_exact + (jnp.log(nf / max_exact) / math.log(REL_MAX_DIST / max_exact)
                         * (N_REL_BUCKETS - max_exact)).astype(jnp.int32)
    large = jnp.minimum(large, N_REL_BUCKETS - 1)
    return rel_bias[jnp.where(n < max_exact, n, large)]


def _toeplitz(w, n):
    heads, period = w.shape
    flat = jnp.tile(w, (1, n))[:, :n * (period - 1)]
    return flat.reshape(heads, n, period - 1)[:, :, :n]


def _diff_bias_tiles(rel_bias, tq):
    by_dist = _rel_bias_by_distance(rel_bias.astype(F32), 2 * tq)
    by_dist = ((by_dist - by_dist[2 * tq - 1]) * LOG2E).T
    diag = _toeplitz(by_dist, tq)
    prev = _toeplitz(jnp.roll(by_dist, -tq, axis=1), tq)
    tiles = jnp.stack([diag, prev], axis=1)
    return jnp.concatenate([tiles, tiles], axis=-1)


def _attn_layer(x, h, bsz, seq, layer_idx, w_in, w_out, fg_bias, dq_g, dk_g, lq1, lk1, lq2, lk2,
                subln_g, fq_g, fk_g, bias_tiles):
    tq = min(ATTN_Q, seq)
    q_scale = HEAD_DIM ** -0.5 * LOG2E
    w_bf = w_in.astype(BF16)
    w_qk = jnp.concatenate([w_bf[:, 0:1024], w_bf[:, 1536:2560]], axis=1)
    w_v = jnp.concatenate([w_bf[:, 1024:1536], w_bf[:, 2560:3072]], axis=1).T
    w_f = jnp.pad(w_bf[:, 3072:], ((0, 0), (0, LANES - N_FOX_HEADS)))
    fgb = jnp.pad(fg_bias.astype(F32), (0, LANES - N_FOX_HEADS)).reshape(1, LANES)
    gains = jnp.stack([jnp.tile(dq_g, 8) * q_scale, jnp.tile(dk_g, 8),
                       jnp.tile(fq_g, 8) * q_scale, jnp.tile(fk_g, 8)]).astype(F32)
    qk, vt, aug = _inproj(h, w_qk, w_v, w_f, gains, fgb, bsz, seq)
    qk = qk.reshape(bsz, seq, QK_COLS)

    lam_init = 0.8 - 0.6 * math.exp(-0.3 * layer_idx)
    lam = (jnp.exp(jnp.sum(lq1.astype(F32) * lk1.astype(F32)))
           - jnp.exp(jnp.sum(lq2.astype(F32) * lk2.astype(F32))) + lam_init)
    scal = jnp.stack([lam, jnp.asarray(1.0 - lam_init, F32)]).astype(F32)
    od = _diff_attn(qk, vt, bias_tiles, scal, subln_g.astype(F32).reshape(LANES, 1))

    of = _fox_attn(qk, aug.reshape(bsz, seq, AUG_COLS), vt)

    w_o = w_out.astype(BF16)
    half = N_DIFF_HEADS * LANES
    return _outproj(x, od.reshape(bsz * seq, half), of.reshape(bsz * seq, half),
                    w_o[:half], w_o[half:])


def _s5_kernel(n_chunks, *refs):
    x_refs = refs[:S5_CHUNK]
    t_ref, p_ref, q_ref, a_ref, y_ref, z_ref, xp_ref = refs[S5_CHUNK:]
    x = jnp.concatenate([r[...] for r in x_refs], axis=1)
    n_seq = x.shape[0] // n_chunks
    half = z_ref.shape[1] // 2
    z_ref[...] = _dot(x, p_ref[...])
    a_re = a_ref[0:1, :]
    a_im = a_ref[1:2, :]

    def scan(c, carry):
        out = []
        for b in range(n_seq):
            x_re, x_im = carry[2 * b], carry[2 * b + 1]
            row = pl.ds(b * n_chunks + c, 1)
            xp_ref[row, :half] = x_re
            xp_ref[row, half:] = x_im
            z = z_ref[row, :]
            out += [a_re * x_re - a_im * x_im + z[:, :half],
                    a_re * x_im + a_im * x_re + z[:, half:]]
        return tuple(out)

    zero = jnp.zeros((1, half), F32)
    lax.fori_loop(0, n_chunks, scan, (zero,) * (2 * n_seq))
    y = _dot(x, t_ref[...]) + _dot(xp_ref[...].astype(BF16), q_ref[...])
    for t in range(S5_CHUNK):
        y_ref[t] = y[:, t * LANES:(t + 1) * LANES]


def _s5_mats(a_re, a_im, log_step, b_re, b_im, c_re, c_im):
    hp = lax.Precision.HIGHEST
    L = S5_CHUNK
    step = jnp.exp(log_step.astype(F32))[:, None]
    ar, ai = a_re.astype(F32), a_im.astype(F32)
    mag = jnp.exp(ar * step)
    lr, li = mag * jnp.cos(ai * step), mag * jnp.sin(ai * step)
    den = ar * ar + ai * ai
    cr = ((lr - 1.0) * ar + li * ai) / den
    ci = (li * ar - (lr - 1.0) * ai) / den
    br, bi = b_re.astype(F32), b_im.astype(F32)
    bbr = cr[..., None] * br - ci[..., None] * bi
    bbi = cr[..., None] * bi + ci[..., None] * br
    pr, pi = [jnp.ones_like(lr)], [jnp.zeros_like(li)]
    for _ in range(L):
        pr, pi = pr + [pr[-1] * lr - pi[-1] * li], pi + [pr[-1] * li + pi[-1] * lr]
    pr, pi = jnp.stack(pr), jnp.stack(pi)
    wr = pr[:L, :, :, None] * bbr - pi[:L, :, :, None] * bbi
    wi = pr[:L, :, :, None] * bbi + pi[:L, :, :, None] * bbr
    cre, cim = c_re.astype(F32), c_im.astype(F32)
    kern = (jnp.einsum('gcp,ngpd->ngcd', cre, wr, precision=hp)
            - jnp.einsum('gcp,ngpd->ngcd', cim, wi, precision=hp))
    kb, gb = S5_GROUPS // 8, 8
    s = jnp.arange(L)
    lag = s[None, :] - s[:, None]
    tk = kern.astype(BF16)[jnp.maximum(lag, 0)]
    tk = jnp.where((lag >= 0)[:, :, None, None, None], tk, jnp.zeros_like(tk))
    t_c = tk.reshape(L, L, kb, gb, S5_GROUP, S5_GROUP).transpose(2, 0, 3, 5, 1, 4)
    t_c = t_c.reshape(kb, gb * L * S5_GROUP, L * S5_GROUP)

    def rows_tok(w):
        w = w.astype(BF16).reshape(L, kb, gb, S5_STATE, S5_GROUP).transpose(1, 0, 2, 4, 3)
        return w.reshape(kb, gb * L * S5_GROUP, S5_STATE)

    p_re, p_im = rows_tok(wr[::-1]), rows_tok(wi[::-1])
    q_re = (cre[None] * pr[1:, :, None, :] - cim[None] * pi[1:, :, None, :])
    q_im = -(cre[None] * pi[1:, :, None, :] + cim[None] * pr[1:, :, None, :])
    q_re = q_re.astype(BF16).transpose(1, 3, 0, 2).reshape(kb, gb * S5_STATE, L * S5_GROUP)
    q_im = q_im.astype(BF16).transpose(1, 3, 0, 2).reshape(kb, gb * S5_STATE, L * S5_GROUP)
    return t_c, p_re, p_im, q_re, q_im, pr[L], pi[L]


def _scatter_groups(m, token_rows, token_cols):
    kb, r, c = m.shape
    gb = 8
    row = lax.broadcasted_iota(jnp.int32, (r, 1), 0)
    row_group = (row // S5_GROUP) % gb if token_rows else row // S5_STATE
    col = lax.broadcasted_iota(jnp.int32, (1, gb * c), 1)
    if token_cols:
        src = lax.broadcasted_iota(jnp.int32, (c, gb * c), 0)
        dst = lax.broadcasted_iota(jnp.int32, (c, gb * c), 1)
        rep = ((src // S5_GROUP == dst // (gb * S5_GROUP)) & (src % S5_GROUP == dst % S5_GROUP))
        m = jnp.dot(m, rep.astype(BF16), preferred_element_type=BF16)
        col_group = (col // S5_GROUP) % gb
    else:
        m = jnp.tile(m, (1, 1, gb))
        col_group = col // c
    return jnp.where(row_group == col_group, m, jnp.zeros_like(m))


def _s5_core(h, bsz, seq, a_re, a_im, log_step, b_re, b_im, c_re, c_im):
    L = S5_CHUNK
    n_chunks = seq // L
    rows = bsz * n_chunks
    kb = D_MODEL // LANES
    rb = min(S5_ROWS, rows)
    assert rb % n_chunks == 0
    t_c, p_re, p_im, q_re, q_im, a16r, a16i = _s5_mats(a_re, a_im, log_step, b_re, b_im,
                                                        c_re, c_im)
    t_big = _scatter_groups(t_c, True, True)
    p_big = jnp.concatenate([_scatter_groups(p_re, True, False),
                             _scatter_groups(p_im, True, False)], axis=2)
    q_big = jnp.concatenate([_scatter_groups(q_re, False, True),
                             _scatter_groups(q_im, False, True)], axis=1)
    n_state = 8 * S5_STATE
    a_big = jnp.stack([a16r.reshape(kb, n_state), a16i.reshape(kb, n_state)], axis=1)
    h2 = h.reshape(rows, L * D_MODEL)
    x_specs = [pl.BlockSpec((rb, LANES), lambda k, r, t=t: (r, t * kb + k)) for t in range(L)]
    return pl.pallas_call(
        functools.partial(_s5_kernel, n_chunks),
        grid=(kb, rows // rb),
        in_specs=x_specs + [
            pl.BlockSpec((None, L * LANES, L * LANES), lambda k, r: (k, 0, 0)),
            pl.BlockSpec((None, L * LANES, 2 * n_state), lambda k, r: (k, 0, 0)),
            pl.BlockSpec((None, 2 * n_state, L * LANES), lambda k, r: (k, 0, 0)),
            pl.BlockSpec((None, 2, n_state), lambda k, r: (k, 0, 0))],
        out_specs=pl.BlockSpec((L, rb, LANES), lambda k, r: (0, r, k)),
        out_shape=jax.ShapeDtypeStruct((L, rows, D_MODEL), F32),
        scratch_shapes=[pltpu.VMEM((rb, 2 * n_state), F32), pltpu.VMEM((rb, 2 * n_state), F32)],
        compiler_params=_params(2),
        name="s5_core",
    )(*([h2] * L), t_big, p_big, q_big, a_big)


def _glu_kernel(x_ref, y_ref, g_ref, d_ref, wa_ref, wb_ref, o_ref):
    x = x_ref[...]
    y = y_ref[...] + d_ref[...] * _rms(x, g_ref[...])
    y = jax.nn.gelu(y).astype(BF16)
    o_ref[...] = x + _dot(y, wa_ref[...]) * jax.nn.sigmoid(_dot(y, wb_ref[...]))


def _s5_layer(x, h, bsz, seq, mix_g, a_re, a_im, log_step, b_re, b_im, c_re, c_im, d_skip,
              w_a, w_b):
    L = S5_CHUNK
    rows = x.shape[0] // L
    rg = min(TOKEN_TILE, rows)
    y = _s5_core(h, bsz, seq, a_re, a_im, log_step, b_re, b_im, c_re, c_im)
    tile = pl.BlockSpec((rg, D_MODEL), lambda r, t: (r, t))
    out = pl.pallas_call(
        _glu_kernel,
        grid=(rows // rg, L),
        in_specs=[tile, pl.BlockSpec((None, rg, D_MODEL), lambda r, t: (t, r, 0)),
                  _const_spec((1, D_MODEL)), _const_spec((1, D_MODEL)),
                  _const_spec((D_MODEL, D_MODEL)), _const_spec((D_MODEL, D_MODEL))],
        out_specs=tile,
        out_shape=jax.ShapeDtypeStruct((rows, L * D_MODEL), F32),
        compiler_params=_params(2),
        name="s5_glu",
    )(x.reshape(rows, L * D_MODEL), y, mix_g.astype(F32).reshape(1, D_MODEL),
      d_skip.astype(F32).reshape(1, D_MODEL), w_a.astype(BF16), w_b.astype(BF16))
    return out.reshape(rows * L, D_MODEL)


def kernel(x, ffn1_norm, ffn1_gate, ffn1_up, ffn1_down, mix_norm, ffn2_norm, ffn2_gate, ffn2_up, ffn2_down, attn_w_in, attn_w_out, fg_bias, diff_q_norm, diff_k_norm, diff_lambda_q1, diff_lambda_k1, diff_lambda_q2, diff_lambda_k2, diff_subln, fox_q_norm, fox_k_norm, rel_bias, s5_a_re, s5_a_im, s5_log_step, s5_b_re, s5_b_im, s5_c_re, s5_c_im, s5_d, s5_glu_a, s5_glu_b):
    bsz, seq, _ = x.shape
    depth = ffn1_norm.shape[0]
    assert min(ATTN_Q, seq) >= REL_MAX_DIST, "bias tiles assume the bias saturates within a block"
    bias_tiles = _diff_bias_tiles(rel_bias, min(ATTN_Q, seq))
    xt = x.reshape(bsz * seq, D_MODEL).astype(F32)
    for i in range(depth):
        xt, h = _ffn(xt, ffn1_norm[i], ffn1_gate[i].astype(BF16), ffn1_up[i].astype(BF16),
                     ffn1_down[i].astype(BF16), g_next=mix_norm[i])
        j = i // 2
        if i % 2 == 0:
            xt = _attn_layer(xt, h, bsz, seq, i, attn_w_in[j], attn_w_out[j], fg_bias[j],
                             diff_q_norm[j], diff_k_norm[j], diff_lambda_q1[j], diff_lambda_k1[j],
                             diff_lambda_q2[j], diff_lambda_k2[j], diff_subln[j], fox_q_norm[j],
                             fox_k_norm[j], bias_tiles)
        else:
            xt = _s5_layer(xt, h, bsz, seq, mix_norm[i], s5_a_re[j], s5_a_im[j], s5_log_step[j],
                           s5_b_re[j], s5_b_im[j], s5_c_re[j], s5_c_im[j], s5_d[j],
                           s5_glu_a[j], s5_glu_b[j])
        xt, _ = _ffn(xt, ffn2_norm[i], ffn2_gate[i].astype(BF16), ffn2_up[i].astype(BF16),
                     ffn2_down[i].astype(BF16))
    return xt.reshape(bsz, seq, D_MODEL).astype(x.dtype)
```

```python
import functools
import math

import jax
import jax.numpy as jnp
import numpy as np
from jax import lax
from jax.experimental import pallas as pl
from jax.experimental.pallas import tpu as pltpu

D_MODEL = 1024
D_FF = 2816
HEAD_DIM = 64
N_DIFF_HEADS = 4
N_FOX_HEADS = 8
N_REL_BUCKETS = 32
REL_MAX_DIST = 128
S5_GROUP = 16
S5_GROUPS = D_MODEL // S5_GROUP
S5_STATE = 64
NORM_EPS = 1e-6

QK_COLS = 2048
V_COLS = 1024
AUG_COLS = 1024
LANES = 128
MXU_DIM = 256
VMEM_LIMIT = 56 * 1024 * 1024

TOKEN_TILE = 512
FFN_TILE = 1024
FFN_SPLITS = (0, 6 * MXU_DIM, D_FF)
ATTN_Q = 512
ATTN_LANES = 256
S5_CHUNK = 16
S5_ROWS = 512
NEG_BIG = -1e30
LOG2E = 1.4426950408889634

BF16 = jnp.bfloat16
F32 = jnp.float32


def _dot(a, b):
    return jnp.dot(a, b, preferred_element_type=F32)


def _dot_nt(a, b):
    return lax.dot_general(a, b, (((1,), (1,)), ((), ())), preferred_element_type=F32)


def _rms(x, g):
    ms = jnp.mean(x * x, axis=-1, keepdims=True)
    return x * lax.rsqrt(ms + NORM_EPS) * g


def _const_spec(shape):
    return pl.BlockSpec(shape, lambda *_: (0,) * len(shape), pipeline_mode=pl.Buffered(1))


def _params(n_axes):
    return pltpu.CompilerParams(dimension_semantics=("arbitrary",) * n_axes,
                                vmem_limit_bytes=VMEM_LIMIT)


def _split3(x):
    p0 = x.astype(BF16)
    r1 = x - p0.astype(F32)
    p1 = r1.astype(BF16)
    p2 = (r1 - p1.astype(F32)).astype(BF16)
    return p0, p1, p2


def _ffn_kernel(x_ref, g_ref, gn_ref, wg_ref, wu_ref, wd_ref, o_ref, *h_out):
    x = x_ref[...]
    h = _rms(x, g_ref[...]).astype(BF16)
    acc = None
    for lo, hi in zip(FFN_SPLITS[:-1], FFN_SPLITS[1:]):
        sl = slice(lo, hi)
        a = _dot(h, wg_ref[:, sl])
        b = _dot(h, wu_ref[:, sl])
        act = (a * jax.nn.sigmoid(a) * b).astype(BF16)
        d = _dot(act, wd_ref[sl, :])
        acc = d if acc is None else acc + d
    xo = x + 0.5 * acc
    o_ref[...] = xo
    if h_out:
        h_out[0][...] = _rms(xo, gn_ref[...]).astype(BF16)


def _ffn(x, g, wg, wu, wd, g_next=None):
    t = x.shape[0]
    tm = min(FFN_TILE, t)
    emit = g_next is not None
    gn = g_next if emit else g
    row = pl.BlockSpec((tm, D_MODEL), lambda i: (i, 0))
    out_shape = [jax.ShapeDtypeStruct((t, D_MODEL), F32)]
    out_specs = [row]
    if emit:
        out_shape.append(jax.ShapeDtypeStruct((t, D_MODEL), BF16))
        out_specs.append(row)
    res = pl.pallas_call(
        _ffn_kernel,
        grid=(t // tm,),
        in_specs=[row, _const_spec((1, D_MODEL)), _const_spec((1, D_MODEL)),
                  _const_spec((D_MODEL, D_FF)), _const_spec((D_MODEL, D_FF)),
                  _const_spec((D_FF, D_MODEL))],
        out_specs=out_specs,
        out_shape=out_shape,
        compiler_params=_params(1),
        name="ffn",
    )(x, g.reshape(1, D_MODEL), gn.reshape(1, D_MODEL), wg, wu, wd)
    return (res[0], res[1]) if emit else (res[0], None)


def _inproj_kernel(tiles_per_seq, h_ref, wqk_ref, wv_ref, wf_ref, gains_ref, ones_ref, tri_ref,
                   fgb_ref, sel_ref, aug1_ref, qk_ref, vt_ref, aug_ref, carry_ref):
    i = pl.program_id(0)
    h = h_ref[...]
    ones_blk = ones_ref[...]
    for c in range(QK_COLS // 512):
        y = _dot(h, wqk_ref[:, c * 512:(c + 1) * 512])
        y2 = y * y
        hi = y2.astype(BF16)
        lo = (y2 - hi.astype(F32)).astype(BF16)
        ssq = jnp.concatenate(
            [_dot(hi[:, k * MXU_DIM:(k + 1) * MXU_DIM], ones_blk)
             + _dot(lo[:, k * MXU_DIM:(k + 1) * MXU_DIM], ones_blk)
             for k in range(512 // MXU_DIM)], axis=1)
        y = y * lax.rsqrt(ssq * (1.0 / HEAD_DIM) + NORM_EPS) * gains_ref[c:c + 1, :]
        qk_ref[:, c * 512:(c + 1) * 512] = y.astype(BF16)
    for c in range(V_COLS // 512):
        vt = _dot_nt(wv_ref[c * 512:(c + 1) * 512, :], h)
        vt_ref[c * 512:(c + 1) * 512, :] = vt.astype(BF16)

    fl = _dot(h, wf_ref[...]) + fgb_ref[...]
    logf = -(jnp.maximum(-fl, 0.0) + jnp.log1p(jnp.exp(-jnp.abs(fl))))

    @pl.when(i % tiles_per_seq == 0)
    def _():
        carry_ref[...] = jnp.zeros_like(carry_ref)

    tri = tri_ref[...]
    cum = sum(_dot(tri, p) for p in _split3(logf)) + carry_ref[...]
    carry_ref[...] = cum[cum.shape[0] - 1:, :]
    aug = sum(_dot(p, sel_ref[n]) for n, p in enumerate(_split3(cum * LOG2E))) + aug1_ref[...]
    aug_ref[...] = aug.astype(BF16)


def _decay_lane_maps():
    sel = np.zeros((3, LANES, AUG_COLS), np.float32)
    ones = np.zeros((1, AUG_COLS), np.float32)
    k_off = AUG_COLS // 2
    for head in range(N_FOX_HEADS):
        base = (head // 2) * LANES + (head % 2) * 6
        for n in range(3):
            sel[n, head, base + n] = 1.0
            sel[n, head, k_off + base + 3 + n] = -1.0
            ones[0, base + 3 + n] = 1.0
            ones[0, k_off + base + n] = 1.0
    return jnp.asarray(sel, BF16), jnp.asarray(ones, F32)


def _inproj(h, w_qk, w_v, w_f, gains, fg_bias_row, bsz, seq):
    t = h.shape[0]
    tm = min(TOKEN_TILE, seq)
    tps = seq // tm
    ones_blk = jnp.kron(jnp.eye(MXU_DIM // HEAD_DIM, dtype=F32),
                        jnp.ones((HEAD_DIM, HEAD_DIM), F32)).astype(BF16)
    tri = jnp.tril(jnp.ones((tm, tm), F32)).astype(BF16)
    sel, aug_ones = _decay_lane_maps()
    return pl.pallas_call(
        functools.partial(_inproj_kernel, tps),
        grid=(t // tm,),
        in_specs=[pl.BlockSpec((tm, D_MODEL), lambda i: (i, 0)),
                  _const_spec((D_MODEL, QK_COLS)), _const_spec((D_MODEL, V_COLS)),
                  _const_spec((D_MODEL, LANES)),
                  _const_spec((QK_COLS // 512, 512)), _const_spec((MXU_DIM, MXU_DIM)),
                  _const_spec((tm, tm)), _const_spec((1, LANES)),
                  _const_spec((3, LANES, AUG_COLS)), _const_spec((1, AUG_COLS))],
        out_specs=[pl.BlockSpec((tm, QK_COLS), lambda i: (i, 0)),
                   pl.BlockSpec((None, V_COLS, tm), lambda i: (i // tps, 0, i % tps)),
                   pl.BlockSpec((tm, AUG_COLS), lambda i: (i, 0))],
        out_shape=[jax.ShapeDtypeStruct((t, QK_COLS), BF16),
                   jax.ShapeDtypeStruct((bsz, V_COLS, seq), BF16),
                   jax.ShapeDtypeStruct((t, AUG_COLS), BF16)],
        scratch_shapes=[pltpu.VMEM((1, LANES), F32)],
        compiler_params=_params(1),
        name="attn_inproj",
    )(h, w_qk, w_v, w_f, gains, ones_blk, tri, fg_bias_row, sel, aug_ones)


def _attn_sweep(q_both, k_ref, kaug_ref, vt_ref, bias_ref, m_ref, l_ref, acc_ref):
    qi = pl.program_id(2)
    blk = q_both.shape[0] // 2
    m_ref[...] = jnp.full_like(m_ref, NEG_BIG)
    l_ref[...] = jnp.zeros_like(l_ref)
    acc_ref[...] = jnp.zeros_like(acc_ref)

    def run(items):
        rows = [pl.ds(pl.multiple_of(j * blk, blk), blk) for j, _, _ in items]
        width = 2 * blk if bias_ref is not None else ATTN_LANES
        lanes = [slice(c * width, (c + 1) * width) for c in range(2 * blk // width)]
        scores = {}
        for n, r in enumerate(rows):
            k_blk = k_ref[r, :]
            if kaug_ref is not None:
                k_blk = jnp.concatenate([k_blk, kaug_ref[r, :]], axis=1)
            for c, ls in enumerate(lanes):
                scores[n, c] = _dot_nt(k_blk, q_both[ls, :])
        for n, (_, bias_idx, masked) in enumerate(items):
            for c, ls in enumerate(lanes):
                s = scores.pop((n, c))
                if bias_idx is not None and bias_ref is not None:
                    s = s + bias_ref[bias_idx, :, ls]
                if masked:
                    key = lax.broadcasted_iota(jnp.int32, s.shape, 0)
                    qry = (lax.broadcasted_iota(jnp.int32, s.shape, 1) + ls.start) & (blk - 1)
                    s = jnp.where(key <= qry, s, NEG_BIG)
                m_prev = m_ref[:, ls]
                m_new = jnp.maximum(m_prev, jnp.max(s, axis=0, keepdims=True))
                alpha = jnp.exp2(m_prev - m_new)
                p = jnp.exp2(s - m_new)
                l_ref[:, ls] = alpha * l_ref[:, ls] + jnp.sum(p, axis=0, keepdims=True)
                m_ref[:, ls] = m_new
                acc_ref[:, ls] = alpha * acc_ref[:, ls] + _dot(vt_ref[:, rows[n]], p.astype(BF16))

    n_far = jnp.maximum(qi - 1, 0)

    def far_pair(j, carry):
        run([(2 * j, None, False), (2 * j + 1, None, False)])
        return carry

    lax.fori_loop(0, n_far // 2, far_pair, 0)
    prev, diag = (qi - 1, 1, False), (qi, 0, True)

    @pl.when(qi == 0)
    def _():
        run([diag])

    @pl.when((qi >= 1) & (n_far % 2 == 0))
    def _():
        run([prev, diag])

    @pl.when(n_far % 2 == 1)
    def _():
        run([(n_far - 1, None, False), prev, diag])

    halves = (slice(None), slice(0, blk)), (slice(None), slice(blk, 2 * blk))
    return [acc_ref[sl] / l_ref[sl] for sl in halves]


def _diff_attn_kernel(scal_ref, q_ref, k_ref, vt_ref, bias_ref, subln_ref, o_ref,
                      m_ref, l_ref, acc_ref):
    q = q_ref[...]
    lane = lax.broadcasted_iota(jnp.int32, q.shape, 1)
    zero = jnp.zeros_like(q)
    q_both = jnp.concatenate([jnp.where(lane < HEAD_DIM, q, zero),
                              jnp.where(lane < HEAD_DIM, zero, q)], axis=0)
    o1, o2 = _attn_sweep(q_both, k_ref, None, vt_ref, bias_ref, m_ref, l_ref, acc_ref)
    lam = scal_ref[0]
    out_scale = scal_ref[1]
    o = o1 - lam * o2
    ms = jnp.mean(o * o, axis=0, keepdims=True)
    o = o * lax.rsqrt(ms + NORM_EPS) * subln_ref[...] * out_scale
    o_ref[...] = o.T.astype(o_ref.dtype)


def _fox_attn_kernel(q_ref, qaug_ref, k_ref, kaug_ref, vt_ref, o_ref, m_ref, l_ref, acc_ref):
    q = jnp.concatenate([q_ref[...], qaug_ref[...]], axis=1)
    lane = lax.broadcasted_iota(jnp.int32, q.shape, 1)
    zero = jnp.zeros_like(q)
    in_a = (lane < HEAD_DIM) | ((lane >= LANES) & (lane < LANES + 6))
    in_b = ((lane >= HEAD_DIM) & (lane < LANES)) | ((lane >= LANES + 6) & (lane < LANES + 12))
    q_both = jnp.concatenate([jnp.where(in_a, q, zero), jnp.where(in_b, q, zero)], axis=0)
    o1, o2 = _attn_sweep(q_both, k_ref, kaug_ref, vt_ref, None, m_ref, l_ref, acc_ref)
    row = lax.broadcasted_iota(jnp.int32, o1.shape, 0)
    o_ref[...] = jnp.where(row < HEAD_DIM, o1, o2).T.astype(o_ref.dtype)


def _attn_scratch(tq):
    return [pltpu.VMEM((1, 2 * tq), F32), pltpu.VMEM((1, 2 * tq), F32),
            pltpu.VMEM((LANES, 2 * tq), F32)]


def _diff_attn(qk, vt, bias_tiles, scal, subln_col):
    bsz, seq, _ = qk.shape
    tq = min(ATTN_Q, seq)
    return pl.pallas_call(
        _diff_attn_kernel,
        grid=(bsz, N_DIFF_HEADS, seq // tq),
        in_specs=[pl.BlockSpec(memory_space=pltpu.SMEM),
                  pl.BlockSpec((None, tq, LANES), lambda b, s, i: (b, i, s)),
                  pl.BlockSpec((None, seq, LANES), lambda b, s, i: (b, 0, 4 + s)),
                  pl.BlockSpec((None, LANES, seq), lambda b, s, i: (b, s, 0)),
                  pl.BlockSpec((None, 2, tq, 2 * tq), lambda b, s, i: (s, 0, 0, 0)),
                  _const_spec((LANES, 1))],
        out_specs=pl.BlockSpec((None, tq, LANES), lambda b, s, i: (b, i, s)),
        out_shape=jax.ShapeDtypeStruct((bsz, seq, N_DIFF_HEADS * LANES), BF16),
        scratch_shapes=_attn_scratch(tq),
        compiler_params=_params(3),
        name="diff_attn",
    )(scal, qk, qk, vt, bias_tiles, subln_col)


def _fox_attn(qk, aug, vt):
    bsz, seq, _ = qk.shape
    tq = min(ATTN_Q, seq)
    return pl.pallas_call(
        _fox_attn_kernel,
        grid=(bsz, N_FOX_HEADS // 2, seq // tq),
        in_specs=[pl.BlockSpec((None, tq, LANES), lambda b, s, i: (b, i, 8 + s)),
                  pl.BlockSpec((None, tq, LANES), lambda b, s, i: (b, i, s)),
                  pl.BlockSpec((None, seq, LANES), lambda b, s, i: (b, 0, 12 + s)),
                  pl.BlockSpec((None, seq, LANES), lambda b, s, i: (b, 0, 4 + s)),
                  pl.BlockSpec((None, LANES, seq), lambda b, s, i: (b, 4 + s, 0))],
        out_specs=pl.BlockSpec((None, tq, LANES), lambda b, s, i: (b, i, s)),
        out_shape=jax.ShapeDtypeStruct((bsz, seq, N_FOX_HEADS * HEAD_DIM), BF16),
        scratch_shapes=_attn_scratch(tq),
        compiler_params=_params(3),
        name="fox_attn",
    )(qk, aug, qk, aug, vt)


def _outproj_kernel(x_ref, od_ref, of_ref, wd_ref, wf_ref, o_ref):
    o_ref[...] = x_ref[...] + _dot(od_ref[...], wd_ref[...]) + _dot(of_ref[...], wf_ref[...])


def _outproj(x, od, of, w_d, w_f):
    t = x.shape[0]
    tm = min(TOKEN_TILE, t)
    half = od.shape[1]
    return pl.pallas_call(
        _outproj_kernel,
        grid=(t // tm,),
        in_specs=[pl.BlockSpec((tm, D_MODEL), lambda i: (i, 0)),
                  pl.BlockSpec((tm, half), lambda i: (i, 0)),
                  pl.BlockSpec((tm, half), lambda i: (i, 0)),
                  _const_spec((half, D_MODEL)), _const_spec((half, D_MODEL))],
        out_specs=pl.BlockSpec((tm, D_MODEL), lambda i: (i, 0)),
        out_shape=jax.ShapeDtypeStruct((t, D_MODEL), F32),
        compiler_params=_params(1),
        name="attn_outproj",
    )(x, od, of, w_d, w_f)


def _rel_bias_by_distance(rel_bias, n_dist):
    n = jnp.arange(n_dist, dtype=jnp.int32)
    max_exact = N_REL_BUCKETS // 2
    nf = jnp.maximum(n, 1).astype(F32)
    large = max_exact + (jnp.log(nf / max_exact) / math.log(REL_MAX_DIST / max_exact)
                         * (N_REL_BUCKETS - max_exact)).astype(jnp.int32)
    large = jnp.minimum(large, N_REL_BUCKETS - 1)
    return rel_bias[jnp.where(n < max_exact, n, large)]


def _toeplitz(w, n):
    heads, period = w.shape
    flat = jnp.tile(w, (1, n))[:, :n * (period - 1)]
    return flat.reshape(heads, n, period - 1)[:, :, :n]


def _diff_bias_tiles(rel_bias, tq):
    by_dist = _rel_bias_by_distance(rel_bias.astype(F32), 2 * tq)
    by_dist = ((by_dist - by_dist[2 * tq - 1]) * LOG2E).T
    diag = _toeplitz(by_dist, tq)
    prev = _toeplitz(jnp.roll(by_dist, -tq, axis=1), tq)
    tiles = jnp.stack([diag, prev], axis=1)
    return jnp.concatenate([tiles, tiles], axis=-1)


def _attn_layer(x, h, bsz, seq, layer_idx, w_in, w_out, fg_bias, dq_g, dk_g, lq1, lk1, lq2, lk2,
                subln_g, fq_g, fk_g, bias_tiles):
    tq = min(ATTN_Q, seq)
    q_scale = HEAD_DIM ** -0.5 * LOG2E
    w_bf = w_in.astype(BF16)
    w_qk = jnp.concatenate([w_bf[:, 0:1024], w_bf[:, 1536:2560]], axis=1)
    w_v = jnp.concatenate([w_bf[:, 1024:1536], w_bf[:, 2560:3072]], axis=1).T
    w_f = jnp.pad(w_bf[:, 3072:], ((0, 0), (0, LANES - N_FOX_HEADS)))
    fgb = jnp.pad(fg_bias.astype(F32), (0, LANES - N_FOX_HEADS)).reshape(1, LANES)
    gains = jnp.stack([jnp.tile(dq_g, 8) * q_scale, jnp.tile(dk_g, 8),
                       jnp.tile(fq_g, 8) * q_scale, jnp.tile(fk_g, 8)]).astype(F32)
    qk, vt, aug = _inproj(h, w_qk, w_v, w_f, gains, fgb, bsz, seq)
    qk = qk.reshape(bsz, seq, QK_COLS)

    lam_init = 0.8 - 0.6 * math.exp(-0.3 * layer_idx)
    lam = (jnp.exp(jnp.sum(lq1.astype(F32) * lk1.astype(F32)))
           - jnp.exp(jnp.sum(lq2.astype(F32) * lk2.astype(F32))) + lam_init)
    scal = jnp.stack([lam, jnp.asarray(1.0 - lam_init, F32)]).astype(F32)
    od = _diff_attn(qk, vt, bias_tiles, scal, subln_g.astype(F32).reshape(LANES, 1))

    of = _fox_attn(qk, aug.reshape(bsz, seq, AUG_COLS), vt)

    w_o = w_out.astype(BF16)
    half = N_DIFF_HEADS * LANES
    return _outproj(x, od.reshape(bsz * seq, half), of.reshape(bsz * seq, half),
                    w_o[:half], w_o[half:])


def _s5_kernel(n_chunks, *refs):
    x_refs = refs[:S5_CHUNK]
    t_ref, p_ref, q_ref, a_ref, y_ref, z_ref, xp_ref = refs[S5_CHUNK:]
    x = jnp.concatenate([r[...] for r in x_refs], axis=1)
    n_seq = x.shape[0] // n_chunks
    half = z_ref.shape[1] // 2
    z_ref[...] = _dot(x, p_ref[...])
    a_re = a_ref[0:1, :]
    a_im = a_ref[1:2, :]

    def scan(c, carry):
        out = []
        for b in range(n_seq):
            x_re, x_im = carry[2 * b], carry[2 * b + 1]
            row = pl.ds(b * n_chunks + c, 1)
            xp_ref[row, :half] = x_re
            xp_ref[row, half:] = x_im
            z = z_ref[row, :]
            out += [a_re * x_re - a_im * x_im + z[:, :half],
                    a_re * x_im + a_im * x_re + z[:, half:]]
        return tuple(out)

    zero = jnp.zeros((1, half), F32)
    lax.fori_loop(0, n_chunks, scan, (zero,) * (2 * n_seq))
    y = _dot(x, t_ref[...]) + _dot(xp_ref[...].astype(BF16), q_ref[...])
    for t in range(S5_CHUNK):
        y_ref[t] = y[:, t * LANES:(t + 1) * LANES]


def _s5_operators(a_re, a_im, log_step, b_re, b_im, c_re, c_im):
    hp = lax.Precision.HIGHEST
    L, kb, gb = S5_CHUNK, S5_GROUPS // 8, 8
    step = jnp.exp(log_step.astype(F32))[:, None]
    ar, ai = a_re.astype(F32), a_im.astype(F32)
    mag = jnp.exp(ar * step)
    lr, li = mag * jnp.cos(ai * step), mag * jnp.sin(ai * step)
    den = ar * ar + ai * ai
    cr = (((lr - 1.0) * ar + li * ai) / den)[:, None, :]
    ci = ((li * ar - (lr - 1.0) * ai) / den)[:, None, :]
    br, bi = b_re.astype(F32).transpose(0, 2, 1), b_im.astype(F32).transpose(0, 2, 1)
    bbr, bbi = cr * br - ci * bi, cr * bi + ci * br
    pr, pi = [jnp.ones_like(lr)], [jnp.zeros_like(li)]
    for _ in range(L):
        pr, pi = pr + [pr[-1] * lr - pi[-1] * li], pi + [pr[-1] * li + pi[-1] * lr]
    pr, pi = jnp.stack(pr), jnp.stack(pi)
    wr = pr[:L, :, None, :] * bbr - pi[:L, :, None, :] * bbi
    wi = pr[:L, :, None, :] * bbi + pi[:L, :, None, :] * bbr
    cre, cim = c_re.astype(F32), c_im.astype(F32)
    kern = (jnp.einsum('gcp,ngap->ngac', cre, wr, precision=hp)
            - jnp.einsum('gcp,ngap->ngac', cim, wi, precision=hp))

    def group_mask(rows, rows_per_group, cols, cols_per_group):
        r = lax.broadcasted_iota(jnp.int32, (rows, 1), 0) // rows_per_group
        c = lax.broadcasted_iota(jnp.int32, (1, cols), 1) // cols_per_group
        return r == c

    def spread_channels(m):
        src = lax.broadcasted_iota(jnp.int32, (S5_GROUP, LANES), 0)
        dst = lax.broadcasted_iota(jnp.int32, (S5_GROUP, LANES), 1)
        return jnp.dot(m.astype(BF16), (src == dst % S5_GROUP).astype(BF16),
                       preferred_element_type=BF16)

    zero = jnp.zeros((), BF16)
    k_blk = kern.reshape(L, kb, LANES, S5_GROUP).transpose(1, 0, 2, 3)
    k_blk = jnp.where(group_mask(LANES, S5_GROUP, LANES, S5_GROUP), spread_channels(k_blk), zero)
    nil = jnp.zeros_like(k_blk[:, 0])
    t_big = jnp.concatenate(
        [jnp.concatenate([k_blk[:, t - s] if t >= s else nil for t in range(L)], axis=2)
         for s in range(L)], axis=1)

    def state_in(w):
        w = jnp.tile(w[::-1].astype(BF16).reshape(L, kb, LANES, S5_STATE), (1, 1, 1, gb))
        w = jnp.where(group_mask(LANES, S5_GROUP, gb * S5_STATE, S5_STATE), w, zero)
        return w.transpose(1, 0, 2, 3).reshape(kb, L * LANES, gb * S5_STATE)
    p_big = jnp.concatenate([state_in(wr), state_in(wi)], axis=2)

    cre_t, cim_t = cre.transpose(0, 2, 1), cim.transpose(0, 2, 1)
    def state_out(m):
        m = spread_channels(m.reshape(L, kb, gb * S5_STATE, S5_GROUP))
        m = jnp.where(group_mask(gb * S5_STATE, S5_STATE, LANES, S5_GROUP), m, zero)
        return m.transpose(1, 2, 0, 3).reshape(kb, gb * S5_STATE, L * LANES)
    q_re = cre_t[None] * pr[1:, :, :, None] - cim_t[None] * pi[1:, :, :, None]
    q_im = -(cre_t[None] * pi[1:, :, :, None] + cim_t[None] * pr[1:, :, :, None])
    q_big = jnp.concatenate([state_out(q_re), state_out(q_im)], axis=1)
    a_big = jnp.stack([pr[L].reshape(kb, gb * S5_STATE), pi[L].reshape(kb, gb * S5_STATE)], axis=1)
    return t_big, p_big, q_big, a_big


def _s5_core(h, bsz, seq, a_re, a_im, log_step, b_re, b_im, c_re, c_im):
    L = S5_CHUNK
    n_chunks = seq // L
    rows = bsz * n_chunks
    kb = D_MODEL // LANES
    rb = min(S5_ROWS, rows)
    assert rb % n_chunks == 0
    t_big, p_big, q_big, a_big = _s5_operators(a_re, a_im, log_step, b_re, b_im, c_re, c_im)
    n_state = 8 * S5_STATE
    h2 = h.reshape(rows, L * D_MODEL)
    x_specs = [pl.BlockSpec((rb, LANES), lambda k, r, t=t: (r, t * kb + k)) for t in range(L)]
    return pl.pallas_call(
        functools.partial(_s5_kernel, n_chunks),
        grid=(kb, rows // rb),
        in_specs=x_specs + [
            pl.BlockSpec((None, L * LANES, L * LANES), lambda k, r: (k, 0, 0)),
            pl.BlockSpec((None, L * LANES, 2 * n_state), lambda k, r: (k, 0, 0)),
            pl.BlockSpec((None, 2 * n_state, L * LANES), lambda k, r: (k, 0, 0)),
            pl.BlockSpec((None, 2, n_state), lambda k, r: (k, 0, 0))],
        out_specs=pl.BlockSpec((L, rb, LANES), lambda k, r: (0, r, k)),
        out_shape=jax.ShapeDtypeStruct((L, rows, D_MODEL), F32),
        scratch_shapes=[pltpu.VMEM((rb, 2 * n_state), F32), pltpu.VMEM((rb, 2 * n_state), F32)],
        compiler_params=_params(2),
        name="s5_core",
    )(*([h2] * L), t_big, p_big, q_big, a_big)


def _glu_kernel(x_ref, y_ref, g_ref, d_ref, wa_ref, wb_ref, o_ref):
    x = x_ref[...]
    y = y_ref[...] + d_ref[...] * _rms(x, g_ref[...])
    y = jax.nn.gelu(y).astype(BF16)
    o_ref[...] = x + _dot(y, wa_ref[...]) * jax.nn.sigmoid(_dot(y, wb_ref[...]))


def _s5_layer(x, h, bsz, seq, mix_g, a_re, a_im, log_step, b_re, b_im, c_re, c_im, d_skip,
              w_a, w_b):
    L = S5_CHUNK
    rows = x.shape[0] // L
    rg = min(TOKEN_TILE, rows)
    y = _s5_core(h, bsz, seq, a_re, a_im, log_step, b_re, b_im, c_re, c_im)
    tile = pl.BlockSpec((rg, D_MODEL), lambda r, t: (r, t))
    out = pl.pallas_call(
        _glu_kernel,
        grid=(rows // rg, L),
        in_specs=[tile, pl.BlockSpec((None, rg, D_MODEL), lambda r, t: (t, r, 0)),
                  _const_spec((1, D_MODEL)), _const_spec((1, D_MODEL)),
                  _const_spec((D_MODEL, D_MODEL)), _const_spec((D_MODEL, D_MODEL))],
        out_specs=tile,
        out_shape=jax.ShapeDtypeStruct((rows, L * D_MODEL), F32),
        compiler_params=_params(2),
        name="s5_glu",
    )(x.reshape(rows, L * D_MODEL), y, mix_g.astype(F32).reshape(1, D_MODEL),
      d_skip.astype(F32).reshape(1, D_MODEL), w_a.astype(BF16), w_b.astype(BF16))
    return out.reshape(rows * L, D_MODEL)


def kernel(x, ffn1_norm, ffn1_gate, ffn1_up, ffn1_down, mix_norm, ffn2_norm, ffn2_gate, ffn2_up, ffn2_down, attn_w_in, attn_w_out, fg_bias, diff_q_norm, diff_k_norm, diff_lambda_q1, diff_lambda_k1, diff_lambda_q2, diff_lambda_k2, diff_subln, fox_q_norm, fox_k_norm, rel_bias, s5_a_re, s5_a_im, s5_log_step, s5_b_re, s5_b_im, s5_c_re, s5_c_im, s5_d, s5_glu_a, s5_glu_b):
    bsz, seq, _ = x.shape
    depth = ffn1_norm.shape[0]
    assert min(ATTN_Q, seq) >= REL_MAX_DIST, "bias tiles assume the bias saturates within a block"
    bias_tiles = _diff_bias_tiles(rel_bias, min(ATTN_Q, seq))
    xt = x.reshape(bsz * seq, D_MODEL).astype(F32)
    for i in range(depth):
        xt, h = _ffn(xt, ffn1_norm[i], ffn1_gate[i].astype(BF16), ffn1_up[i].astype(BF16),
                     ffn1_down[i].astype(BF16), g_next=mix_norm[i])
        j = i // 2
        if i % 2 == 0:
            xt = _attn_layer(xt, h, bsz, seq, i, attn_w_in[j], attn_w_out[j], fg_bias[j],
                             diff_q_norm[j], diff_k_norm[j], diff_lambda_q1[j], diff_lambda_k1[j],
                             diff_lambda_q2[j], diff_lambda_k2[j], diff_subln[j], fox_q_norm[j],
                             fox_k_norm[j], bias_tiles)
        else:
            xt = _s5_layer(xt, h, bsz, seq, mix_norm[i], s5_a_re[j], s5_a_im[j], s5_log_step[j],
                           s5_b_re[j], s5_b_im[j], s5_c_re[j], s5_c_im[j], s5_d[j],
                           s5_glu_a[j], s5_glu_b[j])
        xt, _ = _ffn(xt, ffn2_norm[i], ffn2_gate[i].astype(BF16), ffn2_up[i].astype(BF16),
                     ffn2_down[i].astype(BF16))
    return xt.reshape(bsz, seq, D_MODEL).astype(x.dtype)
```

```python
import functools
import math

import jax
import jax.numpy as jnp
import numpy as np
from jax import lax
from jax.experimental import pallas as pl
from jax.experimental.pallas import tpu as pltpu

D_MODEL = 1024
D_FF = 2816
HEAD_DIM = 64
N_DIFF_HEADS = 4
N_FOX_HEADS = 8
N_REL_BUCKETS = 32
REL_MAX_DIST = 128
S5_GROUP = 16
S5_GROUPS = D_MODEL // S5_GROUP
S5_STATE = 64
NORM_EPS = 1e-6

QK_COLS = 2048
V_COLS = 1024
AUG_COLS = 1024
LANES = 128
MXU_DIM = 256
VMEM_LIMIT = 56 * 1024 * 1024

TOKEN_TILE = 512
FFN_TILE = 1024
FFN_SPLITS = (0, 6 * MXU_DIM, D_FF)
ATTN_Q = 512
ATTN_LANES = 256
S5_CHUNK = 16
S5_ROWS = 512
NEG_BIG = -1e30
LOG2E = 1.4426950408889634

BF16 = jnp.bfloat16
F32 = jnp.float32


def _dot(a, b):
    return jnp.dot(a, b, preferred_element_type=F32)


def _dot_nt(a, b):
    return lax.dot_general(a, b, (((1,), (1,)), ((), ())), preferred_element_type=F32)


def _rms(x, g):
    ms = jnp.mean(x * x, axis=-1, keepdims=True)
    return x * lax.rsqrt(ms + NORM_EPS) * g


def _const_spec(shape):
    return pl.BlockSpec(shape, lambda *_: (0,) * len(shape), pipeline_mode=pl.Buffered(1))


def _params(n_axes):
    return pltpu.CompilerParams(dimension_semantics=("arbitrary",) * n_axes,
                                vmem_limit_bytes=VMEM_LIMIT)


def _split3(x):
    p0 = x.astype(BF16)
    r1 = x - p0.astype(F32)
    p1 = r1.astype(BF16)
    p2 = (r1 - p1.astype(F32)).astype(BF16)
    return p0, p1, p2


def _ffn_kernel(chunked, x_ref, g_ref, gn_ref, wg_ref, wu_ref, wd_ref, o_ref, *rest):
    x = x_ref[...]
    h = _rms(x, g_ref[...]).astype(BF16)
    acc = None
    for lo, hi in zip(FFN_SPLITS[:-1], FFN_SPLITS[1:]):
        sl = slice(lo, hi)
        a = _dot(h, wg_ref[:, sl])
        b = _dot(h, wu_ref[:, sl])
        act = (a * jax.nn.sigmoid(a) * b).astype(BF16)
        d = _dot(act, wd_ref[sl, :])
        acc = d if acc is None else acc + d
    xo = x + 0.5 * acc
    o_ref[...] = xo
    if not rest:
        return
    hn = _rms(xo, gn_ref[...])
    if not chunked:
        rest[0][...] = hn.astype(BF16)
        return
    h_ref, stage_ref = rest
    n_rows = h_ref.shape[0]
    for k in range(D_MODEL // LANES):
        stage_ref[k] = hn[:, k * LANES:(k + 1) * LANES]
    for t in range(S5_CHUNK):
        for k in range(D_MODEL // LANES):
            lo = t * D_MODEL + k * LANES
            h_ref[:, lo:lo + LANES] = stage_ref[k, pl.ds(t, n_rows, stride=S5_CHUNK), :].astype(BF16)


def _ffn(x, g, wg, wu, wd, g_next=None, chunked=False):
    t = x.shape[0]
    tm = min(FFN_TILE, t)
    emit = g_next is not None
    gn = g_next if emit else g
    row = pl.BlockSpec((tm, D_MODEL), lambda i: (i, 0))
    out_shape = [jax.ShapeDtypeStruct((t, D_MODEL), F32)]
    out_specs = [row]
    scratch = []
    if emit and chunked:
        out_shape.append(jax.ShapeDtypeStruct((t // S5_CHUNK, S5_CHUNK * D_MODEL), BF16))
        out_specs.append(pl.BlockSpec((tm // S5_CHUNK, S5_CHUNK * D_MODEL), lambda i: (i, 0)))
        scratch = [pltpu.VMEM((D_MODEL // LANES, tm, LANES), F32)]
    elif emit:
        out_shape.append(jax.ShapeDtypeStruct((t, D_MODEL), BF16))
        out_specs.append(row)
    res = pl.pallas_call(
        functools.partial(_ffn_kernel, chunked),
        grid=(t // tm,),
        in_specs=[row, _const_spec((1, D_MODEL)), _const_spec((1, D_MODEL)),
                  _const_spec((D_MODEL, D_FF)), _const_spec((D_MODEL, D_FF)),
                  _const_spec((D_FF, D_MODEL))],
        out_specs=out_specs,
        out_shape=out_shape,
        scratch_shapes=scratch,
        compiler_params=_params(1),
        name="ffn",
    )(x, g.reshape(1, D_MODEL), gn.reshape(1, D_MODEL), wg, wu, wd)
    return (res[0], res[1]) if emit else (res[0], None)


def _inproj_kernel(tiles_per_seq, h_ref, wqk_ref, wv_ref, wf_ref, gains_ref, ones_ref, tri_ref,
                   fgb_ref, sel_ref, aug1_ref, qk_ref, vt_ref, aug_ref, carry_ref):
    i = pl.program_id(0)
    h = h_ref[...]
    ones_blk = ones_ref[...]
    for c in range(QK_COLS // 512):
        y = _dot(h, wqk_ref[:, c * 512:(c + 1) * 512])
        y2 = y * y
        hi = y2.astype(BF16)
        lo = (y2 - hi.astype(F32)).astype(BF16)
        ssq = jnp.concatenate(
            [_dot(hi[:, k * MXU_DIM:(k + 1) * MXU_DIM], ones_blk)
             + _dot(lo[:, k * MXU_DIM:(k + 1) * MXU_DIM], ones_blk)
             for k in range(512 // MXU_DIM)], axis=1)
        y = y * lax.rsqrt(ssq * (1.0 / HEAD_DIM) + NORM_EPS) * gains_ref[c:c + 1, :]
        qk_ref[:, c * 512:(c + 1) * 512] = y.astype(BF16)
    for c in range(V_COLS // 512):
        vt = _dot_nt(wv_ref[c * 512:(c + 1) * 512, :], h)
        vt_ref[c * 512:(c + 1) * 512, :] = vt.astype(BF16)

    fl = _dot(h, wf_ref[...]) + fgb_ref[...]
    logf = -(jnp.maximum(-fl, 0.0) + jnp.log1p(jnp.exp(-jnp.abs(fl))))

    @pl.when(i % tiles_per_seq == 0)
    def _():
        carry_ref[...] = jnp.zeros_like(carry_ref)

    tri = tri_ref[...]
    cum = sum(_dot(tri, p) for p in _split3(logf)) + carry_ref[...]
    carry_ref[...] = cum[cum.shape[0] - 1:, :]
    aug = sum(_dot(p, sel_ref[n]) for n, p in enumerate(_split3(cum * LOG2E))) + aug1_ref[...]
    aug_ref[...] = aug.astype(BF16)


def _decay_lane_maps():
    sel = np.zeros((3, LANES, AUG_COLS), np.float32)
    ones = np.zeros((1, AUG_COLS), np.float32)
    k_off = AUG_COLS // 2
    for head in range(N_FOX_HEADS):
        base = (head // 2) * LANES + (head % 2) * 6
        for n in range(3):
            sel[n, head, base + n] = 1.0
            sel[n, head, k_off + base + 3 + n] = -1.0
            ones[0, base + 3 + n] = 1.0
            ones[0, k_off + base + n] = 1.0
    return jnp.asarray(sel, BF16), jnp.asarray(ones, F32)


def _inproj(h, w_qk, w_v, w_f, gains, fg_bias_row, bsz, seq):
    t = h.shape[0]
    tm = min(TOKEN_TILE, seq)
    tps = seq // tm
    ones_blk = jnp.kron(jnp.eye(MXU_DIM // HEAD_DIM, dtype=F32),
                        jnp.ones((HEAD_DIM, HEAD_DIM), F32)).astype(BF16)
    tri = jnp.tril(jnp.ones((tm, tm), F32)).astype(BF16)
    sel, aug_ones = _decay_lane_maps()
    return pl.pallas_call(
        functools.partial(_inproj_kernel, tps),
        grid=(t // tm,),
        in_specs=[pl.BlockSpec((tm, D_MODEL), lambda i: (i, 0)),
                  _const_spec((D_MODEL, QK_COLS)), _const_spec((D_MODEL, V_COLS)),
                  _const_spec((D_MODEL, LANES)),
                  _const_spec((QK_COLS // 512, 512)), _const_spec((MXU_DIM, MXU_DIM)),
                  _const_spec((tm, tm)), _const_spec((1, LANES)),
                  _const_spec((3, LANES, AUG_COLS)), _const_spec((1, AUG_COLS))],
        out_specs=[pl.BlockSpec((tm, QK_COLS), lambda i: (i, 0)),
                   pl.BlockSpec((None, V_COLS, tm), lambda i: (i // tps, 0, i % tps)),
                   pl.BlockSpec((tm, AUG_COLS), lambda i: (i, 0))],
        out_shape=[jax.ShapeDtypeStruct((t, QK_COLS), BF16),
                   jax.ShapeDtypeStruct((bsz, V_COLS, seq), BF16),
                   jax.ShapeDtypeStruct((t, AUG_COLS), BF16)],
        scratch_shapes=[pltpu.VMEM((1, LANES), F32)],
        compiler_params=_params(1),
        name="attn_inproj",
    )(h, w_qk, w_v, w_f, gains, ones_blk, tri, fg_bias_row, sel, aug_ones)


def _attn_sweep(q_both, k_ref, kaug_ref, vt_ref, bias_ref, m_ref, l_ref, acc_ref):
    qi = pl.program_id(2)
    blk = q_both.shape[0] // 2
    m_ref[...] = jnp.full_like(m_ref, NEG_BIG)
    l_ref[...] = jnp.zeros_like(l_ref)
    acc_ref[...] = jnp.zeros_like(acc_ref)

    def run(items):
        rows = [pl.ds(pl.multiple_of(j * blk, blk), blk) for j, _, _ in items]
        width = 2 * blk if bias_ref is not None else ATTN_LANES
        lanes = [slice(c * width, (c + 1) * width) for c in range(2 * blk // width)]
        scores = {}
        for n, r in enumerate(rows):
            k_blk = k_ref[r, :]
            if kaug_ref is not None:
                k_blk = jnp.concatenate([k_blk, kaug_ref[r, :]], axis=1)
            for c, ls in enumerate(lanes):
                scores[n, c] = _dot_nt(k_blk, q_both[ls, :])
        for n, (_, bias_idx, masked) in enumerate(items):
            for c, ls in enumerate(lanes):
                s = scores.pop((n, c))
                if bias_idx is not None and bias_ref is not None:
                    s = s + bias_ref[bias_idx, :, ls]
                if masked:
                    key = lax.broadcasted_iota(jnp.int32, s.shape, 0)
                    qry = (lax.broadcasted_iota(jnp.int32, s.shape, 1) + ls.start) & (blk - 1)
                    s = jnp.where(key <= qry, s, NEG_BIG)
                m_prev = m_ref[:, ls]
                m_new = jnp.maximum(m_prev, jnp.max(s, axis=0, keepdims=True))
                alpha = jnp.exp2(m_prev - m_new)
                p = jnp.exp2(s - m_new)
                l_ref[:, ls] = alpha * l_ref[:, ls] + jnp.sum(p, axis=0, keepdims=True)
                m_ref[:, ls] = m_new
                acc_ref[:, ls] = alpha * acc_ref[:, ls] + _dot(vt_ref[:, rows[n]], p.astype(BF16))

    n_far = jnp.maximum(qi - 1, 0)

    def far_pair(j, carry):
        run([(2 * j, None, False), (2 * j + 1, None, False)])
        return carry

    lax.fori_loop(0, n_far // 2, far_pair, 0)
    prev, diag = (qi - 1, 1, False), (qi, 0, True)

    @pl.when(qi == 0)
    def _():
        run([diag])

    @pl.when((qi >= 1) & (n_far % 2 == 0))
    def _():
        run([prev, diag])

    @pl.when(n_far % 2 == 1)
    def _():
        run([(n_far - 1, None, False), prev, diag])

    halves = (slice(None), slice(0, blk)), (slice(None), slice(blk, 2 * blk))
    return [acc_ref[sl] / l_ref[sl] for sl in halves]


def _diff_attn_kernel(scal_ref, q_ref, k_ref, vt_ref, bias_ref, subln_ref, o_ref,
                      m_ref, l_ref, acc_ref):
    q = q_ref[...]
    lane = lax.broadcasted_iota(jnp.int32, q.shape, 1)
    zero = jnp.zeros_like(q)
    q_both = jnp.concatenate([jnp.where(lane < HEAD_DIM, q, zero),
                              jnp.where(lane < HEAD_DIM, zero, q)], axis=0)
    o1, o2 = _attn_sweep(q_both, k_ref, None, vt_ref, bias_ref, m_ref, l_ref, acc_ref)
    lam = scal_ref[0]
    out_scale = scal_ref[1]
    o = o1 - lam * o2
    ms = jnp.mean(o * o, axis=0, keepdims=True)
    o = o * lax.rsqrt(ms + NORM_EPS) * subln_ref[...] * out_scale
    o_ref[...] = o.T.astype(o_ref.dtype)


def _fox_attn_kernel(q_ref, qaug_ref, k_ref, kaug_ref, vt_ref, o_ref, m_ref, l_ref, acc_ref):
    q = jnp.concatenate([q_ref[...], qaug_ref[...]], axis=1)
    lane = lax.broadcasted_iota(jnp.int32, q.shape, 1)
    zero = jnp.zeros_like(q)
    in_a = (lane < HEAD_DIM) | ((lane >= LANES) & (lane < LANES + 6))
    in_b = ((lane >= HEAD_DIM) & (lane < LANES)) | ((lane >= LANES + 6) & (lane < LANES + 12))
    q_both = jnp.concatenate([jnp.where(in_a, q, zero), jnp.where(in_b, q, zero)], axis=0)
    o1, o2 = _attn_sweep(q_both, k_ref, kaug_ref, vt_ref, None, m_ref, l_ref, acc_ref)
    row = lax.broadcasted_iota(jnp.int32, o1.shape, 0)
    o_ref[...] = jnp.where(row < HEAD_DIM, o1, o2).T.astype(o_ref.dtype)


def _attn_scratch(tq):
    return [pltpu.VMEM((1, 2 * tq), F32), pltpu.VMEM((1, 2 * tq), F32),
            pltpu.VMEM((LANES, 2 * tq), F32)]


def _diff_attn(qk, vt, bias_tiles, scal, subln_col):
    bsz, seq, _ = qk.shape
    tq = min(ATTN_Q, seq)
    return pl.pallas_call(
        _diff_attn_kernel,
        grid=(bsz, N_DIFF_HEADS, seq // tq),
        in_specs=[pl.BlockSpec(memory_space=pltpu.SMEM),
                  pl.BlockSpec((None, tq, LANES), lambda b, s, i: (b, i, s)),
                  pl.BlockSpec((None, seq, LANES), lambda b, s, i: (b, 0, 4 + s)),
                  pl.BlockSpec((None, LANES, seq), lambda b, s, i: (b, s, 0)),
                  pl.BlockSpec((None, 2, tq, 2 * tq), lambda b, s, i: (s, 0, 0, 0)),
                  _const_spec((LANES, 1))],
        out_specs=pl.BlockSpec((None, tq, LANES), lambda b, s, i: (b, i, s)),
        out_shape=jax.ShapeDtypeStruct((bsz, seq, N_DIFF_HEADS * LANES), BF16),
        scratch_shapes=_attn_scratch(tq),
        compiler_params=_params(3),
        name="diff_attn",
    )(scal, qk, qk, vt, bias_tiles, subln_col)


def _fox_attn(qk, aug, vt):
    bsz, seq, _ = qk.shape
    tq = min(ATTN_Q, seq)
    return pl.pallas_call(
        _fox_attn_kernel,
        grid=(bsz, N_FOX_HEADS // 2, seq // tq),
        in_specs=[pl.BlockSpec((None, tq, LANES), lambda b, s, i: (b, i, 8 + s)),
                  pl.BlockSpec((None, tq, LANES), lambda b, s, i: (b, i, s)),
                  pl.BlockSpec((None, seq, LANES), lambda b, s, i: (b, 0, 12 + s)),
                  pl.BlockSpec((None, seq, LANES), lambda b, s, i: (b, 0, 4 + s)),
                  pl.BlockSpec((None, LANES, seq), lambda b, s, i: (b, 4 + s, 0))],
        out_specs=pl.BlockSpec((None, tq, LANES), lambda b, s, i: (b, i, s)),
        out_shape=jax.ShapeDtypeStruct((bsz, seq, N_FOX_HEADS * HEAD_DIM), BF16),
        scratch_shapes=_attn_scratch(tq),
        compiler_params=_params(3),
        name="fox_attn",
    )(qk, aug, qk, aug, vt)


def _outproj_kernel(x_ref, od_ref, of_ref, wd_ref, wf_ref, o_ref):
    o_ref[...] = x_ref[...] + _dot(od_ref[...], wd_ref[...]) + _dot(of_ref[...], wf_ref[...])


def _outproj(x, od, of, w_d, w_f):
    t = x.shape[0]
    tm = min(TOKEN_TILE, t)
    half = od.shape[1]
    return pl.pallas_call(
        _outproj_kernel,
        grid=(t // tm,),
        in_specs=[pl.BlockSpec((tm, D_MODEL), lambda i: (i, 0)),
                  pl.BlockSpec((tm, half), lambda i: (i, 0)),
                  pl.BlockSpec((tm, half), lambda i: (i, 0)),
                  _const_spec((half, D_MODEL)), _const_spec((half, D_MODEL))],
        out_specs=pl.BlockSpec((tm, D_MODEL), lambda i: (i, 0)),
        out_shape=jax.ShapeDtypeStruct((t, D_MODEL), F32),
        compiler_params=_params(1),
        name="attn_outproj",
    )(x, od, of, w_d, w_f)


def _rel_bias_by_distance(rel_bias, n_dist):
    n = jnp.arange(n_dist, dtype=jnp.int32)
    max_exact = N_REL_BUCKETS // 2
    nf = jnp.maximum(n, 1).astype(F32)
    large = max_exact + (jnp.log(nf / max_exact) / math.log(REL_MAX_DIST / max_exact)
                         * (N_REL_BUCKETS - max_exact)).astype(jnp.int32)
    large = jnp.minimum(large, N_REL_BUCKETS - 1)
    return rel_bias[jnp.where(n < max_exact, n, large)]


def _toeplitz(w, n):
    heads, period = w.shape
    flat = jnp.tile(w, (1, n))[:, :n * (period - 1)]
    return flat.reshape(heads, n, period - 1)[:, :, :n]


def _diff_bias_tiles(rel_bias, tq):
    by_dist = _rel_bias_by_distance(rel_bias.astype(F32), 2 * tq)
    by_dist = ((by_dist - by_dist[2 * tq - 1]) * LOG2E).T
    diag = _toeplitz(by_dist, tq)
    prev = _toeplitz(jnp.roll(by_dist, -tq, axis=1), tq)
    tiles = jnp.stack([diag, prev], axis=1)
    return jnp.concatenate([tiles, tiles], axis=-1)


def _attn_layer(x, h, bsz, seq, layer_idx, w_in, w_out, fg_bias, dq_g, dk_g, lq1, lk1, lq2, lk2,
                subln_g, fq_g, fk_g, bias_tiles):
    tq = min(ATTN_Q, seq)
    q_scale = HEAD_DIM ** -0.5 * LOG2E
    w_bf = w_in.astype(BF16)
    w_qk = jnp.concatenate([w_bf[:, 0:1024], w_bf[:, 1536:2560]], axis=1)
    w_v = jnp.concatenate([w_bf[:, 1024:1536], w_bf[:, 2560:3072]], axis=1).T
    w_f = jnp.pad(w_bf[:, 3072:], ((0, 0), (0, LANES - N_FOX_HEADS)))
    fgb = jnp.pad(fg_bias.astype(F32), (0, LANES - N_FOX_HEADS)).reshape(1, LANES)
    gains = jnp.stack([jnp.tile(dq_g, 8) * q_scale, jnp.tile(dk_g, 8),
                       jnp.tile(fq_g, 8) * q_scale, jnp.tile(fk_g, 8)]).astype(F32)
    qk, vt, aug = _inproj(h, w_qk, w_v, w_f, gains, fgb, bsz, seq)
    qk = qk.reshape(bsz, seq, QK_COLS)

    lam_init = 0.8 - 0.6 * math.exp(-0.3 * layer_idx)
    lam = (jnp.exp(jnp.sum(lq1.astype(F32) * lk1.astype(F32)))
           - jnp.exp(jnp.sum(lq2.astype(F32) * lk2.astype(F32))) + lam_init)
    scal = jnp.stack([lam, jnp.asarray(1.0 - lam_init, F32)]).astype(F32)
    od = _diff_attn(qk, vt, bias_tiles, scal, subln_g.astype(F32).reshape(LANES, 1))

    of = _fox_attn(qk, aug.reshape(bsz, seq, AUG_COLS), vt)

    w_o = w_out.astype(BF16)
    half = N_DIFF_HEADS * LANES
    return _outproj(x, od.reshape(bsz * seq, half), of.reshape(bsz * seq, half),
                    w_o[:half], w_o[half:])


def _s5_kernel(n_chunks, *refs):
    x_refs = refs[:S5_CHUNK]
    t_ref, p_ref, q_ref, a_ref, y_ref, z_ref, xp_ref = refs[S5_CHUNK:]
    x = jnp.concatenate([r[...] for r in x_refs], axis=1)
    n_seq = x.shape[0] // n_chunks
    half = z_ref.shape[1] // 2
    z_ref[...] = _dot(x, p_ref[...])
    a_re = a_ref[0:1, :]
    a_im = a_ref[1:2, :]

    def scan(c, carry):
        out = []
        for b in range(n_seq):
            x_re, x_im = carry[2 * b], carry[2 * b + 1]
            row = pl.ds(b * n_chunks + c, 1)
            xp_ref[row, :half] = x_re
            xp_ref[row, half:] = x_im
            z = z_ref[row, :]
            out += [a_re * x_re - a_im * x_im + z[:, :half],
                    a_re * x_im + a_im * x_re + z[:, half:]]
        return tuple(out)

    zero = jnp.zeros((1, half), F32)
    lax.fori_loop(0, n_chunks, scan, (zero,) * (2 * n_seq))
    y = _dot(x, t_ref[...]) + _dot(xp_ref[...].astype(BF16), q_ref[...])
    for t in range(S5_CHUNK):
        y_ref[t] = y[:, t * LANES:(t + 1) * LANES]


def _s5_operators(a_re, a_im, log_step, b_re, b_im, c_re, c_im):
    hp = lax.Precision.HIGHEST
    L, kb, gb = S5_CHUNK, S5_GROUPS // 8, 8
    step = jnp.exp(log_step.astype(F32))[:, None]
    ar, ai = a_re.astype(F32), a_im.astype(F32)
    mag = jnp.exp(ar * step)
    lr, li = mag * jnp.cos(ai * step), mag * jnp.sin(ai * step)
    den = ar * ar + ai * ai
    cr = (((lr - 1.0) * ar + li * ai) / den)[:, None, :]
    ci = ((li * ar - (lr - 1.0) * ai) / den)[:, None, :]
    br, bi = b_re.astype(F32).transpose(0, 2, 1), b_im.astype(F32).transpose(0, 2, 1)
    bbr, bbi = cr * br - ci * bi, cr * bi + ci * br
    pr, pi = [jnp.ones_like(lr)], [jnp.zeros_like(li)]
    for _ in range(L):
        pr, pi = pr + [pr[-1] * lr - pi[-1] * li], pi + [pr[-1] * li + pi[-1] * lr]
    pr, pi = jnp.stack(pr), jnp.stack(pi)
    wr = pr[:L, :, None, :] * bbr - pi[:L, :, None, :] * bbi
    wi = pr[:L, :, None, :] * bbi + pi[:L, :, None, :] * bbr
    cre, cim = c_re.astype(F32), c_im.astype(F32)
    kern = (jnp.einsum('gcp,ngap->ngac', cre, wr, precision=hp)
            - jnp.einsum('gcp,ngap->ngac', cim, wi, precision=hp))

    def group_mask(rows, rows_per_group, cols, cols_per_group):
        r = lax.broadcasted_iota(jnp.int32, (rows, 1), 0) // rows_per_group
        c = lax.broadcasted_iota(jnp.int32, (1, cols), 1) // cols_per_group
        return r == c

    def spread_channels(m):
        src = lax.broadcasted_iota(jnp.int32, (S5_GROUP, LANES), 0)
        dst = lax.broadcasted_iota(jnp.int32, (S5_GROUP, LANES), 1)
        return jnp.dot(m.astype(BF16), (src == dst % S5_GROUP).astype(BF16),
                       preferred_element_type=BF16)

    zero = jnp.zeros((), BF16)
    k_blk = kern.reshape(L, kb, LANES, S5_GROUP).transpose(1, 0, 2, 3)
    k_blk = jnp.where(group_mask(LANES, S5_GROUP, LANES, S5_GROUP), spread_channels(k_blk), zero)
    nil = jnp.zeros_like(k_blk[:, 0])
    t_big = jnp.concatenate(
        [jnp.concatenate([k_blk[:, t - s] if t >= s else nil for t in range(L)], axis=2)
         for s in range(L)], axis=1)

    def state_in(w):
        w = jnp.tile(w[::-1].astype(BF16).reshape(L, kb, LANES, S5_STATE), (1, 1, 1, gb))
        w = jnp.where(group_mask(LANES, S5_GROUP, gb * S5_STATE, S5_STATE), w, zero)
        return w.transpose(1, 0, 2, 3).reshape(kb, L * LANES, gb * S5_STATE)
    p_big = jnp.concatenate([state_in(wr), state_in(wi)], axis=2)

    cre_t, cim_t = cre.transpose(0, 2, 1), cim.transpose(0, 2, 1)
    def state_out(m):
        m = spread_channels(m.reshape(L, kb, gb * S5_STATE, S5_GROUP))
        m = jnp.where(group_mask(gb * S5_STATE, S5_STATE, LANES, S5_GROUP), m, zero)
        return m.transpose(1, 2, 0, 3).reshape(kb, gb * S5_STATE, L * LANES)
    q_re = cre_t[None] * pr[1:, :, :, None] - cim_t[None] * pi[1:, :, :, None]
    q_im = -(cre_t[None] * pi[1:, :, :, None] + cim_t[None] * pr[1:, :, :, None])
    q_big = jnp.concatenate([state_out(q_re), state_out(q_im)], axis=1)
    a_big = jnp.stack([pr[L].reshape(kb, gb * S5_STATE), pi[L].reshape(kb, gb * S5_STATE)], axis=1)
    return t_big, p_big, q_big, a_big


def _s5_core(h2, bsz, seq, a_re, a_im, log_step, b_re, b_im, c_re, c_im):
    L = S5_CHUNK
    n_chunks = seq // L
    rows = bsz * n_chunks
    kb = D_MODEL // LANES
    rb = min(S5_ROWS, rows)
    assert rb % n_chunks == 0
    t_big, p_big, q_big, a_big = _s5_operators(a_re, a_im, log_step, b_re, b_im, c_re, c_im)
    n_state = 8 * S5_STATE
    x_specs = [pl.BlockSpec((rb, LANES), lambda k, r, t=t: (r, t * kb + k)) for t in range(L)]
    return pl.pallas_call(
        functools.partial(_s5_kernel, n_chunks),
        grid=(kb, rows // rb),
        in_specs=x_specs + [
            pl.BlockSpec((None, L * LANES, L * LANES), lambda k, r: (k, 0, 0)),
            pl.BlockSpec((None, L * LANES, 2 * n_state), lambda k, r: (k, 0, 0)),
            pl.BlockSpec((None, 2 * n_state, L * LANES), lambda k, r: (k, 0, 0)),
            pl.BlockSpec((None, 2, n_state), lambda k, r: (k, 0, 0))],
        out_specs=pl.BlockSpec((L, rb, LANES), lambda k, r: (0, r, k)),
        out_shape=jax.ShapeDtypeStruct((L, rows, D_MODEL), F32),
        scratch_shapes=[pltpu.VMEM((rb, 2 * n_state), F32), pltpu.VMEM((rb, 2 * n_state), F32)],
        compiler_params=_params(2),
        name="s5_core",
    )(*([h2] * L), t_big, p_big, q_big, a_big)


def _glu_kernel(x_ref, y_ref, g_ref, d_ref, wa_ref, wb_ref, o_ref, stage_ref):
    n_rows = y_ref.shape[1]
    n_blk = D_MODEL // LANES
    for k in range(n_blk):
        stage_ref[k] = x_ref[:, k * LANES:(k + 1) * LANES]
    x = jnp.concatenate(
        [jnp.concatenate([stage_ref[k, pl.ds(t, n_rows, stride=S5_CHUNK), :]
                          for k in range(n_blk)], axis=1) for t in range(S5_CHUNK)], axis=0)
    y = y_ref[...].reshape(x.shape) + d_ref[...] * _rms(x, g_ref[...])
    y = jax.nn.gelu(y).astype(BF16)
    res = x + _dot(y, wa_ref[...]) * jax.nn.sigmoid(_dot(y, wb_ref[...]))
    for t in range(S5_CHUNK):
        for k in range(n_blk):
            stage_ref[k, pl.ds(t, n_rows, stride=S5_CHUNK), :] = res[
                t * n_rows:(t + 1) * n_rows, k * LANES:(k + 1) * LANES]
    for k in range(n_blk):
        o_ref[:, k * LANES:(k + 1) * LANES] = stage_ref[k]


def _s5_layer(x, h2, bsz, seq, mix_g, a_re, a_im, log_step, b_re, b_im, c_re, c_im, d_skip,
              w_a, w_b):
    t = x.shape[0]
    tm = min(TOKEN_TILE, t)
    y = _s5_core(h2, bsz, seq, a_re, a_im, log_step, b_re, b_im, c_re, c_im)
    row = pl.BlockSpec((tm, D_MODEL), lambda i: (i, 0))
    return pl.pallas_call(
        _glu_kernel,
        grid=(t // tm,),
        in_specs=[row, pl.BlockSpec((S5_CHUNK, tm // S5_CHUNK, D_MODEL), lambda i: (0, i, 0)),
                  _const_spec((1, D_MODEL)), _const_spec((1, D_MODEL)),
                  _const_spec((D_MODEL, D_MODEL)), _const_spec((D_MODEL, D_MODEL))],
        out_specs=row,
        out_shape=jax.ShapeDtypeStruct((t, D_MODEL), F32),
        scratch_shapes=[pltpu.VMEM((D_MODEL // LANES, tm, LANES), F32)],
        compiler_params=_params(1),
        name="s5_glu",
    )(x, y, mix_g.astype(F32).reshape(1, D_MODEL), d_skip.astype(F32).reshape(1, D_MODEL),
      w_a.astype(BF16), w_b.astype(BF16))


def kernel(x, ffn1_norm, ffn1_gate, ffn1_up, ffn1_down, mix_norm, ffn2_norm, ffn2_gate, ffn2_up, ffn2_down, attn_w_in, attn_w_out, fg_bias, diff_q_norm, diff_k_norm, diff_lambda_q1, diff_lambda_k1, diff_lambda_q2, diff_lambda_k2, diff_subln, fox_q_norm, fox_k_norm, rel_bias, s5_a_re, s5_a_im, s5_log_step, s5_b_re, s5_b_im, s5_c_re, s5_c_im, s5_d, s5_glu_a, s5_glu_b):
    bsz, seq, _ = x.shape
    depth = ffn1_norm.shape[0]
    assert min(ATTN_Q, seq) >= REL_MAX_DIST, "bias tiles assume the bias saturates within a block"
    bias_tiles = _diff_bias_tiles(rel_bias, min(ATTN_Q, seq))
    xt = x.reshape(bsz * seq, D_MODEL).astype(F32)
    for i in range(depth):
        xt, h = _ffn(xt, ffn1_norm[i], ffn1_gate[i].astype(BF16), ffn1_up[i].astype(BF16),
                     ffn1_down[i].astype(BF16), g_next=mix_norm[i], chunked=i % 2 == 1)
        j = i // 2
        if i % 2 == 0:
            xt = _attn_layer(xt, h, bsz, seq, i, attn_w_in[j], attn_w_out[j], fg_bias[j],
                             diff_q_norm[j], diff_k_norm[j], diff_lambda_q1[j], diff_lambda_k1[j],
                             diff_lambda_q2[j], diff_lambda_k2[j], diff_subln[j], fox_q_norm[j],
                             fox_k_norm[j], bias_tiles)
        else:
            xt = _s5_layer(xt, h, bsz, seq, mix_norm[i], s5_a_re[j], s5_a_im[j], s5_log_step[j],
                           s5_b_re[j], s5_b_im[j], s5_c_re[j], s5_c_im[j], s5_d[j],
                           s5_glu_a[j], s5_glu_b[j])
        xt, _ = _ffn(xt, ffn2_norm[i], ffn2_gate[i].astype(BF16), ffn2_up[i].astype(BF16),
                     ffn2_down[i].astype(BF16))
    return xt.reshape(bsz, seq, D_MODEL).astype(x.dtype)
```

```python
import functools
import math

import jax
import jax.numpy as jnp
import numpy as np
from jax import lax
from jax.experimental import pallas as pl
from jax.experimental.pallas import tpu as pltpu

D_MODEL = 1024
D_FF = 2816
HEAD_DIM = 64
N_DIFF_HEADS = 4
N_FOX_HEADS = 8
N_REL_BUCKETS = 32
REL_MAX_DIST = 128
S5_GROUP = 16
S5_GROUPS = D_MODEL // S5_GROUP
S5_STATE = 64
NORM_EPS = 1e-6

QK_COLS = 2048
V_COLS = 1024
AUG_COLS = 1024
LANES = 128
MXU_DIM = 256
VMEM_LIMIT = 56 * 1024 * 1024

TOKEN_TILE = 512
FFN_TILE = 1024
FFN_SPLITS = (0, 6 * MXU_DIM, D_FF)
ATTN_Q = 512
ATTN_LANES = 256
S5_CHUNK = 16
S5_ROWS = 512
NEG_BIG = -1e30
LOG2E = 1.4426950408889634

BF16 = jnp.bfloat16
F32 = jnp.float32


def _dot(a, b):
    return jnp.dot(a, b, preferred_element_type=F32)


def _dot_nt(a, b):
    return lax.dot_general(a, b, (((1,), (1,)), ((), ())), preferred_element_type=F32)


def _rms(x, g):
    ms = jnp.mean(x * x, axis=-1, keepdims=True)
    return x * lax.rsqrt(ms + NORM_EPS) * g


def _const_spec(shape):
    return pl.BlockSpec(shape, lambda *_: (0,) * len(shape), pipeline_mode=pl.Buffered(1))


def _params(n_axes):
    return pltpu.CompilerParams(dimension_semantics=("arbitrary",) * n_axes,
                                vmem_limit_bytes=VMEM_LIMIT)


def _split3(x):
    p0 = x.astype(BF16)
    r1 = x - p0.astype(F32)
    p1 = r1.astype(BF16)
    p2 = (r1 - p1.astype(F32)).astype(BF16)
    return p0, p1, p2


def _ffn_kernel(chunked, x_ref, g_ref, gn_ref, wg_ref, wu_ref, wd_ref, o_ref, *rest):
    x = x_ref[...]
    h = _rms(x, g_ref[...]).astype(BF16)
    acc = None
    for lo, hi in zip(FFN_SPLITS[:-1], FFN_SPLITS[1:]):
        sl = slice(lo, hi)
        a = _dot(h, wg_ref[:, sl])
        b = _dot(h, wu_ref[:, sl])
        act = (a * jax.nn.sigmoid(a) * b).astype(BF16)
        d = _dot(act, wd_ref[sl, :])
        acc = d if acc is None else acc + d
    xo = x + 0.5 * acc
    o_ref[...] = xo
    if not rest:
        return
    hn = _rms(xo, gn_ref[...])
    if not chunked:
        rest[0][...] = hn.astype(BF16)
        return
    h_ref, stage_ref = rest
    n_rows = h_ref.shape[0]
    for k in range(D_MODEL // LANES):
        stage_ref[k] = hn[:, k * LANES:(k + 1) * LANES]
    for t in range(S5_CHUNK):
        for k in range(D_MODEL // LANES):
            lo = t * D_MODEL + k * LANES
            h_ref[:, lo:lo + LANES] = stage_ref[k, pl.ds(t, n_rows, stride=S5_CHUNK), :].astype(BF16)


def _ffn(x, g, wg, wu, wd, g_next=None, chunked=False):
    t = x.shape[0]
    tm = min(FFN_TILE, t)
    emit = g_next is not None
    gn = g_next if emit else g
    row = pl.BlockSpec((tm, D_MODEL), lambda i: (i, 0))
    out_shape = [jax.ShapeDtypeStruct((t, D_MODEL), F32)]
    out_specs = [row]
    scratch = []
    if emit and chunked:
        out_shape.append(jax.ShapeDtypeStruct((t // S5_CHUNK, S5_CHUNK * D_MODEL), BF16))
        out_specs.append(pl.BlockSpec((tm // S5_CHUNK, S5_CHUNK * D_MODEL), lambda i: (i, 0)))
        scratch = [pltpu.VMEM((D_MODEL // LANES, tm, LANES), F32)]
    elif emit:
        out_shape.append(jax.ShapeDtypeStruct((t, D_MODEL), BF16))
        out_specs.append(row)
    res = pl.pallas_call(
        functools.partial(_ffn_kernel, chunked),
        grid=(t // tm,),
        in_specs=[row, _const_spec((1, D_MODEL)), _const_spec((1, D_MODEL)),
                  _const_spec((D_MODEL, D_FF)), _const_spec((D_MODEL, D_FF)),
                  _const_spec((D_FF, D_MODEL))],
        out_specs=out_specs,
        out_shape=out_shape,
        scratch_shapes=scratch,
        compiler_params=_params(1),
        name="ffn",
    )(x, g.reshape(1, D_MODEL), gn.reshape(1, D_MODEL), wg, wu, wd)
    return (res[0], res[1]) if emit else (res[0], None)


def _inproj_kernel(tiles_per_seq, h_ref, wqk_ref, wv_ref, wf_ref, gains_ref, ones_ref, tri_ref,
                   fgb_ref, sel_ref, aug1_ref, qk_ref, vt_ref, aug_ref, carry_ref):
    i = pl.program_id(0)
    h = h_ref[...]
    ones_blk = ones_ref[...]
    for c in range(QK_COLS // 512):
        y = _dot(h, wqk_ref[:, c * 512:(c + 1) * 512])
        y2 = y * y
        hi = y2.astype(BF16)
        lo = (y2 - hi.astype(F32)).astype(BF16)
        ssq = jnp.concatenate(
            [_dot(hi[:, k * MXU_DIM:(k + 1) * MXU_DIM], ones_blk)
             + _dot(lo[:, k * MXU_DIM:(k + 1) * MXU_DIM], ones_blk)
             for k in range(512 // MXU_DIM)], axis=1)
        y = y * lax.rsqrt(ssq * (1.0 / HEAD_DIM) + NORM_EPS) * gains_ref[c:c + 1, :]
        qk_ref[:, c * 512:(c + 1) * 512] = y.astype(BF16)
    for c in range(V_COLS // 512):
        vt = _dot_nt(wv_ref[c * 512:(c + 1) * 512, :], h)
        vt_ref[c * 512:(c + 1) * 512, :] = vt.astype(BF16)

    fl = _dot(h, wf_ref[...]) + fgb_ref[...]
    logf = -(jnp.maximum(-fl, 0.0) + jnp.log1p(jnp.exp(-jnp.abs(fl))))

    @pl.when(i % tiles_per_seq == 0)
    def _():
        carry_ref[...] = jnp.zeros_like(carry_ref)

    tri = tri_ref[...]
    cum = sum(_dot(tri, p) for p in _split3(logf)) + carry_ref[...]
    carry_ref[...] = cum[cum.shape[0] - 1:, :]
    aug = sum(_dot(p, sel_ref[n]) for n, p in enumerate(_split3(cum * LOG2E))) + aug1_ref[...]
    aug_ref[...] = aug.astype(BF16)


def _decay_lane_maps():
    sel = np.zeros((3, LANES, AUG_COLS), np.float32)
    ones = np.zeros((1, AUG_COLS), np.float32)
    k_off = AUG_COLS // 2
    for head in range(N_FOX_HEADS):
        base = (head // 2) * LANES + (head % 2) * 6
        for n in range(3):
            sel[n, head, base + n] = 1.0
            sel[n, head, k_off + base + 3 + n] = -1.0
            ones[0, base + 3 + n] = 1.0
            ones[0, k_off + base + n] = 1.0
    return jnp.asarray(sel, BF16), jnp.asarray(ones, F32)


def _inproj(h, w_qk, w_v, w_f, gains, fg_bias_row, bsz, seq):
    t = h.shape[0]
    tm = min(TOKEN_TILE, seq)
    tps = seq // tm
    ones_blk = jnp.kron(jnp.eye(MXU_DIM // HEAD_DIM, dtype=F32),
                        jnp.ones((HEAD_DIM, HEAD_DIM), F32)).astype(BF16)
    tri = jnp.tril(jnp.ones((tm, tm), F32)).astype(BF16)
    sel, aug_ones = _decay_lane_maps()
    return pl.pallas_call(
        functools.partial(_inproj_kernel, tps),
        grid=(t // tm,),
        in_specs=[pl.BlockSpec((tm, D_MODEL), lambda i: (i, 0)),
                  _const_spec((D_MODEL, QK_COLS)), _const_spec((D_MODEL, V_COLS)),
                  _const_spec((D_MODEL, LANES)),
                  _const_spec((QK_COLS // 512, 512)), _const_spec((MXU_DIM, MXU_DIM)),
                  _const_spec((tm, tm)), _const_spec((1, LANES)),
                  _const_spec((3, LANES, AUG_COLS)), _const_spec((1, AUG_COLS))],
        out_specs=[pl.BlockSpec((tm, QK_COLS), lambda i: (i, 0)),
                   pl.BlockSpec((None, V_COLS, tm), lambda i: (i // tps, 0, i % tps)),
                   pl.BlockSpec((tm, AUG_COLS), lambda i: (i, 0))],
        out_shape=[jax.ShapeDtypeStruct((t, QK_COLS), BF16),
                   jax.ShapeDtypeStruct((bsz, V_COLS, seq), BF16),
                   jax.ShapeDtypeStruct((t, AUG_COLS), BF16)],
        scratch_shapes=[pltpu.VMEM((1, LANES), F32)],
        compiler_params=_params(1),
        name="attn_inproj",
    )(h, w_qk, w_v, w_f, gains, ones_blk, tri, fg_bias_row, sel, aug_ones)


def _attn_sweep(q_both, k_ref, kaug_ref, vt_ref, bias_ref, m_ref, l_ref, acc_ref):
    qi = pl.program_id(2)
    blk = q_both.shape[0] // 2
    m_ref[...] = jnp.full_like(m_ref, NEG_BIG)
    l_ref[...] = jnp.zeros_like(l_ref)
    acc_ref[...] = jnp.zeros_like(acc_ref)

    def run(items):
        rows = [pl.ds(pl.multiple_of(j * blk, blk), blk) for j, _, _ in items]
        lanes = [slice(c * ATTN_LANES, (c + 1) * ATTN_LANES)
                 for c in range(2 * blk // ATTN_LANES)]
        scores = {}
        for n, r in enumerate(rows):
            k_blk = k_ref[r, :]
            if kaug_ref is not None:
                k_blk = jnp.concatenate([k_blk, kaug_ref[r, :]], axis=1)
            for c, ls in enumerate(lanes):
                scores[n, c] = _dot_nt(k_blk, q_both[ls, :])
        for n, (_, bias_idx, masked) in enumerate(items):
            for c, ls in enumerate(lanes):
                s = scores.pop((n, c))
                if bias_idx is not None and bias_ref is not None:
                    s = bias_ref[bias_idx, :, ls] + s
                if masked:
                    key = lax.broadcasted_iota(jnp.int32, s.shape, 0)
                    qry = (lax.broadcasted_iota(jnp.int32, s.shape, 1) + ls.start) & (blk - 1)
                    s = jnp.where(key <= qry, s, NEG_BIG)
                m_prev = m_ref[:, ls]
                m_new = jnp.maximum(m_prev, jnp.max(s, axis=0, keepdims=True))
                alpha = jnp.exp2(m_prev - m_new)
                p = jnp.exp2(s - m_new)
                l_ref[:, ls] = alpha * l_ref[:, ls] + jnp.sum(p, axis=0, keepdims=True)
                m_ref[:, ls] = m_new
                acc_ref[:, ls] = alpha * acc_ref[:, ls] + _dot(vt_ref[:, rows[n]], p.astype(BF16))

    n_far = jnp.maximum(qi - 1, 0)

    def far_pair(j, carry):
        run([(2 * j, None, False), (2 * j + 1, None, False)])
        return carry

    lax.fori_loop(0, n_far // 2, far_pair, 0)
    prev, diag = (qi - 1, 1, False), (qi, 0, True)

    @pl.when(qi == 0)
    def _():
        run([diag])

    @pl.when((qi >= 1) & (n_far % 2 == 0))
    def _():
        run([prev, diag])

    @pl.when(n_far % 2 == 1)
    def _():
        run([(n_far - 1, None, False), prev, diag])

    halves = (slice(None), slice(0, blk)), (slice(None), slice(blk, 2 * blk))
    return [acc_ref[sl] / l_ref[sl] for sl in halves]


def _diff_attn_kernel(scal_ref, q_ref, k_ref, vt_ref, bias_ref, subln_ref, o_ref,
                      m_ref, l_ref, acc_ref):
    q = q_ref[...]
    lane = lax.broadcasted_iota(jnp.int32, q.shape, 1)
    zero = jnp.zeros_like(q)
    q_both = jnp.concatenate([jnp.where(lane < HEAD_DIM, q, zero),
                              jnp.where(lane < HEAD_DIM, zero, q)], axis=0)
    o1, o2 = _attn_sweep(q_both, k_ref, None, vt_ref, bias_ref, m_ref, l_ref, acc_ref)
    lam = scal_ref[0]
    out_scale = scal_ref[1]
    o = o1 - lam * o2
    ms = jnp.mean(o * o, axis=0, keepdims=True)
    o = o * lax.rsqrt(ms + NORM_EPS) * subln_ref[...] * out_scale
    o_ref[...] = o.T.astype(o_ref.dtype)


def _fox_attn_kernel(q_ref, qaug_ref, k_ref, kaug_ref, vt_ref, o_ref, m_ref, l_ref, acc_ref):
    q = jnp.concatenate([q_ref[...], qaug_ref[...]], axis=1)
    lane = lax.broadcasted_iota(jnp.int32, q.shape, 1)
    zero = jnp.zeros_like(q)
    in_a = (lane < HEAD_DIM) | ((lane >= LANES) & (lane < LANES + 6))
    in_b = ((lane >= HEAD_DIM) & (lane < LANES)) | ((lane >= LANES + 6) & (lane < LANES + 12))
    q_both = jnp.concatenate([jnp.where(in_a, q, zero), jnp.where(in_b, q, zero)], axis=0)
    o1, o2 = _attn_sweep(q_both, k_ref, kaug_ref, vt_ref, None, m_ref, l_ref, acc_ref)
    row = lax.broadcasted_iota(jnp.int32, o1.shape, 0)
    o_ref[...] = jnp.where(row < HEAD_DIM, o1, o2).T.astype(o_ref.dtype)


def _attn_scratch(tq):
    return [pltpu.VMEM((1, 2 * tq), F32), pltpu.VMEM((1, 2 * tq), F32),
            pltpu.VMEM((LANES, 2 * tq), F32)]


def _diff_attn(qk, vt, bias_tiles, scal, subln_col):
    bsz, seq, _ = qk.shape
    tq = min(ATTN_Q, seq)
    return pl.pallas_call(
        _diff_attn_kernel,
        grid=(bsz, N_DIFF_HEADS, seq // tq),
        in_specs=[pl.BlockSpec(memory_space=pltpu.SMEM),
                  pl.BlockSpec((None, tq, LANES), lambda b, s, i: (b, i, s)),
                  pl.BlockSpec((None, seq, LANES), lambda b, s, i: (b, 0, 4 + s)),
                  pl.BlockSpec((None, LANES, seq), lambda b, s, i: (b, s, 0)),
                  pl.BlockSpec((None, 2, tq, 2 * tq), lambda b, s, i: (s, 0, 0, 0)),
                  _const_spec((LANES, 1))],
        out_specs=pl.BlockSpec((None, tq, LANES), lambda b, s, i: (b, i, s)),
        out_shape=jax.ShapeDtypeStruct((bsz, seq, N_DIFF_HEADS * LANES), BF16),
        scratch_shapes=_attn_scratch(tq),
        compiler_params=_params(3),
        name="diff_attn",
    )(scal, qk, qk, vt, bias_tiles, subln_col)


def _fox_attn(qk, aug, vt):
    bsz, seq, _ = qk.shape
    tq = min(ATTN_Q, seq)
    return pl.pallas_call(
        _fox_attn_kernel,
        grid=(bsz, N_FOX_HEADS // 2, seq // tq),
        in_specs=[pl.BlockSpec((None, tq, LANES), lambda b, s, i: (b, i, 8 + s)),
                  pl.BlockSpec((None, tq, LANES), lambda b, s, i: (b, i, s)),
                  pl.BlockSpec((None, seq, LANES), lambda b, s, i: (b, 0, 12 + s)),
                  pl.BlockSpec((None, seq, LANES), lambda b, s, i: (b, 0, 4 + s)),
                  pl.BlockSpec((None, LANES, seq), lambda b, s, i: (b, 4 + s, 0))],
        out_specs=pl.BlockSpec((None, tq, LANES), lambda b, s, i: (b, i, s)),
        out_shape=jax.ShapeDtypeStruct((bsz, seq, N_FOX_HEADS * HEAD_DIM), BF16),
        scratch_shapes=_attn_scratch(tq),
        compiler_params=_params(3),
        name="fox_attn",
    )(qk, aug, qk, aug, vt)


def _outproj_kernel(x_ref, od_ref, of_ref, wd_ref, wf_ref, o_ref):
    o_ref[...] = x_ref[...] + _dot(od_ref[...], wd_ref[...]) + _dot(of_ref[...], wf_ref[...])


def _outproj(x, od, of, w_d, w_f):
    t = x.shape[0]
    tm = min(TOKEN_TILE, t)
    half = od.shape[1]
    return pl.pallas_call(
        _outproj_kernel,
        grid=(t // tm,),
        in_specs=[pl.BlockSpec((tm, D_MODEL), lambda i: (i, 0)),
                  pl.BlockSpec((tm, half), lambda i: (i, 0)),
                  pl.BlockSpec((tm, half), lambda i: (i, 0)),
                  _const_spec((half, D_MODEL)), _const_spec((half, D_MODEL))],
        out_specs=pl.BlockSpec((tm, D_MODEL), lambda i: (i, 0)),
        out_shape=jax.ShapeDtypeStruct((t, D_MODEL), F32),
        compiler_params=_params(1),
        name="attn_outproj",
    )(x, od, of, w_d, w_f)


def _rel_bias_by_distance(rel_bias, n_dist):
    n = jnp.arange(n_dist, dtype=jnp.int32)
    max_exact = N_REL_BUCKETS // 2
    nf = jnp.maximum(n, 1).astype(F32)
    large = max_exact + (jnp.log(nf / max_exact) / math.log(REL_MAX_DIST / max_exact)
                         * (N_REL_BUCKETS - max_exact)).astype(jnp.int32)
    large = jnp.minimum(large, N_REL_BUCKETS - 1)
    return rel_bias[jnp.where(n < max_exact, n, large)]


def _toeplitz(w, n):
    heads, period = w.shape
    flat = jnp.tile(w, (1, n))[:, :n * (period - 1)]
    return flat.reshape(heads, n, period - 1)[:, :, :n]


def _diff_bias_tiles(rel_bias, tq):
    by_dist = _rel_bias_by_distance(rel_bias.astype(F32), 2 * tq)
    by_dist = ((by_dist - by_dist[2 * tq - 1]) * LOG2E).T
    diag = _toeplitz(by_dist, tq)
    prev = _toeplitz(jnp.roll(by_dist, -tq, axis=1), tq)
    tiles = jnp.stack([diag, prev], axis=1)
    return jnp.concatenate([tiles, tiles], axis=-1)


def _attn_layer(x, h, bsz, seq, layer_idx, w_in, w_out, fg_bias, dq_g, dk_g, lq1, lk1, lq2, lk2,
                subln_g, fq_g, fk_g, bias_tiles):
    tq = min(ATTN_Q, seq)
    q_scale = HEAD_DIM ** -0.5 * LOG2E
    w_bf = w_in.astype(BF16)
    w_qk = jnp.concatenate([w_bf[:, 0:1024], w_bf[:, 1536:2560]], axis=1)
    w_v = jnp.concatenate([w_bf[:, 1024:1536], w_bf[:, 2560:3072]], axis=1).T
    w_f = jnp.pad(w_bf[:, 3072:], ((0, 0), (0, LANES - N_FOX_HEADS)))
    fgb = jnp.pad(fg_bias.astype(F32), (0, LANES - N_FOX_HEADS)).reshape(1, LANES)
    gains = jnp.stack([jnp.tile(dq_g, 8) * q_scale, jnp.tile(dk_g, 8),
                       jnp.tile(fq_g, 8) * q_scale, jnp.tile(fk_g, 8)]).astype(F32)
    qk, vt, aug = _inproj(h, w_qk, w_v, w_f, gains, fgb, bsz, seq)
    qk = qk.reshape(bsz, seq, QK_COLS)

    lam_init = 0.8 - 0.6 * math.exp(-0.3 * layer_idx)
    lam = (jnp.exp(jnp.sum(lq1.astype(F32) * lk1.astype(F32)))
           - jnp.exp(jnp.sum(lq2.astype(F32) * lk2.astype(F32))) + lam_init)
    scal = jnp.stack([lam, jnp.asarray(1.0 - lam_init, F32)]).astype(F32)
    od = _diff_attn(qk, vt, bias_tiles, scal, subln_g.astype(F32).reshape(LANES, 1))

    of = _fox_attn(qk, aug.reshape(bsz, seq, AUG_COLS), vt)

    w_o = w_out.astype(BF16)
    half = N_DIFF_HEADS * LANES
    return _outproj(x, od.reshape(bsz * seq, half), of.reshape(bsz * seq, half),
                    w_o[:half], w_o[half:])


def _s5_kernel(n_chunks, *refs):
    x_refs = refs[:S5_CHUNK]
    kblk_ref, p_ref, q_ref, a_ref, y_ref, z_ref, xp_ref, t_ref = refs[S5_CHUNK:]

    @pl.when(pl.program_id(1) == 0)
    def _():
        t_ref[...] = jnp.zeros_like(t_ref)
        for s in range(S5_CHUNK):
            for t in range(s, S5_CHUNK):
                t_ref[s * LANES:(s + 1) * LANES, t * LANES:(t + 1) * LANES] = kblk_ref[t - s]

    x = jnp.concatenate([r[...] for r in x_refs], axis=1)
    n_seq = x.shape[0] // n_chunks
    half = z_ref.shape[1] // 2
    z_ref[...] = _dot(x, p_ref[...])
    a_re = a_ref[0:1, :]
    a_im = a_ref[1:2, :]

    def scan(c, carry):
        out = []
        for b in range(n_seq):
            x_re, x_im = carry[2 * b], carry[2 * b + 1]
            row = pl.ds(b * n_chunks + c, 1)
            xp_ref[row, :half] = x_re
            xp_ref[row, half:] = x_im
            z = z_ref[row, :]
            out += [a_re * x_re - a_im * x_im + z[:, :half],
                    a_re * x_im + a_im * x_re + z[:, half:]]
        return tuple(out)

    zero = jnp.zeros((1, half), F32)
    lax.fori_loop(0, n_chunks, scan, (zero,) * (2 * n_seq))
    y = _dot(x, t_ref[...]) + _dot(xp_ref[...].astype(BF16), q_ref[...])
    for t in range(S5_CHUNK):
        y_ref[t] = y[:, t * LANES:(t + 1) * LANES]


def _s5_operators(a_re, a_im, log_step, b_re, b_im, c_re, c_im):
    hp = lax.Precision.HIGHEST
    L, kb, gb = S5_CHUNK, S5_GROUPS // 8, 8
    step = jnp.exp(log_step.astype(F32))[:, None]
    ar, ai = a_re.astype(F32), a_im.astype(F32)
    mag = jnp.exp(ar * step)
    lr, li = mag * jnp.cos(ai * step), mag * jnp.sin(ai * step)
    den = ar * ar + ai * ai
    cr = (((lr - 1.0) * ar + li * ai) / den)[:, None, :]
    ci = ((li * ar - (lr - 1.0) * ai) / den)[:, None, :]
    br, bi = b_re.astype(F32).transpose(0, 2, 1), b_im.astype(F32).transpose(0, 2, 1)
    bbr, bbi = cr * br - ci * bi, cr * bi + ci * br
    pr, pi = [jnp.ones_like(lr)], [jnp.zeros_like(li)]
    for _ in range(L):
        pr, pi = pr + [pr[-1] * lr - pi[-1] * li], pi + [pr[-1] * li + pi[-1] * lr]
    pr, pi = jnp.stack(pr), jnp.stack(pi)
    wr = pr[:L, :, None, :] * bbr - pi[:L, :, None, :] * bbi
    wi = pr[:L, :, None, :] * bbi + pi[:L, :, None, :] * bbr
    cre, cim = c_re.astype(F32), c_im.astype(F32)
    kern = (jnp.einsum('gcp,ngap->ngac', cre, wr, precision=hp)
            - jnp.einsum('gcp,ngap->ngac', cim, wi, precision=hp))

    def group_mask(rows, rows_per_group, cols, cols_per_group):
        r = lax.broadcasted_iota(jnp.int32, (rows, 1), 0) // rows_per_group
        c = lax.broadcasted_iota(jnp.int32, (1, cols), 1) // cols_per_group
        return r == c

    def spread_channels(m):
        src = lax.broadcasted_iota(jnp.int32, (S5_GROUP, LANES), 0)
        dst = lax.broadcasted_iota(jnp.int32, (S5_GROUP, LANES), 1)
        return jnp.dot(m.astype(BF16), (src == dst % S5_GROUP).astype(BF16),
                       preferred_element_type=BF16)

    zero = jnp.zeros((), BF16)
    k_blk = kern.reshape(L, kb, LANES, S5_GROUP).transpose(1, 0, 2, 3)
    k_blk = jnp.where(group_mask(LANES, S5_GROUP, LANES, S5_GROUP), spread_channels(k_blk), zero)

    def state_in(w):
        w = jnp.tile(w[::-1].astype(BF16).reshape(L, kb, LANES, S5_STATE), (1, 1, 1, gb))
        w = jnp.where(group_mask(LANES, S5_GROUP, gb * S5_STATE, S5_STATE), w, zero)
        return w.transpose(1, 0, 2, 3).reshape(kb, L * LANES, gb * S5_STATE)
    p_big = jnp.concatenate([state_in(wr), state_in(wi)], axis=2)

    cre_t, cim_t = cre.transpose(0, 2, 1), cim.transpose(0, 2, 1)
    def state_out(m):
        m = spread_channels(m.reshape(L, kb, gb * S5_STATE, S5_GROUP))
        m = jnp.where(group_mask(gb * S5_STATE, S5_STATE, LANES, S5_GROUP), m, zero)
        return m.transpose(1, 2, 0, 3).reshape(kb, gb * S5_STATE, L * LANES)
    q_re = cre_t[None] * pr[1:, :, :, None] - cim_t[None] * pi[1:, :, :, None]
    q_im = -(cre_t[None] * pi[1:, :, :, None] + cim_t[None] * pr[1:, :, :, None])
    q_big = jnp.concatenate([state_out(q_re), state_out(q_im)], axis=1)
    a_big = jnp.stack([pr[L].reshape(kb, gb * S5_STATE), pi[L].reshape(kb, gb * S5_STATE)], axis=1)
    return k_blk, p_big, q_big, a_big


def _s5_core(h2, bsz, seq, a_re, a_im, log_step, b_re, b_im, c_re, c_im):
    L = S5_CHUNK
    n_chunks = seq // L
    rows = bsz * n_chunks
    kb = D_MODEL // LANES
    rb = min(S5_ROWS, rows)
    assert rb % n_chunks == 0
    k_blk, p_big, q_big, a_big = _s5_operators(a_re, a_im, log_step, b_re, b_im, c_re, c_im)
    n_state = 8 * S5_STATE
    x_specs = [pl.BlockSpec((rb, LANES), lambda k, r, t=t: (r, t * kb + k)) for t in range(L)]
    return pl.pallas_call(
        functools.partial(_s5_kernel, n_chunks),
        grid=(kb, rows // rb),
        in_specs=x_specs + [
            pl.BlockSpec((None, L, LANES, LANES), lambda k, r: (k, 0, 0, 0)),
            pl.BlockSpec((None, L * LANES, 2 * n_state), lambda k, r: (k, 0, 0)),
            pl.BlockSpec((None, 2 * n_state, L * LANES), lambda k, r: (k, 0, 0)),
            pl.BlockSpec((None, 2, n_state), lambda k, r: (k, 0, 0))],
        out_specs=pl.BlockSpec((L, rb, LANES), lambda k, r: (0, r, k)),
        out_shape=jax.ShapeDtypeStruct((L, rows, D_MODEL), F32),
        scratch_shapes=[pltpu.VMEM((rb, 2 * n_state), F32), pltpu.VMEM((rb, 2 * n_state), F32),
                        pltpu.VMEM((L * LANES, L * LANES), BF16)],
        compiler_params=_params(2),
        name="s5_core",
    )(*([h2] * L), k_blk, p_big, q_big, a_big)


def _glu_kernel(x_ref, y_ref, g_ref, d_ref, wa_ref, wb_ref, o_ref, stage_ref):
    n_rows = y_ref.shape[1]
    n_blk = D_MODEL // LANES
    for k in range(n_blk):
        stage_ref[k] = x_ref[:, k * LANES:(k + 1) * LANES]
    x = jnp.concatenate(
        [jnp.concatenate([stage_ref[k, pl.ds(t, n_rows, stride=S5_CHUNK), :]
                          for k in range(n_blk)], axis=1) for t in range(S5_CHUNK)], axis=0)
    y = y_ref[...].reshape(x.shape) + d_ref[...] * _rms(x, g_ref[...])
    y = jax.nn.gelu(y).astype(BF16)
    res = x + _dot(y, wa_ref[...]) * jax.nn.sigmoid(_dot(y, wb_ref[...]))
    for t in range(S5_CHUNK):
        for k in range(n_blk):
            stage_ref[k, pl.ds(t, n_rows, stride=S5_CHUNK), :] = res[
                t * n_rows:(t + 1) * n_rows, k * LANES:(k + 1) * LANES]
    for k in range(n_blk):
        o_ref[:, k * LANES:(k + 1) * LANES] = stage_ref[k]


def _s5_layer(x, h2, bsz, seq, mix_g, a_re, a_im, log_step, b_re, b_im, c_re, c_im, d_skip,
              w_a, w_b):
    t = x.shape[0]
    tm = min(TOKEN_TILE, t)
    y = _s5_core(h2, bsz, seq, a_re, a_im, log_step, b_re, b_im, c_re, c_im)
    row = pl.BlockSpec((tm, D_MODEL), lambda i: (i, 0))
    return pl.pallas_call(
        _glu_kernel,
        grid=(t // tm,),
        in_specs=[row, pl.BlockSpec((S5_CHUNK, tm // S5_CHUNK, D_MODEL), lambda i: (0, i, 0)),
                  _const_spec((1, D_MODEL)), _const_spec((1, D_MODEL)),
                  _const_spec((D_MODEL, D_MODEL)), _const_spec((D_MODEL, D_MODEL))],
        out_specs=row,
        out_shape=jax.ShapeDtypeStruct((t, D_MODEL), F32),
        scratch_shapes=[pltpu.VMEM((D_MODEL // LANES, tm, LANES), F32)],
        compiler_params=_params(1),
        name="s5_glu",
    )(x, y, mix_g.astype(F32).reshape(1, D_MODEL), d_skip.astype(F32).reshape(1, D_MODEL),
      w_a.astype(BF16), w_b.astype(BF16))


def kernel(x, ffn1_norm, ffn1_gate, ffn1_up, ffn1_down, mix_norm, ffn2_norm, ffn2_gate, ffn2_up, ffn2_down, attn_w_in, attn_w_out, fg_bias, diff_q_norm, diff_k_norm, diff_lambda_q1, diff_lambda_k1, diff_lambda_q2, diff_lambda_k2, diff_subln, fox_q_norm, fox_k_norm, rel_bias, s5_a_re, s5_a_im, s5_log_step, s5_b_re, s5_b_im, s5_c_re, s5_c_im, s5_d, s5_glu_a, s5_glu_b):
    bsz, seq, _ = x.shape
    depth = ffn1_norm.shape[0]
    assert min(ATTN_Q, seq) >= REL_MAX_DIST, "bias tiles assume the bias saturates within a block"
    bias_tiles = _diff_bias_tiles(rel_bias, min(ATTN_Q, seq))
    xt = x.reshape(bsz * seq, D_MODEL).astype(F32)
    for i in range(depth):
        xt, h = _ffn(xt, ffn1_norm[i], ffn1_gate[i].astype(BF16), ffn1_up[i].astype(BF16),
                     ffn1_down[i].astype(BF16), g_next=mix_norm[i], chunked=i % 2 == 1)
        j = i // 2
        if i % 2 == 0:
            xt = _attn_layer(xt, h, bsz, seq, i, attn_w_in[j], attn_w_out[j], fg_bias[j],
                             diff_q_norm[j], diff_k_norm[j], diff_lambda_q1[j], diff_lambda_k1[j],
                             diff_lambda_q2[j], diff_lambda_k2[j], diff_subln[j], fox_q_norm[j],
                             fox_k_norm[j], bias_tiles)
        else:
            xt = _s5_layer(xt, h, bsz, seq, mix_norm[i], s5_a_re[j], s5_a_im[j], s5_log_step[j],
                           s5_b_re[j], s5_b_im[j], s5_c_re[j], s5_c_im[j], s5_d[j],
                           s5_glu_a[j], s5_glu_b[j])
        xt, _ = _ffn(xt, ffn2_norm[i], ffn2_gate[i].astype(BF16), ffn2_up[i].astype(BF16),
                     ffn2_down[i].astype(BF16))
    return xt.reshape(bsz, seq, D_MODEL).astype(x.dtype)
```

```python
import functools
import math

import jax
import jax.numpy as jnp
import numpy as np
from jax import lax
from jax.experimental import pallas as pl
from jax.experimental.pallas import tpu as pltpu

D_MODEL = 1024
D_FF = 2816
HEAD_DIM = 64
N_DIFF_HEADS = 4
N_FOX_HEADS = 8
N_REL_BUCKETS = 32
REL_MAX_DIST = 128
S5_GROUP = 16
S5_GROUPS = D_MODEL // S5_GROUP
S5_STATE = 64
NORM_EPS = 1e-6

QK_COLS = 2048
V_COLS = 1024
AUG_COLS = 1024
LANES = 128
MXU_DIM = 256
VMEM_LIMIT = 56 * 1024 * 1024

TOKEN_TILE = 512
FFN_TILE = 1024
FFN_SPLITS = (0, 6 * MXU_DIM, D_FF)
ATTN_Q = 512
ATTN_LANES = 256
ATTN_GROUP = 3
S5_CHUNK = 16
S5_ROWS = 512
NEG_BIG = -1e30
LOG2E = 1.4426950408889634

BF16 = jnp.bfloat16
F32 = jnp.float32


def _dot(a, b):
    return jnp.dot(a, b, preferred_element_type=F32)


def _dot_nt(a, b):
    return lax.dot_general(a, b, (((1,), (1,)), ((), ())), preferred_element_type=F32)


def _rms(x, g):
    ms = jnp.mean(x * x, axis=-1, keepdims=True)
    return x * lax.rsqrt(ms + NORM_EPS) * g


def _const_spec(shape):
    return pl.BlockSpec(shape, lambda *_: (0,) * len(shape), pipeline_mode=pl.Buffered(1))


def _params(n_axes):
    return pltpu.CompilerParams(dimension_semantics=("arbitrary",) * n_axes,
                                vmem_limit_bytes=VMEM_LIMIT)


def _split3(x):
    p0 = x.astype(BF16)
    r1 = x - p0.astype(F32)
    p1 = r1.astype(BF16)
    p2 = (r1 - p1.astype(F32)).astype(BF16)
    return p0, p1, p2


def _ffn_kernel(chunked, x_ref, g_ref, gn_ref, wg_ref, wu_ref, wd_ref, o_ref, *rest):
    x = x_ref[...]
    h = _rms(x, g_ref[...]).astype(BF16)
    acc = None
    for lo, hi in zip(FFN_SPLITS[:-1], FFN_SPLITS[1:]):
        sl = slice(lo, hi)
        a = _dot(h, wg_ref[:, sl])
        b = _dot(h, wu_ref[:, sl])
        act = (a * jax.nn.sigmoid(a) * b).astype(BF16)
        d = _dot(act, wd_ref[sl, :])
        acc = d if acc is None else acc + d
    xo = x + 0.5 * acc
    o_ref[...] = xo
    if not rest:
        return
    hn = _rms(xo, gn_ref[...])
    if not chunked:
        rest[0][...] = hn.astype(BF16)
        return
    h_ref, stage_ref = rest
    n_rows = h_ref.shape[0]
    for k in range(D_MODEL // LANES):
        stage_ref[k] = hn[:, k * LANES:(k + 1) * LANES]
    for t in range(S5_CHUNK):
        for k in range(D_MODEL // LANES):
            lo = t * D_MODEL + k * LANES
            h_ref[:, lo:lo + LANES] = stage_ref[k, pl.ds(t, n_rows, stride=S5_CHUNK), :].astype(BF16)


def _ffn(x, g, wg, wu, wd, g_next=None, chunked=False):
    t = x.shape[0]
    tm = min(FFN_TILE, t)
    emit = g_next is not None
    gn = g_next if emit else g
    row = pl.BlockSpec((tm, D_MODEL), lambda i: (i, 0))
    out_shape = [jax.ShapeDtypeStruct((t, D_MODEL), F32)]
    out_specs = [row]
    scratch = []
    if emit and chunked:
        out_shape.append(jax.ShapeDtypeStruct((t // S5_CHUNK, S5_CHUNK * D_MODEL), BF16))
        out_specs.append(pl.BlockSpec((tm // S5_CHUNK, S5_CHUNK * D_MODEL), lambda i: (i, 0)))
        scratch = [pltpu.VMEM((D_MODEL // LANES, tm, LANES), F32)]
    elif emit:
        out_shape.append(jax.ShapeDtypeStruct((t, D_MODEL), BF16))
        out_specs.append(row)
    res = pl.pallas_call(
        functools.partial(_ffn_kernel, chunked),
        grid=(t // tm,),
        in_specs=[row, _const_spec((1, D_MODEL)), _const_spec((1, D_MODEL)),
                  _const_spec((D_MODEL, D_FF)), _const_spec((D_MODEL, D_FF)),
                  _const_spec((D_FF, D_MODEL))],
        out_specs=out_specs,
        out_shape=out_shape,
        scratch_shapes=scratch,
        compiler_params=_params(1),
        name="ffn",
    )(x, g.reshape(1, D_MODEL), gn.reshape(1, D_MODEL), wg, wu, wd)
    return (res[0], res[1]) if emit else (res[0], None)


def _inproj_kernel(tiles_per_seq, h_ref, wqk_ref, wv_ref, wf_ref, gains_ref, ones_ref, tri_ref,
                   fgb_ref, sel_ref, aug1_ref, qk_ref, vt_ref, aug_ref, carry_ref):
    i = pl.program_id(0)
    h = h_ref[...]
    ones_blk = ones_ref[...]
    for c in range(QK_COLS // 512):
        y = _dot(h, wqk_ref[:, c * 512:(c + 1) * 512])
        y2 = y * y
        hi = y2.astype(BF16)
        lo = (y2 - hi.astype(F32)).astype(BF16)
        ssq = jnp.concatenate(
            [_dot(hi[:, k * MXU_DIM:(k + 1) * MXU_DIM], ones_blk)
             + _dot(lo[:, k * MXU_DIM:(k + 1) * MXU_DIM], ones_blk)
             for k in range(512 // MXU_DIM)], axis=1)
        y = y * lax.rsqrt(ssq * (1.0 / HEAD_DIM) + NORM_EPS) * gains_ref[c:c + 1, :]
        qk_ref[:, c * 512:(c + 1) * 512] = y.astype(BF16)
    for c in range(V_COLS // 512):
        vt = _dot_nt(wv_ref[c * 512:(c + 1) * 512, :], h)
        vt_ref[c * 512:(c + 1) * 512, :] = vt.astype(BF16)

    fl = _dot(h, wf_ref[...]) + fgb_ref[...]
    logf = -(jnp.maximum(-fl, 0.0) + jnp.log1p(jnp.exp(-jnp.abs(fl))))

    @pl.when(i % tiles_per_seq == 0)
    def _():
        carry_ref[...] = jnp.zeros_like(carry_ref)

    tri = tri_ref[...]
    cum = sum(_dot(tri, p) for p in _split3(logf)) + carry_ref[...]
    carry_ref[...] = cum[cum.shape[0] - 1:, :]
    aug = sum(_dot(p, sel_ref[n]) for n, p in enumerate(_split3(cum * LOG2E))) + aug1_ref[...]
    aug_ref[...] = aug.astype(BF16)


def _decay_lane_maps():
    sel = np.zeros((3, LANES, AUG_COLS), np.float32)
    ones = np.zeros((1, AUG_COLS), np.float32)
    k_off = AUG_COLS // 2
    for head in range(N_FOX_HEADS):
        base = (head // 2) * LANES + (head % 2) * 6
        for n in range(3):
            sel[n, head, base + n] = 1.0
            sel[n, head, k_off + base + 3 + n] = -1.0
            ones[0, base + 3 + n] = 1.0
            ones[0, k_off + base + n] = 1.0
    return jnp.asarray(sel, BF16), jnp.asarray(ones, F32)


def _inproj(h, w_qk, w_v, w_f, gains, fg_bias_row, bsz, seq):
    t = h.shape[0]
    tm = min(TOKEN_TILE, seq)
    tps = seq // tm
    ones_blk = jnp.kron(jnp.eye(MXU_DIM // HEAD_DIM, dtype=F32),
                        jnp.ones((HEAD_DIM, HEAD_DIM), F32)).astype(BF16)
    tri = jnp.tril(jnp.ones((tm, tm), F32)).astype(BF16)
    sel, aug_ones = _decay_lane_maps()
    return pl.pallas_call(
        functools.partial(_inproj_kernel, tps),
        grid=(t // tm,),
        in_specs=[pl.BlockSpec((tm, D_MODEL), lambda i: (i, 0)),
                  _const_spec((D_MODEL, QK_COLS)), _const_spec((D_MODEL, V_COLS)),
                  _const_spec((D_MODEL, LANES)),
                  _const_spec((QK_COLS // 512, 512)), _const_spec((MXU_DIM, MXU_DIM)),
                  _const_spec((tm, tm)), _const_spec((1, LANES)),
                  _const_spec((3, LANES, AUG_COLS)), _const_spec((1, AUG_COLS))],
        out_specs=[pl.BlockSpec((tm, QK_COLS), lambda i: (i, 0)),
                   pl.BlockSpec((None, V_COLS, tm), lambda i: (i // tps, 0, i % tps)),
                   pl.BlockSpec((tm, AUG_COLS), lambda i: (i, 0))],
        out_shape=[jax.ShapeDtypeStruct((t, QK_COLS), BF16),
                   jax.ShapeDtypeStruct((bsz, V_COLS, seq), BF16),
                   jax.ShapeDtypeStruct((t, AUG_COLS), BF16)],
        scratch_shapes=[pltpu.VMEM((1, LANES), F32)],
        compiler_params=_params(1),
        name="attn_inproj",
    )(h, w_qk, w_v, w_f, gains, ones_blk, tri, fg_bias_row, sel, aug_ones)


def _attn_sweep(q_both, k_ref, kaug_ref, vt_ref, bias_ref, m_ref, l_ref, acc_ref):
    qi = pl.program_id(2)
    blk = q_both.shape[0] // 2
    m_ref[...] = jnp.full_like(m_ref, NEG_BIG)
    l_ref[...] = jnp.zeros_like(l_ref)
    acc_ref[...] = jnp.zeros_like(acc_ref)

    def run(items):
        rows = [pl.ds(pl.multiple_of(j * blk, blk), blk) for j, _, _ in items]
        lanes = [slice(c * ATTN_LANES, (c + 1) * ATTN_LANES)
                 for c in range(2 * blk // ATTN_LANES)]
        scores = {}
        for n, r in enumerate(rows):
            k_blk = k_ref[r, :]
            if kaug_ref is not None:
                k_blk = jnp.concatenate([k_blk, kaug_ref[r, :]], axis=1)
            for c, ls in enumerate(lanes):
                scores[n, c] = _dot_nt(k_blk, q_both[ls, :])
        for n, (_, bias_idx, masked) in enumerate(items):
            for c, ls in enumerate(lanes):
                s = scores.pop((n, c))
                if bias_idx is not None and bias_ref is not None:
                    s = bias_ref[bias_idx, :, ls] + s
                if masked:
                    key = lax.broadcasted_iota(jnp.int32, s.shape, 0)
                    qry = (lax.broadcasted_iota(jnp.int32, s.shape, 1) + ls.start) & (blk - 1)
                    s = jnp.where(key <= qry, s, NEG_BIG)
                m_prev = m_ref[:, ls]
                m_new = jnp.maximum(m_prev, jnp.max(s, axis=0, keepdims=True))
                alpha = jnp.exp2(m_prev - m_new)
                p = jnp.exp2(s - m_new)
                l_ref[:, ls] = alpha * l_ref[:, ls] + jnp.sum(p, axis=0, keepdims=True)
                m_ref[:, ls] = m_new
                acc_ref[:, ls] = alpha * acc_ref[:, ls] + _dot(vt_ref[:, rows[n]], p.astype(BF16))

    n_far = jnp.maximum(qi - 1, 0)

    def far_group(j, carry):
        run([(ATTN_GROUP * j + n, None, False) for n in range(ATTN_GROUP)])
        return carry

    lax.fori_loop(0, n_far // ATTN_GROUP, far_group, 0)
    prev, diag = (qi - 1, 1, False), (qi, 0, True)

    @pl.when(qi == 0)
    def _():
        run([diag])

    for left in range(ATTN_GROUP):
        @pl.when((qi >= 1) & (n_far % ATTN_GROUP == left))
        def _():
            run([(n_far - left + n, None, False) for n in range(left)] + [prev, diag])

    halves = (slice(None), slice(0, blk)), (slice(None), slice(blk, 2 * blk))
    return [acc_ref[sl] / l_ref[sl] for sl in halves]


def _diff_attn_kernel(scal_ref, q_ref, k_ref, vt_ref, bias_ref, subln_ref, o_ref,
                      m_ref, l_ref, acc_ref):
    q = q_ref[...]
    lane = lax.broadcasted_iota(jnp.int32, q.shape, 1)
    zero = jnp.zeros_like(q)
    q_both = jnp.concatenate([jnp.where(lane < HEAD_DIM, q, zero),
                              jnp.where(lane < HEAD_DIM, zero, q)], axis=0)
    o1, o2 = _attn_sweep(q_both, k_ref, None, vt_ref, bias_ref, m_ref, l_ref, acc_ref)
    lam = scal_ref[0]
    out_scale = scal_ref[1]
    o = o1 - lam * o2
    ms = jnp.mean(o * o, axis=0, keepdims=True)
    o = o * lax.rsqrt(ms + NORM_EPS) * subln_ref[...] * out_scale
    o_ref[...] = o.T.astype(o_ref.dtype)


def _fox_attn_kernel(q_ref, qaug_ref, k_ref, kaug_ref, vt_ref, o_ref, m_ref, l_ref, acc_ref):
    q = jnp.concatenate([q_ref[...], qaug_ref[...]], axis=1)
    lane = lax.broadcasted_iota(jnp.int32, q.shape, 1)
    zero = jnp.zeros_like(q)
    in_a = (lane < HEAD_DIM) | ((lane >= LANES) & (lane < LANES + 6))
    in_b = ((lane >= HEAD_DIM) & (lane < LANES)) | ((lane >= LANES + 6) & (lane < LANES + 12))
    q_both = jnp.concatenate([jnp.where(in_a, q, zero), jnp.where(in_b, q, zero)], axis=0)
    o1, o2 = _attn_sweep(q_both, k_ref, kaug_ref, vt_ref, None, m_ref, l_ref, acc_ref)
    row = lax.broadcasted_iota(jnp.int32, o1.shape, 0)
    o_ref[...] = jnp.where(row < HEAD_DIM, o1, o2).T.astype(o_ref.dtype)


def _attn_scratch(tq):
    return [pltpu.VMEM((1, 2 * tq), F32), pltpu.VMEM((1, 2 * tq), F32),
            pltpu.VMEM((LANES, 2 * tq), F32)]


def _diff_attn(qk, vt, bias_tiles, scal, subln_col):
    bsz, seq, _ = qk.shape
    tq = min(ATTN_Q, seq)
    return pl.pallas_call(
        _diff_attn_kernel,
        grid=(bsz, N_DIFF_HEADS, seq // tq),
        in_specs=[pl.BlockSpec(memory_space=pltpu.SMEM),
                  pl.BlockSpec((None, tq, LANES), lambda b, s, i: (b, i, s)),
                  pl.BlockSpec((None, seq, LANES), lambda b, s, i: (b, 0, 4 + s)),
                  pl.BlockSpec((None, LANES, seq), lambda b, s, i: (b, s, 0)),
                  pl.BlockSpec((None, 2, tq, 2 * tq), lambda b, s, i: (s, 0, 0, 0)),
                  _const_spec((LANES, 1))],
        out_specs=pl.BlockSpec((None, tq, LANES), lambda b, s, i: (b, i, s)),
        out_shape=jax.ShapeDtypeStruct((bsz, seq, N_DIFF_HEADS * LANES), BF16),
        scratch_shapes=_attn_scratch(tq),
        compiler_params=_params(3),
        name="diff_attn",
    )(scal, qk, qk, vt, bias_tiles, subln_col)


def _fox_attn(qk, aug, vt):
    bsz, seq, _ = qk.shape
    tq = min(ATTN_Q, seq)
    return pl.pallas_call(
        _fox_attn_kernel,
        grid=(bsz, N_FOX_HEADS // 2, seq // tq),
        in_specs=[pl.BlockSpec((None, tq, LANES), lambda b, s, i: (b, i, 8 + s)),
                  pl.BlockSpec((None, tq, LANES), lambda b, s, i: (b, i, s)),
                  pl.BlockSpec((None, seq, LANES), lambda b, s, i: (b, 0, 12 + s)),
                  pl.BlockSpec((None, seq, LANES), lambda b, s, i: (b, 0, 4 + s)),
                  pl.BlockSpec((None, LANES, seq), lambda b, s, i: (b, 4 + s, 0))],
        out_specs=pl.BlockSpec((None, tq, LANES), lambda b, s, i: (b, i, s)),
        out_shape=jax.ShapeDtypeStruct((bsz, seq, N_FOX_HEADS * HEAD_DIM), BF16),
        scratch_shapes=_attn_scratch(tq),
        compiler_params=_params(3),
        name="fox_attn",
    )(qk, aug, qk, aug, vt)


def _outproj_kernel(x_ref, od_ref, of_ref, wd_ref, wf_ref, o_ref):
    o_ref[...] = x_ref[...] + _dot(od_ref[...], wd_ref[...]) + _dot(of_ref[...], wf_ref[...])


def _outproj(x, od, of, w_d, w_f):
    t = x.shape[0]
    tm = min(TOKEN_TILE, t)
    half = od.shape[1]
    return pl.pallas_call(
        _outproj_kernel,
        grid=(t // tm,),
        in_specs=[pl.BlockSpec((tm, D_MODEL), lambda i: (i, 0)),
                  pl.BlockSpec((tm, half), lambda i: (i, 0)),
                  pl.BlockSpec((tm, half), lambda i: (i, 0)),
                  _const_spec((half, D_MODEL)), _const_spec((half, D_MODEL))],
        out_specs=pl.BlockSpec((tm, D_MODEL), lambda i: (i, 0)),
        out_shape=jax.ShapeDtypeStruct((t, D_MODEL), F32),
        compiler_params=_params(1),
        name="attn_outproj",
    )(x, od, of, w_d, w_f)


def _rel_bias_by_distance(rel_bias, n_dist):
    n = jnp.arange(n_dist, dtype=jnp.int32)
    max_exact = N_REL_BUCKETS // 2
    nf = jnp.maximum(n, 1).astype(F32)
    large = max_exact + (jnp.log(nf / max_exact) / math.log(REL_MAX_DIST / max_exact)
                         * (N_REL_BUCKETS - max_exact)).astype(jnp.int32)
    large = jnp.minimum(large, N_REL_BUCKETS - 1)
    return rel_bias[jnp.where(n < max_exact, n, large)]


def _toeplitz(w, n):
    heads, period = w.shape
    flat = jnp.tile(w, (1, n))[:, :n * (period - 1)]
    return flat.reshape(heads, n, period - 1)[:, :, :n]


def _diff_bias_tiles(rel_bias, tq):
    by_dist = _rel_bias_by_distance(rel_bias.astype(F32), 2 * tq)
    by_dist = ((by_dist - by_dist[2 * tq - 1]) * LOG2E).T
    diag = _toeplitz(by_dist, tq)
    prev = _toeplitz(jnp.roll(by_dist, -tq, axis=1), tq)
    tiles = jnp.stack([diag, prev], axis=1)
    return jnp.concatenate([tiles, tiles], axis=-1)


def _attn_layer(x, h, bsz, seq, layer_idx, w_in, w_out, fg_bias, dq_g, dk_g, lq1, lk1, lq2, lk2,
                subln_g, fq_g, fk_g, bias_tiles):
    tq = min(ATTN_Q, seq)
    q_scale = HEAD_DIM ** -0.5 * LOG2E
    w_bf = w_in.astype(BF16)
    w_qk = jnp.concatenate([w_bf[:, 0:1024], w_bf[:, 1536:2560]], axis=1)
    w_v = jnp.concatenate([w_bf[:, 1024:1536], w_bf[:, 2560:3072]], axis=1).T
    w_f = jnp.pad(w_bf[:, 3072:], ((0, 0), (0, LANES - N_FOX_HEADS)))
    fgb = jnp.pad(fg_bias.astype(F32), (0, LANES - N_FOX_HEADS)).reshape(1, LANES)
    gains = jnp.stack([jnp.tile(dq_g, 8) * q_scale, jnp.tile(dk_g, 8),
                       jnp.tile(fq_g, 8) * q_scale, jnp.tile(fk_g, 8)]).astype(F32)
    qk, vt, aug = _inproj(h, w_qk, w_v, w_f, gains, fgb, bsz, seq)
    qk = qk.reshape(bsz, seq, QK_COLS)

    lam_init = 0.8 - 0.6 * math.exp(-0.3 * layer_idx)
    lam = (jnp.exp(jnp.sum(lq1.astype(F32) * lk1.astype(F32)))
           - jnp.exp(jnp.sum(lq2.astype(F32) * lk2.astype(F32))) + lam_init)
    scal = jnp.stack([lam, jnp.asarray(1.0 - lam_init, F32)]).astype(F32)
    od = _diff_attn(qk, vt, bias_tiles, scal, subln_g.astype(F32).reshape(LANES, 1))

    of = _fox_attn(qk, aug.reshape(bsz, seq, AUG_COLS), vt)

    w_o = w_out.astype(BF16)
    half = N_DIFF_HEADS * LANES
    return _outproj(x, od.reshape(bsz * seq, half), of.reshape(bsz * seq, half),
                    w_o[:half], w_o[half:])


def _s5_kernel(n_chunks, *refs):
    x_refs = refs[:S5_CHUNK]
    kblk_ref, p_ref, q_ref, a_ref, y_ref, z_ref, xp_ref, t_ref = refs[S5_CHUNK:]

    @pl.when(pl.program_id(1) == 0)
    def _():
        t_ref[...] = jnp.zeros_like(t_ref)
        for s in range(S5_CHUNK):
            for t in range(s, S5_CHUNK):
                t_ref[s * LANES:(s + 1) * LANES, t * LANES:(t + 1) * LANES] = kblk_ref[t - s]

    x = jnp.concatenate([r[...] for r in x_refs], axis=1)
    n_seq = x.shape[0] // n_chunks
    half = z_ref.shape[1] // 2
    z_ref[...] = _dot(x, p_ref[...])
    a_re = a_ref[0:1, :]
    a_im = a_ref[1:2, :]

    def scan(c, carry):
        out = []
        for b in range(n_seq):
            x_re, x_im = carry[2 * b], carry[2 * b + 1]
            row = pl.ds(b * n_chunks + c, 1)
            xp_ref[row, :half] = x_re
            xp_ref[row, half:] = x_im
            z = z_ref[row, :]
            out += [a_re * x_re - a_im * x_im + z[:, :half],
                    a_re * x_im + a_im * x_re + z[:, half:]]
        return tuple(out)

    zero = jnp.zeros((1, half), F32)
    lax.fori_loop(0, n_chunks, scan, (zero,) * (2 * n_seq))
    inter = _dot(xp_ref[...].astype(BF16), q_ref[...])
    for j in range(S5_CHUNK // 2):
        cols = slice(2 * j * LANES, (2 * j + 2) * LANES)
        y = _dot(x[:, :cols.stop], t_ref[:cols.stop, cols]) + inter[:, cols]
        y_ref[2 * j] = y[:, :LANES]
        y_ref[2 * j + 1] = y[:, LANES:]


def _s5_operators(a_re, a_im, log_step, b_re, b_im, c_re, c_im):
    hp = lax.Precision.HIGHEST
    L, kb, gb = S5_CHUNK, S5_GROUPS // 8, 8
    step = jnp.exp(log_step.astype(F32))[:, None]
    ar, ai = a_re.astype(F32), a_im.astype(F32)
    mag = jnp.exp(ar * step)
    lr, li = mag * jnp.cos(ai * step), mag * jnp.sin(ai * step)
    den = ar * ar + ai * ai
    cr = (((lr - 1.0) * ar + li * ai) / den)[:, None, :]
    ci = ((li * ar - (lr - 1.0) * ai) / den)[:, None, :]
    br, bi = b_re.astype(F32).transpose(0, 2, 1), b_im.astype(F32).transpose(0, 2, 1)
    bbr, bbi = cr * br - ci * bi, cr * bi + ci * br
    pr, pi = [jnp.ones_like(lr)], [jnp.zeros_like(li)]
    for _ in range(L):
        pr, pi = pr + [pr[-1] * lr - pi[-1] * li], pi + [pr[-1] * li + pi[-1] * lr]
    pr, pi = jnp.stack(pr), jnp.stack(pi)
    wr = pr[:L, :, None, :] * bbr - pi[:L, :, None, :] * bbi
    wi = pr[:L, :, None, :] * bbi + pi[:L, :, None, :] * bbr
    cre, cim = c_re.astype(F32), c_im.astype(F32)
    kern = (jnp.einsum('gcp,ngap->ngac', cre, wr, precision=hp)
            - jnp.einsum('gcp,ngap->ngac', cim, wi, precision=hp))

    def group_mask(rows, rows_per_group, cols, cols_per_group):
        r = lax.broadcasted_iota(jnp.int32, (rows, 1), 0) // rows_per_group
        c = lax.broadcasted_iota(jnp.int32, (1, cols), 1) // cols_per_group
        return r == c

    def spread_channels(m):
        src = lax.broadcasted_iota(jnp.int32, (S5_GROUP, LANES), 0)
        dst = lax.broadcasted_iota(jnp.int32, (S5_GROUP, LANES), 1)
        return jnp.dot(m.astype(BF16), (src == dst % S5_GROUP).astype(BF16),
                       preferred_element_type=BF16)

    zero = jnp.zeros((), BF16)
    k_blk = kern.reshape(L, kb, LANES, S5_GROUP).transpose(1, 0, 2, 3)
    k_blk = jnp.where(group_mask(LANES, S5_GROUP, LANES, S5_GROUP), spread_channels(k_blk), zero)

    def state_in(w):
        w = jnp.tile(w[::-1].astype(BF16).reshape(L, kb, LANES, S5_STATE), (1, 1, 1, gb))
        w = jnp.where(group_mask(LANES, S5_GROUP, gb * S5_STATE, S5_STATE), w, zero)
        return w.transpose(1, 0, 2, 3).reshape(kb, L * LANES, gb * S5_STATE)
    p_big = jnp.concatenate([state_in(wr), state_in(wi)], axis=2)

    cre_t, cim_t = cre.transpose(0, 2, 1), cim.transpose(0, 2, 1)
    def state_out(m):
        m = spread_channels(m.reshape(L, kb, gb * S5_STATE, S5_GROUP))
        m = jnp.where(group_mask(gb * S5_STATE, S5_STATE, LANES, S5_GROUP), m, zero)
        return m.transpose(1, 2, 0, 3).reshape(kb, gb * S5_STATE, L * LANES)
    q_re = cre_t[None] * pr[1:, :, :, None] - cim_t[None] * pi[1:, :, :, None]
    q_im = -(cre_t[None] * pi[1:, :, :, None] + cim_t[None] * pr[1:, :, :, None])
    q_big = jnp.concatenate([state_out(q_re), state_out(q_im)], axis=1)
    a_big = jnp.stack([pr[L].reshape(kb, gb * S5_STATE), pi[L].reshape(kb, gb * S5_STATE)], axis=1)
    return k_blk, p_big, q_big, a_big


def _s5_core(h2, bsz, seq, a_re, a_im, log_step, b_re, b_im, c_re, c_im):
    L = S5_CHUNK
    n_chunks = seq // L
    rows = bsz * n_chunks
    kb = D_MODEL // LANES
    rb = min(S5_ROWS, rows)
    assert rb % n_chunks == 0
    k_blk, p_big, q_big, a_big = _s5_operators(a_re, a_im, log_step, b_re, b_im, c_re, c_im)
    n_state = 8 * S5_STATE
    x_specs = [pl.BlockSpec((rb, LANES), lambda k, r, t=t: (r, t * kb + k)) for t in range(L)]
    return pl.pallas_call(
        functools.partial(_s5_kernel, n_chunks),
        grid=(kb, rows // rb),
        in_specs=x_specs + [
            pl.BlockSpec((None, L, LANES, LANES), lambda k, r: (k, 0, 0, 0)),
            pl.BlockSpec((None, L * LANES, 2 * n_state), lambda k, r: (k, 0, 0)),
            pl.BlockSpec((None, 2 * n_state, L * LANES), lambda k, r: (k, 0, 0)),
            pl.BlockSpec((None, 2, n_state), lambda k, r: (k, 0, 0))],
        out_specs=pl.BlockSpec((L, rb, LANES), lambda k, r: (0, r, k)),
        out_shape=jax.ShapeDtypeStruct((L, rows, D_MODEL), F32),
        scratch_shapes=[pltpu.VMEM((rb, 2 * n_state), F32), pltpu.VMEM((rb, 2 * n_state), F32),
                        pltpu.VMEM((L * LANES, L * LANES), BF16)],
        compiler_params=_params(2),
        name="s5_core",
    )(*([h2] * L), k_blk, p_big, q_big, a_big)


def _glu_kernel(x_ref, y_ref, g_ref, d_ref, wa_ref, wb_ref, o_ref, stage_ref):
    n_rows = y_ref.shape[1]
    n_blk = D_MODEL // LANES
    for k in range(n_blk):
        stage_ref[k] = x_ref[:, k * LANES:(k + 1) * LANES]
    x = jnp.concatenate(
        [jnp.concatenate([stage_ref[k, pl.ds(t, n_rows, stride=S5_CHUNK), :]
                          for k in range(n_blk)], axis=1) for t in range(S5_CHUNK)], axis=0)
    y = y_ref[...].reshape(x.shape) + d_ref[...] * _rms(x, g_ref[...])
    y = jax.nn.gelu(y).astype(BF16)
    res = x + _dot(y, wa_ref[...]) * jax.nn.sigmoid(_dot(y, wb_ref[...]))
    for t in range(S5_CHUNK):
        for k in range(n_blk):
            stage_ref[k, pl.ds(t, n_rows, stride=S5_CHUNK), :] = res[
                t * n_rows:(t + 1) * n_rows, k * LANES:(k + 1) * LANES]
    for k in range(n_blk):
        o_ref[:, k * LANES:(k + 1) * LANES] = stage_ref[k]


def _s5_layer(x, h2, bsz, seq, mix_g, a_re, a_im, log_step, b_re, b_im, c_re, c_im, d_skip,
              w_a, w_b):
    t = x.shape[0]
    tm = min(TOKEN_TILE, t)
    y = _s5_core(h2, bsz, seq, a_re, a_im, log_step, b_re, b_im, c_re, c_im)
    row = pl.BlockSpec((tm, D_MODEL), lambda i: (i, 0))
    return pl.pallas_call(
        _glu_kernel,
        grid=(t // tm,),
        in_specs=[row, pl.BlockSpec((S5_CHUNK, tm // S5_CHUNK, D_MODEL), lambda i: (0, i, 0)),
                  _const_spec((1, D_MODEL)), _const_spec((1, D_MODEL)),
                  _const_spec((D_MODEL, D_MODEL)), _const_spec((D_MODEL, D_MODEL))],
        out_specs=row,
        out_shape=jax.ShapeDtypeStruct((t, D_MODEL), F32),
        scratch_shapes=[pltpu.VMEM((D_MODEL // LANES, tm, LANES), F32)],
        compiler_params=_params(1),
        name="s5_glu",
    )(x, y, mix_g.astype(F32).reshape(1, D_MODEL), d_skip.astype(F32).reshape(1, D_MODEL),
      w_a.astype(BF16), w_b.astype(BF16))


def kernel(x, ffn1_norm, ffn1_gate, ffn1_up, ffn1_down, mix_norm, ffn2_norm, ffn2_gate, ffn2_up, ffn2_down, attn_w_in, attn_w_out, fg_bias, diff_q_norm, diff_k_norm, diff_lambda_q1, diff_lambda_k1, diff_lambda_q2, diff_lambda_k2, diff_subln, fox_q_norm, fox_k_norm, rel_bias, s5_a_re, s5_a_im, s5_log_step, s5_b_re, s5_b_im, s5_c_re, s5_c_im, s5_d, s5_glu_a, s5_glu_b):
    bsz, seq, _ = x.shape
    depth = ffn1_norm.shape[0]
    assert min(ATTN_Q, seq) >= REL_MAX_DIST, "bias tiles assume the bias saturates within a block"
    bias_tiles = _diff_bias_tiles(rel_bias, min(ATTN_Q, seq))
    xt = x.reshape(bsz * seq, D_MODEL).astype(F32)
    for i in range(depth):
        xt, h = _ffn(xt, ffn1_norm[i], ffn1_gate[i].astype(BF16), ffn1_up[i].astype(BF16),
                     ffn1_down[i].astype(BF16), g_next=mix_norm[i], chunked=i % 2 == 1)
        j = i // 2
        if i % 2 == 0:
            xt = _attn_layer(xt, h, bsz, seq, i, attn_w_in[j], attn_w_out[j], fg_bias[j],
                             diff_q_norm[j], diff_k_norm[j], diff_lambda_q1[j], diff_lambda_k1[j],
                             diff_lambda_q2[j], diff_lambda_k2[j], diff_subln[j], fox_q_norm[j],
                             fox_k_norm[j], bias_tiles)
        else:
            xt = _s5_layer(xt, h, bsz, seq, mix_norm[i], s5_a_re[j], s5_a_im[j], s5_log_step[j],
                           s5_b_re[j], s5_b_im[j], s5_c_re[j], s5_c_im[j], s5_d[j],
                           s5_glu_a[j], s5_glu_b[j])
        xt, _ = _ffn(xt, ffn2_norm[i], ffn2_gate[i].astype(BF16), ffn2_up[i].astype(BF16),
                     ffn2_down[i].astype(BF16))
    return xt.reshape(bsz, seq, D_MODEL).astype(x.dtype)
```

```python
import functools
import math

import jax
import jax.numpy as jnp
import numpy as np
from jax import lax
from jax.experimental import pallas as pl
from jax.experimental.pallas import tpu as pltpu

D_MODEL = 1024
D_FF = 2816
HEAD_DIM = 64
N_DIFF_HEADS = 4
N_FOX_HEADS = 8
N_REL_BUCKETS = 32
REL_MAX_DIST = 128
S5_GROUP = 16
S5_GROUPS = D_MODEL // S5_GROUP
S5_STATE = 64
NORM_EPS = 1e-6

QK_COLS = 2048
V_COLS = 1024
AUG_COLS = 1024
LANES = 128
MXU_DIM = 256
VMEM_LIMIT = 56 * 1024 * 1024

TOKEN_TILE = 512
FFN_TILE = 1024
FFN_SPLITS = (0, 6 * MXU_DIM, D_FF)
ATTN_Q = 512
ATTN_LANES = 256
ATTN_GROUP = 3
S5_CHUNK = 16
S5_ROWS = 512
NEG_BIG = -1e30
LOG2E = 1.4426950408889634

BF16 = jnp.bfloat16
F32 = jnp.float32


def _dot(a, b):
    return jnp.dot(a, b, preferred_element_type=F32)


def _dot_nt(a, b):
    return lax.dot_general(a, b, (((1,), (1,)), ((), ())), preferred_element_type=F32)


def _rms(x, g):
    ms = jnp.mean(x * x, axis=-1, keepdims=True)
    return x * lax.rsqrt(ms + NORM_EPS) * g


def _const_spec(shape):
    return pl.BlockSpec(shape, lambda *_: (0,) * len(shape), pipeline_mode=pl.Buffered(1))


def _params(n_axes):
    return pltpu.CompilerParams(dimension_semantics=("arbitrary",) * n_axes,
                                vmem_limit_bytes=VMEM_LIMIT)


def _split3(x):
    p0 = x.astype(BF16)
    r1 = x - p0.astype(F32)
    p1 = r1.astype(BF16)
    p2 = (r1 - p1.astype(F32)).astype(BF16)
    return p0, p1, p2


def _ffn_kernel(chunked, x_ref, g_ref, gn_ref, wg_ref, wu_ref, wd_ref, o_ref, *rest):
    x = x_ref[...]
    h = _rms(x, g_ref[...]).astype(BF16)
    acc = None
    for lo, hi in zip(FFN_SPLITS[:-1], FFN_SPLITS[1:]):
        sl = slice(lo, hi)
        a = _dot(h, wg_ref[:, sl])
        b = _dot(h, wu_ref[:, sl])
        act = (a * jax.nn.sigmoid(a) * b).astype(BF16)
        d = _dot(act, wd_ref[sl, :])
        acc = d if acc is None else acc + d
    xo = x + 0.5 * acc
    o_ref[...] = xo
    if not rest:
        return
    hn = _rms(xo, gn_ref[...])
    if not chunked:
        rest[0][...] = hn.astype(BF16)
        return
    h_ref, stage_ref = rest
    n_rows = h_ref.shape[0]
    for k in range(D_MODEL // LANES):
        stage_ref[k] = hn[:, k * LANES:(k + 1) * LANES]
    for t in range(S5_CHUNK):
        for k in range(D_MODEL // LANES):
            lo = t * D_MODEL + k * LANES
            h_ref[:, lo:lo + LANES] = stage_ref[k, pl.ds(t, n_rows, stride=S5_CHUNK), :].astype(BF16)


def _ffn(x, g, wg, wu, wd, g_next=None, chunked=False):
    t = x.shape[0]
    tm = min(FFN_TILE, t)
    emit = g_next is not None
    gn = g_next if emit else g
    row = pl.BlockSpec((tm, D_MODEL), lambda i: (i, 0))
    out_shape = [jax.ShapeDtypeStruct((t, D_MODEL), F32)]
    out_specs = [row]
    scratch = []
    if emit and chunked:
        out_shape.append(jax.ShapeDtypeStruct((t // S5_CHUNK, S5_CHUNK * D_MODEL), BF16))
        out_specs.append(pl.BlockSpec((tm // S5_CHUNK, S5_CHUNK * D_MODEL), lambda i: (i, 0)))
        scratch = [pltpu.VMEM((D_MODEL // LANES, tm, LANES), F32)]
    elif emit:
        out_shape.append(jax.ShapeDtypeStruct((t, D_MODEL), BF16))
        out_specs.append(row)
    res = pl.pallas_call(
        functools.partial(_ffn_kernel, chunked),
        grid=(t // tm,),
        in_specs=[row, _const_spec((1, D_MODEL)), _const_spec((1, D_MODEL)),
                  _const_spec((D_MODEL, D_FF)), _const_spec((D_MODEL, D_FF)),
                  _const_spec((D_FF, D_MODEL))],
        out_specs=out_specs,
        out_shape=out_shape,
        scratch_shapes=scratch,
        compiler_params=_params(1),
        name="ffn",
    )(x, g.reshape(1, D_MODEL), gn.reshape(1, D_MODEL), wg, wu, wd)
    return (res[0], res[1]) if emit else (res[0], None)


def _inproj_kernel(tiles_per_seq, h_ref, wqk_ref, wv_ref, wf_ref, gains_ref, ones_ref, tri_ref,
                   fgb_ref, sel_ref, aug1_ref, qk_ref, vt_ref, aug_ref, carry_ref):
    i = pl.program_id(0)
    h = h_ref[...]
    ones_blk = ones_ref[...]
    for c in range(QK_COLS // 512):
        y = _dot(h, wqk_ref[:, c * 512:(c + 1) * 512])
        y2 = y * y
        hi = y2.astype(BF16)
        lo = (y2 - hi.astype(F32)).astype(BF16)
        ssq = jnp.concatenate(
            [_dot(hi[:, k * MXU_DIM:(k + 1) * MXU_DIM], ones_blk)
             + _dot(lo[:, k * MXU_DIM:(k + 1) * MXU_DIM], ones_blk)
             for k in range(512 // MXU_DIM)], axis=1)
        y = y * lax.rsqrt(ssq * (1.0 / HEAD_DIM) + NORM_EPS) * gains_ref[c:c + 1, :]
        qk_ref[:, c * 512:(c + 1) * 512] = y.astype(BF16)
    for c in range(V_COLS // 512):
        vt = _dot_nt(wv_ref[c * 512:(c + 1) * 512, :], h)
        vt_ref[c * 512:(c + 1) * 512, :] = vt.astype(BF16)

    fl = _dot(h, wf_ref[...]) + fgb_ref[...]
    logf = -(jnp.maximum(-fl, 0.0) + jnp.log1p(jnp.exp(-jnp.abs(fl))))

    @pl.when(i % tiles_per_seq == 0)
    def _():
        carry_ref[...] = jnp.zeros_like(carry_ref)

    tri = tri_ref[...]
    cum = sum(_dot(tri, p) for p in _split3(logf)) + carry_ref[...]
    carry_ref[...] = cum[cum.shape[0] - 1:, :]
    aug = sum(_dot(p, sel_ref[n]) for n, p in enumerate(_split3(cum * LOG2E))) + aug1_ref[...]
    aug_ref[...] = aug.astype(BF16)


def _decay_lane_maps():
    sel = np.zeros((3, LANES, AUG_COLS), np.float32)
    ones = np.zeros((1, AUG_COLS), np.float32)
    k_off = AUG_COLS // 2
    for head in range(N_FOX_HEADS):
        base = (head // 2) * LANES + (head % 2) * 6
        for n in range(3):
            sel[n, head, base + n] = 1.0
            sel[n, head, k_off + base + 3 + n] = -1.0
            ones[0, base + 3 + n] = 1.0
            ones[0, k_off + base + n] = 1.0
    return jnp.asarray(sel, BF16), jnp.asarray(ones, F32)


def _inproj(h, w_qk, w_v, w_f, gains, fg_bias_row, bsz, seq):
    t = h.shape[0]
    tm = min(TOKEN_TILE, seq)
    tps = seq // tm
    ones_blk = jnp.kron(jnp.eye(MXU_DIM // HEAD_DIM, dtype=F32),
                        jnp.ones((HEAD_DIM, HEAD_DIM), F32)).astype(BF16)
    tri = jnp.tril(jnp.ones((tm, tm), F32)).astype(BF16)
    sel, aug_ones = _decay_lane_maps()
    return pl.pallas_call(
        functools.partial(_inproj_kernel, tps),
        grid=(t // tm,),
        in_specs=[pl.BlockSpec((tm, D_MODEL), lambda i: (i, 0)),
                  _const_spec((D_MODEL, QK_COLS)), _const_spec((D_MODEL, V_COLS)),
                  _const_spec((D_MODEL, LANES)),
                  _const_spec((QK_COLS // 512, 512)), _const_spec((MXU_DIM, MXU_DIM)),
                  _const_spec((tm, tm)), _const_spec((1, LANES)),
                  _const_spec((3, LANES, AUG_COLS)), _const_spec((1, AUG_COLS))],
        out_specs=[pl.BlockSpec((tm, QK_COLS), lambda i: (i, 0)),
                   pl.BlockSpec((None, V_COLS, tm), lambda i: (i // tps, 0, i % tps)),
                   pl.BlockSpec((tm, AUG_COLS), lambda i: (i, 0))],
        out_shape=[jax.ShapeDtypeStruct((t, QK_COLS), BF16),
                   jax.ShapeDtypeStruct((bsz, V_COLS, seq), BF16),
                   jax.ShapeDtypeStruct((t, AUG_COLS), BF16)],
        scratch_shapes=[pltpu.VMEM((1, LANES), F32)],
        compiler_params=_params(1),
        name="attn_inproj",
    )(h, w_qk, w_v, w_f, gains, ones_blk, tri, fg_bias_row, sel, aug_ones)


def _attn_sweep(qi, q_both, k_ref, kaug_ref, vt_ref, bias_ref, m_ref, l_ref, acc_ref):
    blk = q_both.shape[0] // 2
    m_ref[...] = jnp.full_like(m_ref, NEG_BIG)
    l_ref[...] = jnp.zeros_like(l_ref)
    acc_ref[...] = jnp.zeros_like(acc_ref)

    def run(items):
        rows = [pl.ds(pl.multiple_of(j * blk, blk), blk) for j, _, _ in items]
        lanes = [slice(c * ATTN_LANES, (c + 1) * ATTN_LANES)
                 for c in range(2 * blk // ATTN_LANES)]
        scores = {}
        for n, r in enumerate(rows):
            k_blk = k_ref[r, :]
            if kaug_ref is not None:
                k_blk = jnp.concatenate([k_blk, kaug_ref[r, :]], axis=1)
            for c, ls in enumerate(lanes):
                scores[n, c] = _dot_nt(k_blk, q_both[ls, :])
        for n, (_, bias_idx, masked) in enumerate(items):
            for c, ls in enumerate(lanes):
                s = scores.pop((n, c))
                if bias_idx is not None and bias_ref is not None:
                    s = bias_ref[bias_idx, :, ls] + s
                if masked:
                    key = lax.broadcasted_iota(jnp.int32, s.shape, 0)
                    qry = (lax.broadcasted_iota(jnp.int32, s.shape, 1) + ls.start) & (blk - 1)
                    s = jnp.where(key <= qry, s, NEG_BIG)
                m_prev = m_ref[:, ls]
                m_new = jnp.maximum(m_prev, jnp.max(s, axis=0, keepdims=True))
                alpha = jnp.exp2(m_prev - m_new)
                p = jnp.exp2(s - m_new)
                l_ref[:, ls] = alpha * l_ref[:, ls] + jnp.sum(p, axis=0, keepdims=True)
                m_ref[:, ls] = m_new
                acc_ref[:, ls] = alpha * acc_ref[:, ls] + _dot(vt_ref[:, rows[n]], p.astype(BF16))

    n_far = jnp.maximum(qi - 1, 0)

    def far_group(j, carry):
        run([(ATTN_GROUP * j + n, None, False) for n in range(ATTN_GROUP)])
        return carry

    lax.fori_loop(0, n_far // ATTN_GROUP, far_group, 0)
    prev, diag = (qi - 1, 1, False), (qi, 0, True)

    @pl.when(qi == 0)
    def _():
        run([diag])

    for left in range(ATTN_GROUP):
        @pl.when((qi >= 1) & (n_far % ATTN_GROUP == left))
        def _():
            run([(n_far - left + n, None, False) for n in range(left)] + [prev, diag])

    halves = (slice(None), slice(0, blk)), (slice(None), slice(blk, 2 * blk))
    return [acc_ref[sl] / l_ref[sl] for sl in halves]


def _diff_attn_kernel(scal_ref, q_ref, k_ref, vt_ref, bias_ref, subln_ref, o_ref,
                      m_ref, l_ref, acc_ref):
    blk = bias_ref.shape[1]
    lam = scal_ref[0]
    out_scale = scal_ref[1]

    def query_block(qi, carry):
        rows = pl.ds(pl.multiple_of(qi * blk, blk), blk)
        q = q_ref[rows, :]
        lane = lax.broadcasted_iota(jnp.int32, q.shape, 1)
        zero = jnp.zeros_like(q)
        q_both = jnp.concatenate([jnp.where(lane < HEAD_DIM, q, zero),
                                  jnp.where(lane < HEAD_DIM, zero, q)], axis=0)
        o1, o2 = _attn_sweep(qi, q_both, k_ref, None, vt_ref, bias_ref, m_ref, l_ref, acc_ref)
        o = o1 - lam * o2
        ms = jnp.mean(o * o, axis=0, keepdims=True)
        o = o * lax.rsqrt(ms + NORM_EPS) * subln_ref[...] * out_scale
        o_ref[rows, :] = o.T.astype(o_ref.dtype)
        return carry

    lax.fori_loop(0, q_ref.shape[0] // blk, query_block, 0)


def _fox_attn_kernel(blk, q_ref, qaug_ref, k_ref, kaug_ref, vt_ref, o_ref, m_ref, l_ref, acc_ref):
    def query_block(qi, carry):
        rows = pl.ds(pl.multiple_of(qi * blk, blk), blk)
        q = jnp.concatenate([q_ref[rows, :], qaug_ref[rows, :]], axis=1)
        lane = lax.broadcasted_iota(jnp.int32, q.shape, 1)
        zero = jnp.zeros_like(q)
        in_a = (lane < HEAD_DIM) | ((lane >= LANES) & (lane < LANES + 6))
        in_b = (((lane >= HEAD_DIM) & (lane < LANES))
                | ((lane >= LANES + 6) & (lane < LANES + 12)))
        q_both = jnp.concatenate([jnp.where(in_a, q, zero), jnp.where(in_b, q, zero)], axis=0)
        o1, o2 = _attn_sweep(qi, q_both, k_ref, kaug_ref, vt_ref, None, m_ref, l_ref, acc_ref)
        row = lax.broadcasted_iota(jnp.int32, o1.shape, 0)
        o_ref[rows, :] = jnp.where(row < HEAD_DIM, o1, o2).T.astype(o_ref.dtype)
        return carry

    lax.fori_loop(0, q_ref.shape[0] // blk, query_block, 0)


def _attn_scratch(tq):
    return [pltpu.VMEM((1, 2 * tq), F32), pltpu.VMEM((1, 2 * tq), F32),
            pltpu.VMEM((LANES, 2 * tq), F32)]


def _seq_block(col):
    return lambda seq: pl.BlockSpec((None, seq, LANES), lambda b, s: (b, 0, col(s)))


def _diff_attn(qk, vt, bias_tiles, scal, subln_col):
    bsz, seq, _ = qk.shape
    tq = min(ATTN_Q, seq)
    return pl.pallas_call(
        _diff_attn_kernel,
        grid=(bsz, N_DIFF_HEADS),
        in_specs=[pl.BlockSpec(memory_space=pltpu.SMEM),
                  _seq_block(lambda s: s)(seq), _seq_block(lambda s: 4 + s)(seq),
                  pl.BlockSpec((None, LANES, seq), lambda b, s: (b, s, 0)),
                  pl.BlockSpec((None, 2, tq, 2 * tq), lambda b, s: (s, 0, 0, 0)),
                  _const_spec((LANES, 1))],
        out_specs=_seq_block(lambda s: s)(seq),
        out_shape=jax.ShapeDtypeStruct((bsz, seq, N_DIFF_HEADS * LANES), BF16),
        scratch_shapes=_attn_scratch(tq),
        compiler_params=_params(2),
        name="diff_attn",
    )(scal, qk, qk, vt, bias_tiles, subln_col)


def _fox_attn(qk, aug, vt):
    bsz, seq, _ = qk.shape
    tq = min(ATTN_Q, seq)
    return pl.pallas_call(
        functools.partial(_fox_attn_kernel, tq),
        grid=(bsz, N_FOX_HEADS // 2),
        in_specs=[_seq_block(lambda s: 8 + s)(seq), _seq_block(lambda s: s)(seq),
                  _seq_block(lambda s: 12 + s)(seq), _seq_block(lambda s: 4 + s)(seq),
                  pl.BlockSpec((None, LANES, seq), lambda b, s: (b, 4 + s, 0))],
        out_specs=_seq_block(lambda s: s)(seq),
        out_shape=jax.ShapeDtypeStruct((bsz, seq, N_FOX_HEADS * HEAD_DIM), BF16),
        scratch_shapes=_attn_scratch(tq),
        compiler_params=_params(2),
        name="fox_attn",
    )(qk, aug, qk, aug, vt)


def _outproj_kernel(x_ref, od_ref, of_ref, wd_ref, wf_ref, o_ref):
    o_ref[...] = x_ref[...] + _dot(od_ref[...], wd_ref[...]) + _dot(of_ref[...], wf_ref[...])


def _outproj(x, od, of, w_d, w_f):
    t = x.shape[0]
    tm = min(TOKEN_TILE, t)
    half = od.shape[1]
    return pl.pallas_call(
        _outproj_kernel,
        grid=(t // tm,),
        in_specs=[pl.BlockSpec((tm, D_MODEL), lambda i: (i, 0)),
                  pl.BlockSpec((tm, half), lambda i: (i, 0)),
                  pl.BlockSpec((tm, half), lambda i: (i, 0)),
                  _const_spec((half, D_MODEL)), _const_spec((half, D_MODEL))],
        out_specs=pl.BlockSpec((tm, D_MODEL), lambda i: (i, 0)),
        out_shape=jax.ShapeDtypeStruct((t, D_MODEL), F32),
        compiler_params=_params(1),
        name="attn_outproj",
    )(x, od, of, w_d, w_f)


def _rel_bias_by_distance(rel_bias, n_dist):
    n = jnp.arange(n_dist, dtype=jnp.int32)
    max_exact = N_REL_BUCKETS // 2
    nf = jnp.maximum(n, 1).astype(F32)
    large = max_exact + (jnp.log(nf / max_exact) / math.log(REL_MAX_DIST / max_exact)
                         * (N_REL_BUCKETS - max_exact)).astype(jnp.int32)
    large = jnp.minimum(large, N_REL_BUCKETS - 1)
    return rel_bias[jnp.where(n < max_exact, n, large)]


def _toeplitz(w, n):
    heads, period = w.shape
    flat = jnp.tile(w, (1, n))[:, :n * (period - 1)]
    return flat.reshape(heads, n, period - 1)[:, :, :n]


def _diff_bias_tiles(rel_bias, tq):
    by_dist = _rel_bias_by_distance(rel_bias.astype(F32), 2 * tq)
    by_dist = ((by_dist - by_dist[2 * tq - 1]) * LOG2E).T
    diag = _toeplitz(by_dist, tq)
    prev = _toeplitz(jnp.roll(by_dist, -tq, axis=1), tq)
    tiles = jnp.stack([diag, prev], axis=1)
    return jnp.concatenate([tiles, tiles], axis=-1)


def _attn_layer(x, h, bsz, seq, layer_idx, w_in, w_out, fg_bias, dq_g, dk_g, lq1, lk1, lq2, lk2,
                subln_g, fq_g, fk_g, bias_tiles):
    tq = min(ATTN_Q, seq)
    q_scale = HEAD_DIM ** -0.5 * LOG2E
    w_bf = w_in.astype(BF16)
    w_qk = jnp.concatenate([w_bf[:, 0:1024], w_bf[:, 1536:2560]], axis=1)
    w_v = jnp.concatenate([w_bf[:, 1024:1536], w_bf[:, 2560:3072]], axis=1).T
    w_f = jnp.pad(w_bf[:, 3072:], ((0, 0), (0, LANES - N_FOX_HEADS)))
    fgb = jnp.pad(fg_bias.astype(F32), (0, LANES - N_FOX_HEADS)).reshape(1, LANES)
    gains = jnp.stack([jnp.tile(dq_g, 8) * q_scale, jnp.tile(dk_g, 8),
                       jnp.tile(fq_g, 8) * q_scale, jnp.tile(fk_g, 8)]).astype(F32)
    qk, vt, aug = _inproj(h, w_qk, w_v, w_f, gains, fgb, bsz, seq)
    qk = qk.reshape(bsz, seq, QK_COLS)

    lam_init = 0.8 - 0.6 * math.exp(-0.3 * layer_idx)
    lam = (jnp.exp(jnp.sum(lq1.astype(F32) * lk1.astype(F32)))
           - jnp.exp(jnp.sum(lq2.astype(F32) * lk2.astype(F32))) + lam_init)
    scal = jnp.stack([lam, jnp.asarray(1.0 - lam_init, F32)]).astype(F32)
    od = _diff_attn(qk, vt, bias_tiles, scal, subln_g.astype(F32).reshape(LANES, 1))

    of = _fox_attn(qk, aug.reshape(bsz, seq, AUG_COLS), vt)

    w_o = w_out.astype(BF16)
    half = N_DIFF_HEADS * LANES
    return _outproj(x, od.reshape(bsz * seq, half), of.reshape(bsz * seq, half),
                    w_o[:half], w_o[half:])


def _s5_kernel(n_chunks, *refs):
    x_refs = refs[:S5_CHUNK]
    kblk_ref, p_ref, q_ref, a_ref, y_ref, z_ref, xp_ref, t_ref = refs[S5_CHUNK:]

    @pl.when(pl.program_id(1) == 0)
    def _():
        t_ref[...] = jnp.zeros_like(t_ref)
        for s in range(S5_CHUNK):
            for t in range(s, S5_CHUNK):
                t_ref[s * LANES:(s + 1) * LANES, t * LANES:(t + 1) * LANES] = kblk_ref[t - s]

    x = jnp.concatenate([r[...] for r in x_refs], axis=1)
    n_seq = x.shape[0] // n_chunks
    half = z_ref.shape[1] // 2
    z_ref[...] = _dot(x, p_ref[...])
    a_re = a_ref[0:1, :]
    a_im = a_ref[1:2, :]

    def scan(c, carry):
        out = []
        for b in range(n_seq):
            x_re, x_im = carry[2 * b], carry[2 * b + 1]
            row = pl.ds(b * n_chunks + c, 1)
            xp_ref[row, :half] = x_re
            xp_ref[row, half:] = x_im
            z = z_ref[row, :]
            out += [a_re * x_re - a_im * x_im + z[:, :half],
                    a_re * x_im + a_im * x_re + z[:, half:]]
        return tuple(out)

    zero = jnp.zeros((1, half), F32)
    lax.fori_loop(0, n_chunks, scan, (zero,) * (2 * n_seq))
    inter = _dot(xp_ref[...].astype(BF16), q_ref[...])
    for j in range(S5_CHUNK // 2):
        cols = slice(2 * j * LANES, (2 * j + 2) * LANES)
        y = _dot(x[:, :cols.stop], t_ref[:cols.stop, cols]) + inter[:, cols]
        y_ref[2 * j] = y[:, :LANES]
        y_ref[2 * j + 1] = y[:, LANES:]


def _s5_operators(a_re, a_im, log_step, b_re, b_im, c_re, c_im):
    hp = lax.Precision.HIGHEST
    L, kb, gb = S5_CHUNK, S5_GROUPS // 8, 8
    step = jnp.exp(log_step.astype(F32))[:, None]
    ar, ai = a_re.astype(F32), a_im.astype(F32)
    mag = jnp.exp(ar * step)
    lr, li = mag * jnp.cos(ai * step), mag * jnp.sin(ai * step)
    den = ar * ar + ai * ai
    cr = (((lr - 1.0) * ar + li * ai) / den)[:, None, :]
    ci = ((li * ar - (lr - 1.0) * ai) / den)[:, None, :]
    br, bi = b_re.astype(F32).transpose(0, 2, 1), b_im.astype(F32).transpose(0, 2, 1)
    bbr, bbi = cr * br - ci * bi, cr * bi + ci * br
    pr, pi = [jnp.ones_like(lr)], [jnp.zeros_like(li)]
    for _ in range(L):
        pr, pi = pr + [pr[-1] * lr - pi[-1] * li], pi + [pr[-1] * li + pi[-1] * lr]
    pr, pi = jnp.stack(pr), jnp.stack(pi)
    wr = pr[:L, :, None, :] * bbr - pi[:L, :, None, :] * bbi
    wi = pr[:L, :, None, :] * bbi + pi[:L, :, None, :] * bbr
    cre, cim = c_re.astype(F32), c_im.astype(F32)
    kern = (jnp.einsum('gcp,ngap->ngac', cre, wr, precision=hp)
            - jnp.einsum('gcp,ngap->ngac', cim, wi, precision=hp))

    def group_mask(rows, rows_per_group, cols, cols_per_group):
        r = lax.broadcasted_iota(jnp.int32, (rows, 1), 0) // rows_per_group
        c = lax.broadcasted_iota(jnp.int32, (1, cols), 1) // cols_per_group
        return r == c

    def spread_channels(m):
        src = lax.broadcasted_iota(jnp.int32, (S5_GROUP, LANES), 0)
        dst = lax.broadcasted_iota(jnp.int32, (S5_GROUP, LANES), 1)
        return jnp.dot(m.astype(BF16), (src == dst % S5_GROUP).astype(BF16),
                       preferred_element_type=BF16)

    zero = jnp.zeros((), BF16)
    k_blk = kern.reshape(L, kb, LANES, S5_GROUP).transpose(1, 0, 2, 3)
    k_blk = jnp.where(group_mask(LANES, S5_GROUP, LANES, S5_GROUP), spread_channels(k_blk), zero)

    def state_in(w):
        w = jnp.tile(w[::-1].astype(BF16).reshape(L, kb, LANES, S5_STATE), (1, 1, 1, gb))
        w = jnp.where(group_mask(LANES, S5_GROUP, gb * S5_STATE, S5_STATE), w, zero)
        return w.transpose(1, 0, 2, 3).reshape(kb, L * LANES, gb * S5_STATE)
    p_big = jnp.concatenate([state_in(wr), state_in(wi)], axis=2)

    cre_t, cim_t = cre.transpose(0, 2, 1), cim.transpose(0, 2, 1)
    def state_out(m):
        m = spread_channels(m.reshape(L, kb, gb * S5_STATE, S5_GROUP))
        m = jnp.where(group_mask(gb * S5_STATE, S5_STATE, LANES, S5_GROUP), m, zero)
        return m.transpose(1, 2, 0, 3).reshape(kb, gb * S5_STATE, L * LANES)
    q_re = cre_t[None] * pr[1:, :, :, None] - cim_t[None] * pi[1:, :, :, None]
    q_im = -(cre_t[None] * pi[1:, :, :, None] + cim_t[None] * pr[1:, :, :, None])
    q_big = jnp.concatenate([state_out(q_re), state_out(q_im)], axis=1)
    a_big = jnp.stack([pr[L].reshape(kb, gb * S5_STATE), pi[L].reshape(kb, gb * S5_STATE)], axis=1)
    return k_blk, p_big, q_big, a_big


def _s5_core(h2, bsz, seq, a_re, a_im, log_step, b_re, b_im, c_re, c_im):
    L = S5_CHUNK
    n_chunks = seq // L
    rows = bsz * n_chunks
    kb = D_MODEL // LANES
    rb = min(S5_ROWS, rows)
    assert rb % n_chunks == 0
    k_blk, p_big, q_big, a_big = _s5_operators(a_re, a_im, log_step, b_re, b_im, c_re, c_im)
    n_state = 8 * S5_STATE
    x_specs = [pl.BlockSpec((rb, LANES), lambda k, r, t=t: (r, t * kb + k)) for t in range(L)]
    return pl.pallas_call(
        functools.partial(_s5_kernel, n_chunks),
        grid=(kb, rows // rb),
        in_specs=x_specs + [
            pl.BlockSpec((None, L, LANES, LANES), lambda k, r: (k, 0, 0, 0)),
            pl.BlockSpec((None, L * LANES, 2 * n_state), lambda k, r: (k, 0, 0)),
            pl.BlockSpec((None, 2 * n_state, L * LANES), lambda k, r: (k, 0, 0)),
            pl.BlockSpec((None, 2, n_state), lambda k, r: (k, 0, 0))],
        out_specs=pl.BlockSpec((L, rb, LANES), lambda k, r: (0, r, k)),
        out_shape=jax.ShapeDtypeStruct((L, rows, D_MODEL), F32),
        scratch_shapes=[pltpu.VMEM((rb, 2 * n_state), F32), pltpu.VMEM((rb, 2 * n_state), F32),
                        pltpu.VMEM((L * LANES, L * LANES), BF16)],
        compiler_params=_params(2),
        name="s5_core",
    )(*([h2] * L), k_blk, p_big, q_big, a_big)


def _glu_kernel(x_ref, y_ref, g_ref, d_ref, wa_ref, wb_ref, o_ref, stage_ref):
    n_rows = y_ref.shape[1]
    n_blk = D_MODEL // LANES
    for k in range(n_blk):
        stage_ref[k] = x_ref[:, k * LANES:(k + 1) * LANES]
    x = jnp.concatenate(
        [jnp.concatenate([stage_ref[k, pl.ds(t, n_rows, stride=S5_CHUNK), :]
                          for k in range(n_blk)], axis=1) for t in range(S5_CHUNK)], axis=0)
    y = y_ref[...].reshape(x.shape) + d_ref[...] * _rms(x, g_ref[...])
    y = jax.nn.gelu(y).astype(BF16)
    res = x + _dot(y, wa_ref[...]) * jax.nn.sigmoid(_dot(y, wb_ref[...]))
    for t in range(S5_CHUNK):
        for k in range(n_blk):
            stage_ref[k, pl.ds(t, n_rows, stride=S5_CHUNK), :] = res[
                t * n_rows:(t + 1) * n_rows, k * LANES:(k + 1) * LANES]
    for k in range(n_blk):
        o_ref[:, k * LANES:(k + 1) * LANES] = stage_ref[k]


def _s5_layer(x, h2, bsz, seq, mix_g, a_re, a_im, log_step, b_re, b_im, c_re, c_im, d_skip,
              w_a, w_b):
    t = x.shape[0]
    tm = min(TOKEN_TILE, t)
    y = _s5_core(h2, bsz, seq, a_re, a_im, log_step, b_re, b_im, c_re, c_im)
    row = pl.BlockSpec((tm, D_MODEL), lambda i: (i, 0))
    return pl.pallas_call(
        _glu_kernel,
        grid=(t // tm,),
        in_specs=[row, pl.BlockSpec((S5_CHUNK, tm // S5_CHUNK, D_MODEL), lambda i: (0, i, 0)),
                  _const_spec((1, D_MODEL)), _const_spec((1, D_MODEL)),
                  _const_spec((D_MODEL, D_MODEL)), _const_spec((D_MODEL, D_MODEL))],
        out_specs=row,
        out_shape=jax.ShapeDtypeStruct((t, D_MODEL), F32),
        scratch_shapes=[pltpu.VMEM((D_MODEL // LANES, tm, LANES), F32)],
        compiler_params=_params(1),
        name="s5_glu",
    )(x, y, mix_g.astype(F32).reshape(1, D_MODEL), d_skip.astype(F32).reshape(1, D_MODEL),
      w_a.astype(BF16), w_b.astype(BF16))


def kernel(x, ffn1_norm, ffn1_gate, ffn1_up, ffn1_down, mix_norm, ffn2_norm, ffn2_gate, ffn2_up, ffn2_down, attn_w_in, attn_w_out, fg_bias, diff_q_norm, diff_k_norm, diff_lambda_q1, diff_lambda_k1, diff_lambda_q2, diff_lambda_k2, diff_subln, fox_q_norm, fox_k_norm, rel_bias, s5_a_re, s5_a_im, s5_log_step, s5_b_re, s5_b_im, s5_c_re, s5_c_im, s5_d, s5_glu_a, s5_glu_b):
    bsz, seq, _ = x.shape
    depth = ffn1_norm.shape[0]
    assert min(ATTN_Q, seq) >= REL_MAX_DIST, "bias tiles assume the bias saturates within a block"
    bias_tiles = _diff_bias_tiles(rel_bias, min(ATTN_Q, seq))
    xt = x.reshape(bsz * seq, D_MODEL).astype(F32)
    for i in range(depth):
        xt, h = _ffn(xt, ffn1_norm[i], ffn1_gate[i].astype(BF16), ffn1_up[i].astype(BF16),
                     ffn1_down[i].astype(BF16), g_next=mix_norm[i], chunked=i % 2 == 1)
        j = i // 2
        if i % 2 == 0:
            xt = _attn_layer(xt, h, bsz, seq, i, attn_w_in[j], attn_w_out[j], fg_bias[j],
                             diff_q_norm[j], diff_k_norm[j], diff_lambda_q1[j], diff_lambda_k1[j],
                             diff_lambda_q2[j], diff_lambda_k2[j], diff_subln[j], fox_q_norm[j],
                             fox_k_norm[j], bias_tiles)
        else:
            xt = _s5_layer(xt, h, bsz, seq, mix_norm[i], s5_a_re[j], s5_a_im[j], s5_log_step[j],
                           s5_b_re[j], s5_b_im[j], s5_c_re[j], s5_c_im[j], s5_d[j],
                           s5_glu_a[j], s5_glu_b[j])
        xt, _ = _ffn(xt, ffn2_norm[i], ffn2_gate[i].astype(BF16), ffn2_up[i].astype(BF16),
                     ffn2_down[i].astype(BF16))
    return xt.reshape(bsz, seq, D_MODEL).astype(x.dtype)
```

```python
import functools
import math

import jax
import jax.numpy as jnp
import numpy as np
from jax import lax
from jax.experimental import pallas as pl
from jax.experimental.pallas import tpu as pltpu

D_MODEL = 1024
D_FF = 2816
HEAD_DIM = 64
N_DIFF_HEADS = 4
N_FOX_HEADS = 8
N_REL_BUCKETS = 32
REL_MAX_DIST = 128
S5_GROUP = 16
S5_GROUPS = D_MODEL // S5_GROUP
S5_STATE = 64
NORM_EPS = 1e-6

QK_COLS = 2048
V_COLS = 1024
AUG_COLS = 256
LANES = 128
MXU_DIM = 256
VMEM_LIMIT = 56 * 1024 * 1024

TOKEN_TILE = 512
FFN_TILE = 1024
FFN_SPLITS = (0, 6 * MXU_DIM, D_FF)
ATTN_Q = 512
ATTN_LANES = 256
ATTN_GROUP = 3
S5_CHUNK = 16
S5_ROWS = 512
NEG_BIG = -1e30
LOG2E = 1.4426950408889634

BF16 = jnp.bfloat16
F32 = jnp.float32


def _dot(a, b):
    return jnp.dot(a, b, preferred_element_type=F32)


def _dot_nt(a, b):
    return lax.dot_general(a, b, (((1,), (1,)), ((), ())), preferred_element_type=F32)


def _rms(x, g):
    ms = jnp.mean(x * x, axis=-1, keepdims=True)
    return x * lax.rsqrt(ms + NORM_EPS) * g


def _const_spec(shape):
    return pl.BlockSpec(shape, lambda *_: (0,) * len(shape), pipeline_mode=pl.Buffered(1))


def _params(n_axes):
    return pltpu.CompilerParams(dimension_semantics=("arbitrary",) * n_axes,
                                vmem_limit_bytes=VMEM_LIMIT)


def _split3(x):
    p0 = x.astype(BF16)
    r1 = x - p0.astype(F32)
    p1 = r1.astype(BF16)
    p2 = (r1 - p1.astype(F32)).astype(BF16)
    return p0, p1, p2


def _ffn_kernel(chunked, x_ref, g_ref, gn_ref, wg_ref, wu_ref, wd_ref, o_ref, *rest):
    x = x_ref[...]
    h = _rms(x, g_ref[...]).astype(BF16)
    acc = None
    for lo, hi in zip(FFN_SPLITS[:-1], FFN_SPLITS[1:]):
        sl = slice(lo, hi)
        a = _dot(h, wg_ref[:, sl])
        b = _dot(h, wu_ref[:, sl])
        act = (a * jax.nn.sigmoid(a) * b).astype(BF16)
        d = _dot(act, wd_ref[sl, :])
        acc = d if acc is None else acc + d
    xo = x + 0.5 * acc
    o_ref[...] = xo
    if not rest:
        return
    hn = _rms(xo, gn_ref[...])
    if not chunked:
        rest[0][...] = hn.astype(BF16)
        return
    h_ref, stage_ref = rest
    n_rows = h_ref.shape[0]
    for k in range(D_MODEL // LANES):
        stage_ref[k] = hn[:, k * LANES:(k + 1) * LANES]
    for t in range(S5_CHUNK):
        for k in range(D_MODEL // LANES):
            lo = t * D_MODEL + k * LANES
            h_ref[:, lo:lo + LANES] = stage_ref[k, pl.ds(t, n_rows, stride=S5_CHUNK), :].astype(BF16)


def _ffn(x, g, wg, wu, wd, g_next=None, chunked=False):
    t = x.shape[0]
    tm = min(FFN_TILE, t)
    emit = g_next is not None
    gn = g_next if emit else g
    row = pl.BlockSpec((tm, D_MODEL), lambda i: (i, 0))
    out_shape = [jax.ShapeDtypeStruct((t, D_MODEL), F32)]
    out_specs = [row]
    scratch = []
    if emit and chunked:
        out_shape.append(jax.ShapeDtypeStruct((t // S5_CHUNK, S5_CHUNK * D_MODEL), BF16))
        out_specs.append(pl.BlockSpec((tm // S5_CHUNK, S5_CHUNK * D_MODEL), lambda i: (i, 0)))
        scratch = [pltpu.VMEM((D_MODEL // LANES, tm, LANES), F32)]
    elif emit:
        out_shape.append(jax.ShapeDtypeStruct((t, D_MODEL), BF16))
        out_specs.append(row)
    res = pl.pallas_call(
        functools.partial(_ffn_kernel, chunked),
        grid=(t // tm,),
        in_specs=[row, _const_spec((1, D_MODEL)), _const_spec((1, D_MODEL)),
                  _const_spec((D_MODEL, D_FF)), _const_spec((D_MODEL, D_FF)),
                  _const_spec((D_FF, D_MODEL))],
        out_specs=out_specs,
        out_shape=out_shape,
        scratch_shapes=scratch,
        compiler_params=_params(1),
        name="ffn",
    )(x, g.reshape(1, D_MODEL), gn.reshape(1, D_MODEL), wg, wu, wd)
    return (res[0], res[1]) if emit else (res[0], None)


def _inproj_kernel(tiles_per_seq, h_ref, wqk_ref, wv_ref, wf_ref, gains_ref, ones_ref, tri_ref,
                   fgb_ref, sel_ref, aug1_ref, qk_ref, vt_ref, aug_ref, carry_ref):
    i = pl.program_id(0)
    h = h_ref[...]
    ones_blk = ones_ref[...]
    for c in range(QK_COLS // 512):
        y = _dot(h, wqk_ref[:, c * 512:(c + 1) * 512])
        y2 = y * y
        hi = y2.astype(BF16)
        lo = (y2 - hi.astype(F32)).astype(BF16)
        ssq = jnp.concatenate(
            [_dot(hi[:, k * MXU_DIM:(k + 1) * MXU_DIM], ones_blk)
             + _dot(lo[:, k * MXU_DIM:(k + 1) * MXU_DIM], ones_blk)
             for k in range(512 // MXU_DIM)], axis=1)
        y = y * lax.rsqrt(ssq * (1.0 / HEAD_DIM) + NORM_EPS) * gains_ref[c:c + 1, :]
        qk_ref[:, c * 512:(c + 1) * 512] = y.astype(BF16)
    for c in range(V_COLS // 512):
        vt = _dot_nt(wv_ref[c * 512:(c + 1) * 512, :], h)
        vt_ref[c * 512:(c + 1) * 512, :] = vt.astype(BF16)

    fl = _dot(h, wf_ref[...]) + fgb_ref[...]
    logf = -(jnp.maximum(-fl, 0.0) + jnp.log1p(jnp.exp(-jnp.abs(fl))))

    @pl.when(i % tiles_per_seq == 0)
    def _():
        carry_ref[...] = jnp.zeros_like(carry_ref)

    tri = tri_ref[...]
    cum = sum(_dot(tri, p) for p in _split3(logf)) + carry_ref[...]
    carry_ref[...] = cum[cum.shape[0] - 1:, :]
    aug = sum(_dot(p, sel_ref[n]) for n, p in enumerate(_split3(cum * LOG2E))) + aug1_ref[...]
    aug_ref[...] = aug.astype(BF16)


def _decay_lane_maps():
    sel = np.zeros((3, LANES, AUG_COLS), np.float32)
    ones = np.zeros((1, AUG_COLS), np.float32)
    k_off = AUG_COLS // 2
    for head in range(N_FOX_HEADS):
        base = head * 6
        for n in range(3):
            sel[n, head, base + n] = 1.0
            sel[n, head, k_off + base + 3 + n] = -1.0
            ones[0, base + 3 + n] = 1.0
            ones[0, k_off + base + n] = 1.0
    return jnp.asarray(sel, BF16), jnp.asarray(ones, F32)


def _inproj(h, w_qk, w_v, w_f, gains, fg_bias_row, bsz, seq):
    t = h.shape[0]
    tm = min(TOKEN_TILE, seq)
    tps = seq // tm
    ones_blk = jnp.kron(jnp.eye(MXU_DIM // HEAD_DIM, dtype=F32),
                        jnp.ones((HEAD_DIM, HEAD_DIM), F32)).astype(BF16)
    tri = jnp.tril(jnp.ones((tm, tm), F32)).astype(BF16)
    sel, aug_ones = _decay_lane_maps()
    return pl.pallas_call(
        functools.partial(_inproj_kernel, tps),
        grid=(t // tm,),
        in_specs=[pl.BlockSpec((tm, D_MODEL), lambda i: (i, 0)),
                  _const_spec((D_MODEL, QK_COLS)), _const_spec((D_MODEL, V_COLS)),
                  _const_spec((D_MODEL, LANES)),
                  _const_spec((QK_COLS // 512, 512)), _const_spec((MXU_DIM, MXU_DIM)),
                  _const_spec((tm, tm)), _const_spec((1, LANES)),
                  _const_spec((3, LANES, AUG_COLS)), _const_spec((1, AUG_COLS))],
        out_specs=[pl.BlockSpec((tm, QK_COLS), lambda i: (i, 0)),
                   pl.BlockSpec((None, V_COLS, tm), lambda i: (i // tps, 0, i % tps)),
                   pl.BlockSpec((tm, AUG_COLS), lambda i: (i, 0))],
        out_shape=[jax.ShapeDtypeStruct((t, QK_COLS), BF16),
                   jax.ShapeDtypeStruct((bsz, V_COLS, seq), BF16),
                   jax.ShapeDtypeStruct((t, AUG_COLS), BF16)],
        scratch_shapes=[pltpu.VMEM((1, LANES), F32)],
        compiler_params=_params(1),
        name="attn_inproj",
    )(h, w_qk, w_v, w_f, gains, ones_blk, tri, fg_bias_row, sel, aug_ones)


def _attn_sweep(qi, q_both, k_ref, kaug_ref, vt_ref, bias_ref, m_ref, l_ref, acc_ref):
    blk = q_both.shape[0] // 2
    m_ref[...] = jnp.full_like(m_ref, NEG_BIG)
    l_ref[...] = jnp.zeros_like(l_ref)
    acc_ref[...] = jnp.zeros_like(acc_ref)

    def run(items):
        starts = [pl.multiple_of(j * blk, blk) for j, _, _ in items]
        lanes = [slice(c * ATTN_LANES, (c + 1) * ATTN_LANES)
                 for c in range(2 * blk // ATTN_LANES)]

        def n_keys(n, ls):
            first_half = (ls.start % blk) + ATTN_LANES <= blk // 2
            return blk // 2 if items[n][2] and first_half else blk

        scores = {}
        for n, start in enumerate(starts):
            k_blk = k_ref[pl.ds(start, blk), :]
            if kaug_ref is not None:
                k_blk = jnp.concatenate([k_blk, kaug_ref[pl.ds(start, blk), :]], axis=1)
            for c, ls in enumerate(lanes):
                scores[n, c] = _dot_nt(k_blk[:n_keys(n, ls)], q_both[ls, :])
        for n, (_, bias_idx, masked) in enumerate(items):
            for c, ls in enumerate(lanes):
                s = scores.pop((n, c))
                keys = s.shape[0]
                if bias_idx is not None and bias_ref is not None:
                    s = bias_ref[bias_idx, :keys, ls] + s
                if masked:
                    key = lax.broadcasted_iota(jnp.int32, s.shape, 0)
                    qry = (lax.broadcasted_iota(jnp.int32, s.shape, 1) + ls.start) & (blk - 1)
                    s = jnp.where(key <= qry, s, NEG_BIG)
                m_prev = m_ref[:, ls]
                m_new = jnp.maximum(m_prev, jnp.max(s, axis=0, keepdims=True))
                alpha = jnp.exp2(m_prev - m_new)
                p = jnp.exp2(s - m_new)
                l_ref[:, ls] = alpha * l_ref[:, ls] + jnp.sum(p, axis=0, keepdims=True)
                m_ref[:, ls] = m_new
                acc_ref[:, ls] = alpha * acc_ref[:, ls] + _dot(
                    vt_ref[:, pl.ds(starts[n], keys)], p.astype(BF16))

    n_far = jnp.maximum(qi - 1, 0)

    def far_group(j, carry):
        run([(ATTN_GROUP * j + n, None, False) for n in range(ATTN_GROUP)])
        return carry

    lax.fori_loop(0, n_far // ATTN_GROUP, far_group, 0)
    prev, diag = (qi - 1, 1, False), (qi, 0, True)

    @pl.when(qi == 0)
    def _():
        run([diag])

    for left in range(ATTN_GROUP):
        @pl.when((qi >= 1) & (n_far % ATTN_GROUP == left))
        def _():
            run([(n_far - left + n, None, False) for n in range(left)] + [prev, diag])

    halves = (slice(None), slice(0, blk)), (slice(None), slice(blk, 2 * blk))
    return [acc_ref[sl] / l_ref[sl] for sl in halves]


def _diff_attn_kernel(scal_ref, q_ref, k_ref, vt_ref, bias_ref, subln_ref, o_ref,
                      m_ref, l_ref, acc_ref):
    blk = bias_ref.shape[1]
    lam = scal_ref[0]
    out_scale = scal_ref[1]

    def query_block(qi, carry):
        rows = pl.ds(pl.multiple_of(qi * blk, blk), blk)
        q = q_ref[rows, :]
        lane = lax.broadcasted_iota(jnp.int32, q.shape, 1)
        zero = jnp.zeros_like(q)
        q_both = jnp.concatenate([jnp.where(lane < HEAD_DIM, q, zero),
                                  jnp.where(lane < HEAD_DIM, zero, q)], axis=0)
        o1, o2 = _attn_sweep(qi, q_both, k_ref, None, vt_ref, bias_ref, m_ref, l_ref, acc_ref)
        o = o1 - lam * o2
        ms = jnp.mean(o * o, axis=0, keepdims=True)
        o = o * lax.rsqrt(ms + NORM_EPS) * subln_ref[...] * out_scale
        o_ref[rows, :] = o.T.astype(o_ref.dtype)
        return carry

    lax.fori_loop(0, q_ref.shape[0] // blk, query_block, 0)


def _fox_attn_kernel(blk, q_ref, qaug_ref, k_ref, kaug_ref, vt_ref, o_ref, m_ref, l_ref, acc_ref):
    decay = LANES + 12 * pl.program_id(1)

    def query_block(qi, carry):
        rows = pl.ds(pl.multiple_of(qi * blk, blk), blk)
        q = jnp.concatenate([q_ref[rows, :], qaug_ref[rows, :]], axis=1)
        lane = lax.broadcasted_iota(jnp.int32, q.shape, 1)
        zero = jnp.zeros_like(q)
        in_a = (lane < HEAD_DIM) | ((lane >= decay) & (lane < decay + 6))
        in_b = (((lane >= HEAD_DIM) & (lane < LANES))
                | ((lane >= decay + 6) & (lane < decay + 12)))
        q_both = jnp.concatenate([jnp.where(in_a, q, zero), jnp.where(in_b, q, zero)], axis=0)
        o1, o2 = _attn_sweep(qi, q_both, k_ref, kaug_ref, vt_ref, None, m_ref, l_ref, acc_ref)
        row = lax.broadcasted_iota(jnp.int32, o1.shape, 0)
        o_ref[rows, :] = jnp.where(row < HEAD_DIM, o1, o2).T.astype(o_ref.dtype)
        return carry

    lax.fori_loop(0, q_ref.shape[0] // blk, query_block, 0)


def _attn_scratch(tq):
    return [pltpu.VMEM((1, 2 * tq), F32), pltpu.VMEM((1, 2 * tq), F32),
            pltpu.VMEM((LANES, 2 * tq), F32)]


def _seq_block(col):
    return lambda seq: pl.BlockSpec((None, seq, LANES), lambda b, s: (b, 0, col(s)))


def _diff_attn(qk, vt, bias_tiles, scal, subln_col):
    bsz, seq, _ = qk.shape
    tq = min(ATTN_Q, seq)
    return pl.pallas_call(
        _diff_attn_kernel,
        grid=(bsz, N_DIFF_HEADS),
        in_specs=[pl.BlockSpec(memory_space=pltpu.SMEM),
                  _seq_block(lambda s: s)(seq), _seq_block(lambda s: 4 + s)(seq),
                  pl.BlockSpec((None, LANES, seq), lambda b, s: (b, s, 0)),
                  pl.BlockSpec((None, 2, tq, 2 * tq), lambda b, s: (s, 0, 0, 0)),
                  _const_spec((LANES, 1))],
        out_specs=_seq_block(lambda s: s)(seq),
        out_shape=jax.ShapeDtypeStruct((bsz, seq, N_DIFF_HEADS * LANES), BF16),
        scratch_shapes=_attn_scratch(tq),
        compiler_params=_params(2),
        name="diff_attn",
    )(scal, qk, qk, vt, bias_tiles, subln_col)


def _fox_attn(qk, aug, vt):
    bsz, seq, _ = qk.shape
    tq = min(ATTN_Q, seq)
    return pl.pallas_call(
        functools.partial(_fox_attn_kernel, tq),
        grid=(bsz, N_FOX_HEADS // 2),
        in_specs=[_seq_block(lambda s: 8 + s)(seq), _seq_block(lambda s: 0)(seq),
                  _seq_block(lambda s: 12 + s)(seq), _seq_block(lambda s: 1)(seq),
                  pl.BlockSpec((None, LANES, seq), lambda b, s: (b, 4 + s, 0))],
        out_specs=_seq_block(lambda s: s)(seq),
        out_shape=jax.ShapeDtypeStruct((bsz, seq, N_FOX_HEADS * HEAD_DIM), BF16),
        scratch_shapes=_attn_scratch(tq),
        compiler_params=_params(2),
        name="fox_attn",
    )(qk, aug, qk, aug, vt)


def _outproj_kernel(x_ref, od_ref, of_ref, wd_ref, wf_ref, o_ref):
    o_ref[...] = x_ref[...] + _dot(od_ref[...], wd_ref[...]) + _dot(of_ref[...], wf_ref[...])


def _outproj(x, od, of, w_d, w_f):
    t = x.shape[0]
    tm = min(TOKEN_TILE, t)
    half = od.shape[1]
    return pl.pallas_call(
        _outproj_kernel,
        grid=(t // tm,),
        in_specs=[pl.BlockSpec((tm, D_MODEL), lambda i: (i, 0)),
                  pl.BlockSpec((tm, half), lambda i: (i, 0)),
                  pl.BlockSpec((tm, half), lambda i: (i, 0)),
                  _const_spec((half, D_MODEL)), _const_spec((half, D_MODEL))],
        out_specs=pl.BlockSpec((tm, D_MODEL), lambda i: (i, 0)),
        out_shape=jax.ShapeDtypeStruct((t, D_MODEL), F32),
        compiler_params=_params(1),
        name="attn_outproj",
    )(x, od, of, w_d, w_f)


def _rel_bias_by_distance(rel_bias, n_dist):
    n = jnp.arange(n_dist, dtype=jnp.int32)
    max_exact = N_REL_BUCKETS // 2
    nf = jnp.maximum(n, 1).astype(F32)
    large = max_exact + (jnp.log(nf / max_exact) / math.log(REL_MAX_DIST / max_exact)
                         * (N_REL_BUCKETS - max_exact)).astype(jnp.int32)
    large = jnp.minimum(large, N_REL_BUCKETS - 1)
    return rel_bias[jnp.where(n < max_exact, n, large)]


def _toeplitz(w, n):
    heads, period = w.shape
    flat = jnp.tile(w, (1, n))[:, :n * (period - 1)]
    return flat.reshape(heads, n, period - 1)[:, :, :n]


def _diff_bias_tiles(rel_bias, tq):
    by_dist = _rel_bias_by_distance(rel_bias.astype(F32), 2 * tq)
    by_dist = ((by_dist - by_dist[2 * tq - 1]) * LOG2E).T
    diag = _toeplitz(by_dist, tq)
    prev = _toeplitz(jnp.roll(by_dist, -tq, axis=1), tq)
    tiles = jnp.stack([diag, prev], axis=1)
    return jnp.concatenate([tiles, tiles], axis=-1)


def _attn_layer(x, h, bsz, seq, layer_idx, w_in, w_out, fg_bias, dq_g, dk_g, lq1, lk1, lq2, lk2,
                subln_g, fq_g, fk_g, bias_tiles):
    tq = min(ATTN_Q, seq)
    q_scale = HEAD_DIM ** -0.5 * LOG2E
    w_bf = w_in.astype(BF16)
    w_qk = jnp.concatenate([w_bf[:, 0:1024], w_bf[:, 1536:2560]], axis=1)
    w_v = jnp.concatenate([w_bf[:, 1024:1536], w_bf[:, 2560:3072]], axis=1).T
    w_f = jnp.pad(w_bf[:, 3072:], ((0, 0), (0, LANES - N_FOX_HEADS)))
    fgb = jnp.pad(fg_bias.astype(F32), (0, LANES - N_FOX_HEADS)).reshape(1, LANES)
    gains = jnp.stack([jnp.tile(dq_g, 8) * q_scale, jnp.tile(dk_g, 8),
                       jnp.tile(fq_g, 8) * q_scale, jnp.tile(fk_g, 8)]).astype(F32)
    qk, vt, aug = _inproj(h, w_qk, w_v, w_f, gains, fgb, bsz, seq)
    qk = qk.reshape(bsz, seq, QK_COLS)

    lam_init = 0.8 - 0.6 * math.exp(-0.3 * layer_idx)
    lam = (jnp.exp(jnp.sum(lq1.astype(F32) * lk1.astype(F32)))
           - jnp.exp(jnp.sum(lq2.astype(F32) * lk2.astype(F32))) + lam_init)
    scal = jnp.stack([lam, jnp.asarray(1.0 - lam_init, F32)]).astype(F32)
    od = _diff_attn(qk, vt, bias_tiles, scal, subln_g.astype(F32).reshape(LANES, 1))

    of = _fox_attn(qk, aug.reshape(bsz, seq, AUG_COLS), vt)

    w_o = w_out.astype(BF16)
    half = N_DIFF_HEADS * LANES
    return _outproj(x, od.reshape(bsz * seq, half), of.reshape(bsz * seq, half),
                    w_o[:half], w_o[half:])


def _s5_kernel(n_chunks, *refs):
    x_refs = refs[:S5_CHUNK]
    kblk_ref, p_ref, q_ref, a_ref, y_ref, z_ref, xp_ref, t_ref = refs[S5_CHUNK:]

    @pl.when(pl.program_id(1) == 0)
    def _():
        t_ref[...] = jnp.zeros_like(t_ref)
        for s in range(S5_CHUNK):
            for t in range(s, S5_CHUNK):
                t_ref[s * LANES:(s + 1) * LANES, t * LANES:(t + 1) * LANES] = kblk_ref[t - s]

    x = jnp.concatenate([r[...] for r in x_refs], axis=1)
    n_seq = x.shape[0] // n_chunks
    half = z_ref.shape[1] // 2
    z_ref[...] = _dot(x, p_ref[...])
    a_re = a_ref[0:1, :]
    a_im = a_ref[1:2, :]

    def scan(c, carry):
        out = []
        for b in range(n_seq):
            x_re, x_im = carry[2 * b], carry[2 * b + 1]
            row = pl.ds(b * n_chunks + c, 1)
            xp_ref[row, :half] = x_re
            xp_ref[row, half:] = x_im
            z = z_ref[row, :]
            out += [a_re * x_re - a_im * x_im + z[:, :half],
                    a_re * x_im + a_im * x_re + z[:, half:]]
        return tuple(out)

    zero = jnp.zeros((1, half), F32)
    lax.fori_loop(0, n_chunks, scan, (zero,) * (2 * n_seq))
    inter = _dot(xp_ref[...].astype(BF16), q_ref[...])
    for j in range(S5_CHUNK // 2):
        cols = slice(2 * j * LANES, (2 * j + 2) * LANES)
        y = _dot(x[:, :cols.stop], t_ref[:cols.stop, cols]) + inter[:, cols]
        y_ref[2 * j] = y[:, :LANES]
        y_ref[2 * j + 1] = y[:, LANES:]


def _s5_operators(a_re, a_im, log_step, b_re, b_im, c_re, c_im):
    hp = lax.Precision.HIGHEST
    L, kb, gb = S5_CHUNK, S5_GROUPS // 8, 8
    step = jnp.exp(log_step.astype(F32))[:, None]
    ar, ai = a_re.astype(F32), a_im.astype(F32)
    mag = jnp.exp(ar * step)
    lr, li = mag * jnp.cos(ai * step), mag * jnp.sin(ai * step)
    den = ar * ar + ai * ai
    cr = (((lr - 1.0) * ar + li * ai) / den)[:, None, :]
    ci = ((li * ar - (lr - 1.0) * ai) / den)[:, None, :]
    br, bi = b_re.astype(F32).transpose(0, 2, 1), b_im.astype(F32).transpose(0, 2, 1)
    bbr, bbi = cr * br - ci * bi, cr * bi + ci * br
    pr, pi = [jnp.ones_like(lr)], [jnp.zeros_like(li)]
    for _ in range(L):
        pr, pi = pr + [pr[-1] * lr - pi[-1] * li], pi + [pr[-1] * li + pi[-1] * lr]
    pr, pi = jnp.stack(pr), jnp.stack(pi)
    wr = pr[:L, :, None, :] * bbr - pi[:L, :, None, :] * bbi
    wi = pr[:L, :, None, :] * bbi + pi[:L, :, None, :] * bbr
    cre, cim = c_re.astype(F32), c_im.astype(F32)
    kern = (jnp.einsum('gcp,ngap->ngac', cre, wr, precision=hp)
            - jnp.einsum('gcp,ngap->ngac', cim, wi, precision=hp))

    def group_mask(rows, rows_per_group, cols, cols_per_group):
        r = lax.broadcasted_iota(jnp.int32, (rows, 1), 0) // rows_per_group
        c = lax.broadcasted_iota(jnp.int32, (1, cols), 1) // cols_per_group
        return r == c

    def spread_channels(m):
        src = lax.broadcasted_iota(jnp.int32, (S5_GROUP, LANES), 0)
        dst = lax.broadcasted_iota(jnp.int32, (S5_GROUP, LANES), 1)
        return jnp.dot(m.astype(BF16), (src == dst % S5_GROUP).astype(BF16),
                       preferred_element_type=BF16)

    zero = jnp.zeros((), BF16)
    k_blk = kern.reshape(L, kb, LANES, S5_GROUP).transpose(1, 0, 2, 3)
    k_blk = jnp.where(group_mask(LANES, S5_GROUP, LANES, S5_GROUP), spread_channels(k_blk), zero)

    def state_in(w):
        w = jnp.tile(w[::-1].astype(BF16).reshape(L, kb, LANES, S5_STATE), (1, 1, 1, gb))
        w = jnp.where(group_mask(LANES, S5_GROUP, gb * S5_STATE, S5_STATE), w, zero)
        return w.transpose(1, 0, 2, 3).reshape(kb, L * LANES, gb * S5_STATE)
    p_big = jnp.concatenate([state_in(wr), state_in(wi)], axis=2)

    cre_t, cim_t = cre.transpose(0, 2, 1), cim.transpose(0, 2, 1)
    def state_out(m):
        m = spread_channels(m.reshape(L, kb, gb * S5_STATE, S5_GROUP))
        m = jnp.where(group_mask(gb * S5_STATE, S5_STATE, LANES, S5_GROUP), m, zero)
        return m.transpose(1, 2, 0, 3).reshape(kb, gb * S5_STATE, L * LANES)
    q_re = cre_t[None] * pr[1:, :, :, None] - cim_t[None] * pi[1:, :, :, None]
    q_im = -(cre_t[None] * pi[1:, :, :, None] + cim_t[None] * pr[1:, :, :, None])
    q_big = jnp.concatenate([state_out(q_re), state_out(q_im)], axis=1)
    a_big = jnp.stack([pr[L].reshape(kb, gb * S5_STATE), pi[L].reshape(kb, gb * S5_STATE)], axis=1)
    return k_blk, p_big, q_big, a_big


def _s5_core(h2, bsz, seq, a_re, a_im, log_step, b_re, b_im, c_re, c_im):
    L = S5_CHUNK
    n_chunks = seq // L
    rows = bsz * n_chunks
    kb = D_MODEL // LANES
    rb = min(S5_ROWS, rows)
    assert rb % n_chunks == 0
    k_blk, p_big, q_big, a_big = _s5_operators(a_re, a_im, log_step, b_re, b_im, c_re, c_im)
    n_state = 8 * S5_STATE
    x_specs = [pl.BlockSpec((rb, LANES), lambda k, r, t=t: (r, t * kb + k)) for t in range(L)]
    return pl.pallas_call(
        functools.partial(_s5_kernel, n_chunks),
        grid=(kb, rows // rb),
        in_specs=x_specs + [
            pl.BlockSpec((None, L, LANES, LANES), lambda k, r: (k, 0, 0, 0)),
            pl.BlockSpec((None, L * LANES, 2 * n_state), lambda k, r: (k, 0, 0)),
            pl.BlockSpec((None, 2 * n_state, L * LANES), lambda k, r: (k, 0, 0)),
            pl.BlockSpec((None, 2, n_state), lambda k, r: (k, 0, 0))],
        out_specs=pl.BlockSpec((L, rb, LANES), lambda k, r: (0, r, k)),
        out_shape=jax.ShapeDtypeStruct((L, rows, D_MODEL), F32),
        scratch_shapes=[pltpu.VMEM((rb, 2 * n_state), F32), pltpu.VMEM((rb, 2 * n_state), F32),
                        pltpu.VMEM((L * LANES, L * LANES), BF16)],
        compiler_params=_params(2),
        name="s5_core",
    )(*([h2] * L), k_blk, p_big, q_big, a_big)


def _glu_kernel(x_ref, y_ref, g_ref, d_ref, wa_ref, wb_ref, o_ref, stage_ref):
    n_rows = y_ref.shape[1]
    n_blk = D_MODEL // LANES
    for k in range(n_blk):
        stage_ref[k] = x_ref[:, k * LANES:(k + 1) * LANES]
    x = jnp.concatenate(
        [jnp.concatenate([stage_ref[k, pl.ds(t, n_rows, stride=S5_CHUNK), :]
                          for k in range(n_blk)], axis=1) for t in range(S5_CHUNK)], axis=0)
    y = y_ref[...].reshape(x.shape) + d_ref[...] * _rms(x, g_ref[...])
    y = jax.nn.gelu(y).astype(BF16)
    res = x + _dot(y, wa_ref[...]) * jax.nn.sigmoid(_dot(y, wb_ref[...]))
    for t in range(S5_CHUNK):
        for k in range(n_blk):
            stage_ref[k, pl.ds(t, n_rows, stride=S5_CHUNK), :] = res[
                t * n_rows:(t + 1) * n_rows, k * LANES:(k + 1) * LANES]
    for k in range(n_blk):
        o_ref[:, k * LANES:(k + 1) * LANES] = stage_ref[k]


def _s5_layer(x, h2, bsz, seq, mix_g, a_re, a_im, log_step, b_re, b_im, c_re, c_im, d_skip,
              w_a, w_b):
    t = x.shape[0]
    tm = min(TOKEN_TILE, t)
    y = _s5_core(h2, bsz, seq, a_re, a_im, log_step, b_re, b_im, c_re, c_im)
    row = pl.BlockSpec((tm, D_MODEL), lambda i: (i, 0))
    return pl.pallas_call(
        _glu_kernel,
        grid=(t // tm,),
        in_specs=[row, pl.BlockSpec((S5_CHUNK, tm // S5_CHUNK, D_MODEL), lambda i: (0, i, 0)),
                  _const_spec((1, D_MODEL)), _const_spec((1, D_MODEL)),
                  _const_spec((D_MODEL, D_MODEL)), _const_spec((D_MODEL, D_MODEL))],
        out_specs=row,
        out_shape=jax.ShapeDtypeStruct((t, D_MODEL), F32),
        scratch_shapes=[pltpu.VMEM((D_MODEL // LANES, tm, LANES), F32)],
        compiler_params=_params(1),
        name="s5_glu",
    )(x, y, mix_g.astype(F32).reshape(1, D_MODEL), d_skip.astype(F32).reshape(1, D_MODEL),
      w_a.astype(BF16), w_b.astype(BF16))


def kernel(x, ffn1_norm, ffn1_gate, ffn1_up, ffn1_down, mix_norm, ffn2_norm, ffn2_gate, ffn2_up, ffn2_down, attn_w_in, attn_w_out, fg_bias, diff_q_norm, diff_k_norm, diff_lambda_q1, diff_lambda_k1, diff_lambda_q2, diff_lambda_k2, diff_subln, fox_q_norm, fox_k_norm, rel_bias, s5_a_re, s5_a_im, s5_log_step, s5_b_re, s5_b_im, s5_c_re, s5_c_im, s5_d, s5_glu_a, s5_glu_b):
    bsz, seq, _ = x.shape
    depth = ffn1_norm.shape[0]
    assert min(ATTN_Q, seq) >= REL_MAX_DIST, "bias tiles assume the bias saturates within a block"
    bias_tiles = _diff_bias_tiles(rel_bias, min(ATTN_Q, seq))
    xt = x.reshape(bsz * seq, D_MODEL).astype(F32)
    for i in range(depth):
        xt, h = _ffn(xt, ffn1_norm[i], ffn1_gate[i].astype(BF16), ffn1_up[i].astype(BF16),
                     ffn1_down[i].astype(BF16), g_next=mix_norm[i], chunked=i % 2 == 1)
        j = i // 2
        if i % 2 == 0:
            xt = _attn_layer(xt, h, bsz, seq, i, attn_w_in[j], attn_w_out[j], fg_bias[j],
                             diff_q_norm[j], diff_k_norm[j], diff_lambda_q1[j], diff_lambda_k1[j],
                             diff_lambda_q2[j], diff_lambda_k2[j], diff_subln[j], fox_q_norm[j],
                             fox_k_norm[j], bias_tiles)
        else:
            xt = _s5_layer(xt, h, bsz, seq, mix_norm[i], s5_a_re[j], s5_a_im[j], s5_log_step[j],
                           s5_b_re[j], s5_b_im[j], s5_c_re[j], s5_c_im[j], s5_d[j],
                           s5_glu_a[j], s5_glu_b[j])
        xt, _ = _ffn(xt, ffn2_norm[i], ffn2_gate[i].astype(BF16), ffn2_up[i].astype(BF16),
                     ffn2_down[i].astype(BF16))
    return xt.reshape(bsz, seq, D_MODEL).astype(x.dtype)
```

```python
import functools
import math

import jax
import jax.numpy as jnp
import numpy as np
from jax import lax
from jax.experimental import pallas as pl
from jax.experimental.pallas import tpu as pltpu

D_MODEL = 1024
D_FF = 2816
HEAD_DIM = 64
N_DIFF_HEADS = 4
N_FOX_HEADS = 8
N_REL_BUCKETS = 32
REL_MAX_DIST = 128
S5_GROUP = 16
S5_GROUPS = D_MODEL // S5_GROUP
S5_STATE = 64
NORM_EPS = 1e-6

QK_COLS = 2048
V_COLS = 1024
AUG_COLS = 256
LANES = 128
MXU_DIM = 256
VMEM_LIMIT = 56 * 1024 * 1024

TOKEN_TILE = 512
FFN_TILE = 1024
FFN_SPLITS = (0, 6 * MXU_DIM, D_FF)
ATTN_Q = 512
ATTN_LANES = 256
ATTN_GROUP = 3
S5_CHUNK = 16
S5_ROWS = 512
NEG_BIG = -1e30
LOG2E = 1.4426950408889634

BF16 = jnp.bfloat16
F32 = jnp.float32


def _dot(a, b):
    return jnp.dot(a, b, preferred_element_type=F32)


def _dot_nt(a, b):
    return lax.dot_general(a, b, (((1,), (1,)), ((), ())), preferred_element_type=F32)


def _rms(x, g):
    ms = jnp.mean(x * x, axis=-1, keepdims=True)
    return x * lax.rsqrt(ms + NORM_EPS) * g


def _const_spec(shape):
    return pl.BlockSpec(shape, lambda *_: (0,) * len(shape), pipeline_mode=pl.Buffered(1))


def _params(n_axes):
    return pltpu.CompilerParams(dimension_semantics=("arbitrary",) * n_axes,
                                vmem_limit_bytes=VMEM_LIMIT)


def _split3(x):
    p0 = x.astype(BF16)
    r1 = x - p0.astype(F32)
    p1 = r1.astype(BF16)
    p2 = (r1 - p1.astype(F32)).astype(BF16)
    return p0, p1, p2


def _ffn_kernel(chunked, x_ref, g_ref, gn_ref, wg_ref, wu_ref, wd_ref, o_ref, *rest):
    x = x_ref[...]
    h = _rms(x, g_ref[...]).astype(BF16)
    acc = None
    for lo, hi in zip(FFN_SPLITS[:-1], FFN_SPLITS[1:]):
        sl = slice(lo, hi)
        a = _dot(h, wg_ref[:, sl])
        b = _dot(h, wu_ref[:, sl])
        act = (a * jax.nn.sigmoid(a) * b).astype(BF16)
        d = _dot(act, wd_ref[sl, :])
        acc = d if acc is None else acc + d
    xo = x + 0.5 * acc
    o_ref[...] = xo
    if not rest:
        return
    hn = _rms(xo, gn_ref[...])
    if not chunked:
        rest[0][...] = hn.astype(BF16)
        return
    h_ref, stage_ref = rest
    n_rows = h_ref.shape[0]
    for k in range(D_MODEL // LANES):
        stage_ref[k] = hn[:, k * LANES:(k + 1) * LANES]
    for t in range(S5_CHUNK):
        for k in range(D_MODEL // LANES):
            lo = t * D_MODEL + k * LANES
            h_ref[:, lo:lo + LANES] = stage_ref[k, pl.ds(t, n_rows, stride=S5_CHUNK), :].astype(BF16)


def _ffn(x, g, wg, wu, wd, g_next=None, chunked=False):
    t = x.shape[0]
    tm = min(FFN_TILE, t)
    emit = g_next is not None
    gn = g_next if emit else g
    row = pl.BlockSpec((tm, D_MODEL), lambda i: (i, 0))
    out_shape = [jax.ShapeDtypeStruct((t, D_MODEL), F32)]
    out_specs = [row]
    scratch = []
    if emit and chunked:
        out_shape.append(jax.ShapeDtypeStruct((t // S5_CHUNK, S5_CHUNK * D_MODEL), BF16))
        out_specs.append(pl.BlockSpec((tm // S5_CHUNK, S5_CHUNK * D_MODEL), lambda i: (i, 0)))
        scratch = [pltpu.VMEM((D_MODEL // LANES, tm, LANES), F32)]
    elif emit:
        out_shape.append(jax.ShapeDtypeStruct((t, D_MODEL), BF16))
        out_specs.append(row)
    res = pl.pallas_call(
        functools.partial(_ffn_kernel, chunked),
        grid=(t // tm,),
        in_specs=[row, _const_spec((1, D_MODEL)), _const_spec((1, D_MODEL)),
                  _const_spec((D_MODEL, D_FF)), _const_spec((D_MODEL, D_FF)),
                  _const_spec((D_FF, D_MODEL))],
        out_specs=out_specs,
        out_shape=out_shape,
        scratch_shapes=scratch,
        compiler_params=_params(1),
        name="ffn",
    )(x, g.reshape(1, D_MODEL), gn.reshape(1, D_MODEL), wg, wu, wd)
    return (res[0], res[1]) if emit else (res[0], None)


def _inproj_kernel(tiles_per_seq, h_ref, wqk_ref, wv_ref, wf_ref, gains_ref, ones_ref, tri_ref,
                   fgb_ref, sel_ref, aug1_ref, qk_ref, vt_ref, aug_ref, carry_ref):
    i = pl.program_id(0)
    h = h_ref[...]
    ones_blk = ones_ref[...]
    for c in range(QK_COLS // 512):
        y = _dot(h, wqk_ref[:, c * 512:(c + 1) * 512])
        y2 = y * y
        hi = y2.astype(BF16)
        lo = (y2 - hi.astype(F32)).astype(BF16)
        ssq = jnp.concatenate(
            [_dot(hi[:, k * MXU_DIM:(k + 1) * MXU_DIM], ones_blk)
             + _dot(lo[:, k * MXU_DIM:(k + 1) * MXU_DIM], ones_blk)
             for k in range(512 // MXU_DIM)], axis=1)
        y = y * lax.rsqrt(ssq * (1.0 / HEAD_DIM) + NORM_EPS) * gains_ref[c:c + 1, :]
        qk_ref[:, c * 512:(c + 1) * 512] = y.astype(BF16)
    for c in range(V_COLS // 512):
        vt = _dot_nt(wv_ref[c * 512:(c + 1) * 512, :], h)
        vt_ref[c * 512:(c + 1) * 512, :] = vt.astype(BF16)

    fl = _dot(h, wf_ref[...]) + fgb_ref[...]
    logf = -(jnp.maximum(-fl, 0.0) + jnp.log1p(jnp.exp(-jnp.abs(fl))))

    @pl.when(i % tiles_per_seq == 0)
    def _():
        carry_ref[...] = jnp.zeros_like(carry_ref)

    tri = tri_ref[...]
    cum = sum(_dot(tri, p) for p in _split3(logf)) + carry_ref[...]
    carry_ref[...] = cum[cum.shape[0] - 1:, :]
    aug = sum(_dot(p, sel_ref[n]) for n, p in enumerate(_split3(cum * LOG2E))) + aug1_ref[...]
    aug_ref[...] = aug.astype(BF16)


def _decay_lane_maps():
    sel = np.zeros((3, LANES, AUG_COLS), np.float32)
    ones = np.zeros((1, AUG_COLS), np.float32)
    k_off = AUG_COLS // 2
    for head in range(N_FOX_HEADS):
        base = head * 6
        for n in range(3):
            sel[n, head, base + n] = 1.0
            sel[n, head, k_off + base + 3 + n] = -1.0
            ones[0, base + 3 + n] = 1.0
            ones[0, k_off + base + n] = 1.0
    return jnp.asarray(sel, BF16), jnp.asarray(ones, F32)


def _inproj(h, w_qk, w_v, w_f, gains, fg_bias_row, bsz, seq):
    t = h.shape[0]
    tm = min(TOKEN_TILE, seq)
    tps = seq // tm
    ones_blk = jnp.kron(jnp.eye(MXU_DIM // HEAD_DIM, dtype=F32),
                        jnp.ones((HEAD_DIM, HEAD_DIM), F32)).astype(BF16)
    tri = jnp.tril(jnp.ones((tm, tm), F32)).astype(BF16)
    sel, aug_ones = _decay_lane_maps()
    return pl.pallas_call(
        functools.partial(_inproj_kernel, tps),
        grid=(t // tm,),
        in_specs=[pl.BlockSpec((tm, D_MODEL), lambda i: (i, 0)),
                  _const_spec((D_MODEL, QK_COLS)), _const_spec((D_MODEL, V_COLS)),
                  _const_spec((D_MODEL, LANES)),
                  _const_spec((QK_COLS // 512, 512)), _const_spec((MXU_DIM, MXU_DIM)),
                  _const_spec((tm, tm)), _const_spec((1, LANES)),
                  _const_spec((3, LANES, AUG_COLS)), _const_spec((1, AUG_COLS))],
        out_specs=[pl.BlockSpec((tm, QK_COLS), lambda i: (i, 0)),
                   pl.BlockSpec((None, V_COLS, tm), lambda i: (i // tps, 0, i % tps)),
                   pl.BlockSpec((tm, AUG_COLS), lambda i: (i, 0))],
        out_shape=[jax.ShapeDtypeStruct((t, QK_COLS), BF16),
                   jax.ShapeDtypeStruct((bsz, V_COLS, seq), BF16),
                   jax.ShapeDtypeStruct((t, AUG_COLS), BF16)],
        scratch_shapes=[pltpu.VMEM((1, LANES), F32)],
        compiler_params=_params(1),
        name="attn_inproj",
    )(h, w_qk, w_v, w_f, gains, ones_blk, tri, fg_bias_row, sel, aug_ones)


def _attn_sweep(qi, q_both, k_ref, kaug_ref, vt_ref, bias_ref, m_ref, l_ref, acc_ref):
    blk = q_both.shape[0] // 2
    m_ref[...] = jnp.full_like(m_ref, NEG_BIG)
    l_ref[...] = jnp.zeros_like(l_ref)
    acc_ref[...] = jnp.zeros_like(acc_ref)

    def run(items):
        starts = [pl.multiple_of(j * blk, blk) for j, _, _ in items]
        lanes = [slice(c * ATTN_LANES, (c + 1) * ATTN_LANES)
                 for c in range(2 * blk // ATTN_LANES)]

        def n_keys(n, ls):
            first_half = (ls.start % blk) + ATTN_LANES <= blk // 2
            return blk // 2 if items[n][2] and first_half else blk

        scores = {}
        for n, start in enumerate(starts):
            k_blk = k_ref[pl.ds(start, blk), :]
            if kaug_ref is not None:
                k_blk = jnp.concatenate([k_blk, kaug_ref[pl.ds(start, blk), :]], axis=1)
            for c, ls in enumerate(lanes):
                scores[n, c] = _dot_nt(k_blk[:n_keys(n, ls)], q_both[ls, :])
        for n, (_, bias_idx, masked) in enumerate(items):
            for c, ls in enumerate(lanes):
                s = scores.pop((n, c))
                keys = s.shape[0]
                if bias_idx is not None and bias_ref is not None:
                    s = bias_ref[bias_idx, :keys, ls] + s
                if masked:
                    key = lax.broadcasted_iota(jnp.int32, s.shape, 0)
                    qry = (lax.broadcasted_iota(jnp.int32, s.shape, 1) + ls.start) & (blk - 1)
                    s = jnp.where(key <= qry, s, NEG_BIG)
                m_prev = m_ref[:, ls]
                m_new = jnp.maximum(m_prev, jnp.max(s, axis=0, keepdims=True))
                alpha = jnp.exp2(m_prev - m_new)
                p = jnp.exp2(s - m_new)
                l_ref[:, ls] = alpha * l_ref[:, ls] + jnp.sum(p, axis=0, keepdims=True)
                m_ref[:, ls] = m_new
                acc_ref[:, ls] = alpha * acc_ref[:, ls] + _dot(
                    vt_ref[:, pl.ds(starts[n], keys)], p.astype(BF16))

    n_far = jnp.maximum(qi - 1, 0)

    def far_group(j, carry):
        run([(ATTN_GROUP * j + n, None, False) for n in range(ATTN_GROUP)])
        return carry

    lax.fori_loop(0, n_far // ATTN_GROUP, far_group, 0)
    prev, diag = (qi - 1, 1, False), (qi, 0, True)

    @pl.when(qi == 0)
    def _():
        run([diag])

    for left in range(ATTN_GROUP):
        @pl.when((qi >= 1) & (n_far % ATTN_GROUP == left))
        def _():
            run([(n_far - left + n, None, False) for n in range(left)] + [prev, diag])

    halves = (slice(None), slice(0, blk)), (slice(None), slice(blk, 2 * blk))
    return [acc_ref[sl] / l_ref[sl] for sl in halves]


def _diff_attn_kernel(scal_ref, q_ref, k_ref, vt_ref, bias_ref, subln_ref, o_ref,
                      m_ref, l_ref, acc_ref):
    blk = bias_ref.shape[1]
    lam = scal_ref[0]
    out_scale = scal_ref[1]

    def query_block(qi, carry):
        rows = pl.ds(pl.multiple_of(qi * blk, blk), blk)
        q = q_ref[rows, :]
        lane = lax.broadcasted_iota(jnp.int32, q.shape, 1)
        zero = jnp.zeros_like(q)
        q_both = jnp.concatenate([jnp.where(lane < HEAD_DIM, q, zero),
                                  jnp.where(lane < HEAD_DIM, zero, q)], axis=0)
        o1, o2 = _attn_sweep(qi, q_both, k_ref, None, vt_ref, bias_ref, m_ref, l_ref, acc_ref)
        o = o1 - lam * o2
        ms = jnp.mean(o * o, axis=0, keepdims=True)
        o = o * lax.rsqrt(ms + NORM_EPS) * subln_ref[...] * out_scale
        o_ref[rows, :] = o.T.astype(o_ref.dtype)
        return carry

    lax.fori_loop(0, q_ref.shape[0] // blk, query_block, 0)


def _fox_attn_kernel(blk, q_ref, qaug_ref, k_ref, kaug_ref, vt_ref, o_ref, m_ref, l_ref, acc_ref):
    decay = LANES + 12 * pl.program_id(1)

    def query_block(qi, carry):
        rows = pl.ds(pl.multiple_of(qi * blk, blk), blk)
        q = jnp.concatenate([q_ref[rows, :], qaug_ref[rows, :]], axis=1)
        lane = lax.broadcasted_iota(jnp.int32, q.shape, 1)
        zero = jnp.zeros_like(q)
        in_a = (lane < HEAD_DIM) | ((lane >= decay) & (lane < decay + 6))
        in_b = (((lane >= HEAD_DIM) & (lane < LANES))
                | ((lane >= decay + 6) & (lane < decay + 12)))
        q_both = jnp.concatenate([jnp.where(in_a, q, zero), jnp.where(in_b, q, zero)], axis=0)
        o1, o2 = _attn_sweep(qi, q_both, k_ref, kaug_ref, vt_ref, None, m_ref, l_ref, acc_ref)
        row = lax.broadcasted_iota(jnp.int32, o1.shape, 0)
        o_ref[rows, :] = jnp.where(row < HEAD_DIM, o1, o2).T.astype(o_ref.dtype)
        return carry

    lax.fori_loop(0, q_ref.shape[0] // blk, query_block, 0)


def _attn_scratch(tq):
    return [pltpu.VMEM((1, 2 * tq), F32), pltpu.VMEM((1, 2 * tq), F32),
            pltpu.VMEM((LANES, 2 * tq), F32)]


def _seq_block(col):
    return lambda seq: pl.BlockSpec((None, seq, LANES), lambda b, s: (b, 0, col(s)))


def _diff_attn(qk, vt, bias_tiles, scal, subln_col):
    bsz, seq, _ = qk.shape
    tq = min(ATTN_Q, seq)
    return pl.pallas_call(
        _diff_attn_kernel,
        grid=(bsz, N_DIFF_HEADS),
        in_specs=[pl.BlockSpec(memory_space=pltpu.SMEM),
                  _seq_block(lambda s: s)(seq), _seq_block(lambda s: 4 + s)(seq),
                  pl.BlockSpec((None, LANES, seq), lambda b, s: (b, s, 0)),
                  pl.BlockSpec((None, 2, tq, 2 * tq), lambda b, s: (s, 0, 0, 0)),
                  _const_spec((LANES, 1))],
        out_specs=_seq_block(lambda s: s)(seq),
        out_shape=jax.ShapeDtypeStruct((bsz, seq, N_DIFF_HEADS * LANES), BF16),
        scratch_shapes=_attn_scratch(tq),
        compiler_params=_params(2),
        name="diff_attn",
    )(scal, qk, qk, vt, bias_tiles, subln_col)


def _fox_attn(qk, aug, vt):
    bsz, seq, _ = qk.shape
    tq = min(ATTN_Q, seq)
    return pl.pallas_call(
        functools.partial(_fox_attn_kernel, tq),
        grid=(bsz, N_FOX_HEADS // 2),
        in_specs=[_seq_block(lambda s: 8 + s)(seq), _seq_block(lambda s: 0)(seq),
                  _seq_block(lambda s: 12 + s)(seq), _seq_block(lambda s: 1)(seq),
                  pl.BlockSpec((None, LANES, seq), lambda b, s: (b, 4 + s, 0))],
        out_specs=_seq_block(lambda s: s)(seq),
        out_shape=jax.ShapeDtypeStruct((bsz, seq, N_FOX_HEADS * HEAD_DIM), BF16),
        scratch_shapes=_attn_scratch(tq),
        compiler_params=_params(2),
        name="fox_attn",
    )(qk, aug, qk, aug, vt)


def _outproj_kernel(x_ref, od_ref, of_ref, wd_ref, wf_ref, o_ref):
    o_ref[...] = x_ref[...] + _dot(od_ref[...], wd_ref[...]) + _dot(of_ref[...], wf_ref[...])


def _outproj(x, od, of, w_d, w_f):
    t = x.shape[0]
    tm = min(TOKEN_TILE, t)
    half = od.shape[1]
    return pl.pallas_call(
        _outproj_kernel,
        grid=(t // tm,),
        in_specs=[pl.BlockSpec((tm, D_MODEL), lambda i: (i, 0)),
                  pl.BlockSpec((tm, half), lambda i: (i, 0)),
                  pl.BlockSpec((tm, half), lambda i: (i, 0)),
                  _const_spec((half, D_MODEL)), _const_spec((half, D_MODEL))],
        out_specs=pl.BlockSpec((tm, D_MODEL), lambda i: (i, 0)),
        out_shape=jax.ShapeDtypeStruct((t, D_MODEL), F32),
        compiler_params=_params(1),
        name="attn_outproj",
    )(x, od, of, w_d, w_f)


def _rel_bias_by_distance(rel_bias, n_dist):
    n = jnp.arange(n_dist, dtype=jnp.int32)
    max_exact = N_REL_BUCKETS // 2
    nf = jnp.maximum(n, 1).astype(F32)
    large = max_exact + (jnp.log(nf / max_exact) / math.log(REL_MAX_DIST / max_exact)
                         * (N_REL_BUCKETS - max_exact)).astype(jnp.int32)
    large = jnp.minimum(large, N_REL_BUCKETS - 1)
    return rel_bias[jnp.where(n < max_exact, n, large)]


def _toeplitz(w, n):
    heads, period = w.shape
    flat = jnp.tile(w, (1, n))[:, :n * (period - 1)]
    return flat.reshape(heads, n, period - 1)[:, :, :n]


def _diff_bias_tiles(rel_bias, tq):
    by_dist = _rel_bias_by_distance(rel_bias.astype(F32), 2 * tq)
    by_dist = ((by_dist - by_dist[2 * tq - 1]) * LOG2E).T
    diag = _toeplitz(by_dist, tq)
    prev = _toeplitz(jnp.roll(by_dist, -tq, axis=1), tq)
    tiles = jnp.stack([diag, prev], axis=1)
    return jnp.concatenate([tiles, tiles], axis=-1)


def _attn_layer(x, h, bsz, seq, layer_idx, w_in, w_out, fg_bias, dq_g, dk_g, lq1, lk1, lq2, lk2,
                subln_g, fq_g, fk_g, bias_tiles):
    tq = min(ATTN_Q, seq)
    q_scale = HEAD_DIM ** -0.5 * LOG2E
    w_bf = w_in.astype(BF16)
    w_qk = jnp.concatenate([w_bf[:, 0:1024], w_bf[:, 1536:2560]], axis=1)
    w_v = jnp.concatenate([w_bf[:, 1024:1536], w_bf[:, 2560:3072]], axis=1).T
    w_f = jnp.pad(w_bf[:, 3072:], ((0, 0), (0, LANES - N_FOX_HEADS)))
    fgb = jnp.pad(fg_bias.astype(F32), (0, LANES - N_FOX_HEADS)).reshape(1, LANES)
    gains = jnp.stack([jnp.tile(dq_g, 8) * q_scale, jnp.tile(dk_g, 8),
                       jnp.tile(fq_g, 8) * q_scale, jnp.tile(fk_g, 8)]).astype(F32)
    qk, vt, aug = _inproj(h, w_qk, w_v, w_f, gains, fgb, bsz, seq)
    qk = qk.reshape(bsz, seq, QK_COLS)

    lam_init = 0.8 - 0.6 * math.exp(-0.3 * layer_idx)
    lam = (jnp.exp(jnp.sum(lq1.astype(F32) * lk1.astype(F32)))
           - jnp.exp(jnp.sum(lq2.astype(F32) * lk2.astype(F32))) + lam_init)
    scal = jnp.stack([lam, jnp.asarray(1.0 - lam_init, F32)]).astype(F32)
    od = _diff_attn(qk, vt, bias_tiles, scal, subln_g.astype(F32).reshape(LANES, 1))

    of = _fox_attn(qk, aug.reshape(bsz, seq, AUG_COLS), vt)

    w_o = w_out.astype(BF16)
    half = N_DIFF_HEADS * LANES
    return _outproj(x, od.reshape(bsz * seq, half), of.reshape(bsz * seq, half),
                    w_o[:half], w_o[half:])


def _s5_kernel(n_chunks, *refs):
    x_refs = refs[:S5_CHUNK]
    kblk_ref, p_ref, q_ref, a_ref, y_ref, z_ref, xp_ref, t_ref = refs[S5_CHUNK:]

    @pl.when(pl.program_id(1) == 0)
    def _():
        t_ref[...] = jnp.zeros_like(t_ref)
        for s in range(S5_CHUNK):
            for t in range(s, S5_CHUNK):
                t_ref[s * LANES:(s + 1) * LANES, t * LANES:(t + 1) * LANES] = kblk_ref[t - s]

    x = jnp.concatenate([r[...] for r in x_refs], axis=1)
    n_rows = x.shape[0]
    n_seq = n_rows // n_chunks
    half = z_ref.shape[1] // 2
    z_ref[...] = _dot(x, p_ref[...])
    a_re = a_ref[0:1, :]
    a_im = a_ref[1:2, :]

    def scan(c, carry):
        out = []
        for b in range(n_seq):
            x_re, x_im = carry[2 * b], carry[2 * b + 1]
            row = pl.ds(b * n_chunks + c, 1)
            xp_ref[row, :half] = x_re
            xp_ref[row, half:] = x_im
            z = z_ref[row, :]
            out += [a_re * x_re - a_im * x_im + z[:, :half],
                    a_re * x_im + a_im * x_re + z[:, half:]]
        return tuple(out)

    zero = jnp.zeros((1, half), F32)
    lax.fori_loop(0, n_chunks, scan, (zero,) * (2 * n_seq))
    inter = _dot(xp_ref[...].astype(BF16), q_ref[...])
    for j in range(S5_CHUNK // 2):
        cols = slice(2 * j * LANES, (2 * j + 2) * LANES)
        y = _dot(x[:, :cols.stop], t_ref[:cols.stop, cols]) + inter[:, cols]
        y_ref[pl.ds(2 * j, n_rows, stride=S5_CHUNK), :] = y[:, :LANES]
        y_ref[pl.ds(2 * j + 1, n_rows, stride=S5_CHUNK), :] = y[:, LANES:]


def _s5_operators(a_re, a_im, log_step, b_re, b_im, c_re, c_im):
    hp = lax.Precision.HIGHEST
    L, kb, gb = S5_CHUNK, S5_GROUPS // 8, 8
    step = jnp.exp(log_step.astype(F32))[:, None]
    ar, ai = a_re.astype(F32), a_im.astype(F32)
    mag = jnp.exp(ar * step)
    lr, li = mag * jnp.cos(ai * step), mag * jnp.sin(ai * step)
    den = ar * ar + ai * ai
    cr = (((lr - 1.0) * ar + li * ai) / den)[:, None, :]
    ci = ((li * ar - (lr - 1.0) * ai) / den)[:, None, :]
    br, bi = b_re.astype(F32).transpose(0, 2, 1), b_im.astype(F32).transpose(0, 2, 1)
    bbr, bbi = cr * br - ci * bi, cr * bi + ci * br
    pr, pi = [jnp.ones_like(lr)], [jnp.zeros_like(li)]
    for _ in range(L):
        pr, pi = pr + [pr[-1] * lr - pi[-1] * li], pi + [pr[-1] * li + pi[-1] * lr]
    pr, pi = jnp.stack(pr), jnp.stack(pi)
    wr = pr[:L, :, None, :] * bbr - pi[:L, :, None, :] * bbi
    wi = pr[:L, :, None, :] * bbi + pi[:L, :, None, :] * bbr
    cre, cim = c_re.astype(F32), c_im.astype(F32)
    kern = (jnp.einsum('gcp,ngap->ngac', cre, wr, precision=hp)
            - jnp.einsum('gcp,ngap->ngac', cim, wi, precision=hp))

    def group_mask(rows, rows_per_group, cols, cols_per_group):
        r = lax.broadcasted_iota(jnp.int32, (rows, 1), 0) // rows_per_group
        c = lax.broadcasted_iota(jnp.int32, (1, cols), 1) // cols_per_group
        return r == c

    def spread_channels(m):
        src = lax.broadcasted_iota(jnp.int32, (S5_GROUP, LANES), 0)
        dst = lax.broadcasted_iota(jnp.int32, (S5_GROUP, LANES), 1)
        return jnp.dot(m.astype(BF16), (src == dst % S5_GROUP).astype(BF16),
                       preferred_element_type=BF16)

    zero = jnp.zeros((), BF16)
    k_blk = kern.reshape(L, kb, LANES, S5_GROUP).transpose(1, 0, 2, 3)
    k_blk = jnp.where(group_mask(LANES, S5_GROUP, LANES, S5_GROUP), spread_channels(k_blk), zero)

    def state_in(w):
        w = jnp.tile(w[::-1].astype(BF16).reshape(L, kb, LANES, S5_STATE), (1, 1, 1, gb))
        w = jnp.where(group_mask(LANES, S5_GROUP, gb * S5_STATE, S5_STATE), w, zero)
        return w.transpose(1, 0, 2, 3).reshape(kb, L * LANES, gb * S5_STATE)
    p_big = jnp.concatenate([state_in(wr), state_in(wi)], axis=2)

    cre_t, cim_t = cre.transpose(0, 2, 1), cim.transpose(0, 2, 1)
    def state_out(m):
        m = spread_channels(m.reshape(L, kb, gb * S5_STATE, S5_GROUP))
        m = jnp.where(group_mask(gb * S5_STATE, S5_STATE, LANES, S5_GROUP), m, zero)
        return m.transpose(1, 2, 0, 3).reshape(kb, gb * S5_STATE, L * LANES)
    q_re = cre_t[None] * pr[1:, :, :, None] - cim_t[None] * pi[1:, :, :, None]
    q_im = -(cre_t[None] * pi[1:, :, :, None] + cim_t[None] * pr[1:, :, :, None])
    q_big = jnp.concatenate([state_out(q_re), state_out(q_im)], axis=1)
    a_big = jnp.stack([pr[L].reshape(kb, gb * S5_STATE), pi[L].reshape(kb, gb * S5_STATE)], axis=1)
    return k_blk, p_big, q_big, a_big


def _s5_core(h2, bsz, seq, a_re, a_im, log_step, b_re, b_im, c_re, c_im):
    L = S5_CHUNK
    n_chunks = seq // L
    rows = bsz * n_chunks
    kb = D_MODEL // LANES
    rb = min(S5_ROWS, rows)
    assert rb % n_chunks == 0
    k_blk, p_big, q_big, a_big = _s5_operators(a_re, a_im, log_step, b_re, b_im, c_re, c_im)
    n_state = 8 * S5_STATE
    x_specs = [pl.BlockSpec((rb, LANES), lambda k, r, t=t: (r, t * kb + k)) for t in range(L)]
    return pl.pallas_call(
        functools.partial(_s5_kernel, n_chunks),
        grid=(kb, rows // rb),
        in_specs=x_specs + [
            pl.BlockSpec((None, L, LANES, LANES), lambda k, r: (k, 0, 0, 0)),
            pl.BlockSpec((None, L * LANES, 2 * n_state), lambda k, r: (k, 0, 0)),
            pl.BlockSpec((None, 2 * n_state, L * LANES), lambda k, r: (k, 0, 0)),
            pl.BlockSpec((None, 2, n_state), lambda k, r: (k, 0, 0))],
        out_specs=pl.BlockSpec((rb * L, LANES), lambda k, r: (r, k)),
        out_shape=jax.ShapeDtypeStruct((rows * L, D_MODEL), F32),
        scratch_shapes=[pltpu.VMEM((rb, 2 * n_state), F32), pltpu.VMEM((rb, 2 * n_state), F32),
                        pltpu.VMEM((L * LANES, L * LANES), BF16)],
        compiler_params=_params(2),
        name="s5_core",
    )(*([h2] * L), k_blk, p_big, q_big, a_big)


def _glu_kernel(x_ref, y_ref, g_ref, d_ref, wa_ref, wb_ref, o_ref):
    x = x_ref[...]
    y = y_ref[...] + d_ref[...] * _rms(x, g_ref[...])
    y = jax.nn.gelu(y).astype(BF16)
    o_ref[...] = x + _dot(y, wa_ref[...]) * jax.nn.sigmoid(_dot(y, wb_ref[...]))


def _s5_layer(x, h2, bsz, seq, mix_g, a_re, a_im, log_step, b_re, b_im, c_re, c_im, d_skip,
              w_a, w_b):
    t = x.shape[0]
    tm = min(TOKEN_TILE, t)
    y = _s5_core(h2, bsz, seq, a_re, a_im, log_step, b_re, b_im, c_re, c_im)
    row = pl.BlockSpec((tm, D_MODEL), lambda i: (i, 0))
    return pl.pallas_call(
        _glu_kernel,
        grid=(t // tm,),
        in_specs=[row, row, _const_spec((1, D_MODEL)), _const_spec((1, D_MODEL)),
                  _const_spec((D_MODEL, D_MODEL)), _const_spec((D_MODEL, D_MODEL))],
        out_specs=row,
        out_shape=jax.ShapeDtypeStruct((t, D_MODEL), F32),
        compiler_params=_params(1),
        name="s5_glu",
    )(x, y, mix_g.astype(F32).reshape(1, D_MODEL), d_skip.astype(F32).reshape(1, D_MODEL),
      w_a.astype(BF16), w_b.astype(BF16))


def kernel(x, ffn1_norm, ffn1_gate, ffn1_up, ffn1_down, mix_norm, ffn2_norm, ffn2_gate, ffn2_up, ffn2_down, attn_w_in, attn_w_out, fg_bias, diff_q_norm, diff_k_norm, diff_lambda_q1, diff_lambda_k1, diff_lambda_q2, diff_lambda_k2, diff_subln, fox_q_norm, fox_k_norm, rel_bias, s5_a_re, s5_a_im, s5_log_step, s5_b_re, s5_b_im, s5_c_re, s5_c_im, s5_d, s5_glu_a, s5_glu_b):
    bsz, seq, _ = x.shape
    depth = ffn1_norm.shape[0]
    assert min(ATTN_Q, seq) >= REL_MAX_DIST, "bias tiles assume the bias saturates within a block"
    bias_tiles = _diff_bias_tiles(rel_bias, min(ATTN_Q, seq))
    xt = x.reshape(bsz * seq, D_MODEL).astype(F32)
    for i in range(depth):
        xt, h = _ffn(xt, ffn1_norm[i], ffn1_gate[i].astype(BF16), ffn1_up[i].astype(BF16),
                     ffn1_down[i].astype(BF16), g_next=mix_norm[i], chunked=i % 2 == 1)
        j = i // 2
        if i % 2 == 0:
            xt = _attn_layer(xt, h, bsz, seq, i, attn_w_in[j], attn_w_out[j], fg_bias[j],
                             diff_q_norm[j], diff_k_norm[j], diff_lambda_q1[j], diff_lambda_k1[j],
                             diff_lambda_q2[j], diff_lambda_k2[j], diff_subln[j], fox_q_norm[j],
                             fox_k_norm[j], bias_tiles)
        else:
            xt = _s5_layer(xt, h, bsz, seq, mix_norm[i], s5_a_re[j], s5_a_im[j], s5_log_step[j],
                           s5_b_re[j], s5_b_im[j], s5_c_re[j], s5_c_im[j], s5_d[j],
                           s5_glu_a[j], s5_glu_b[j])
        xt, _ = _ffn(xt, ffn2_norm[i], ffn2_gate[i].astype(BF16), ffn2_up[i].astype(BF16),
                     ffn2_down[i].astype(BF16))
    return xt.reshape(bsz, seq, D_MODEL).astype(x.dtype)
```

```python
import functools
import math

import jax
import jax.numpy as jnp
import numpy as np
from jax import lax
from jax.experimental import pallas as pl
from jax.experimental.pallas import tpu as pltpu

D_MODEL = 1024
D_FF = 2816
HEAD_DIM = 64
N_DIFF_HEADS = 4
N_FOX_HEADS = 8
N_REL_BUCKETS = 32
REL_MAX_DIST = 128
S5_GROUP = 16
S5_GROUPS = D_MODEL // S5_GROUP
S5_STATE = 64
NORM_EPS = 1e-6

QK_COLS = 2048
V_COLS = 1024
AUG_COLS = 256
LANES = 128
MXU_DIM = 256
VMEM_LIMIT = 56 * 1024 * 1024

TOKEN_TILE = 512
FFN_TILE = 1024
FFN_SPLITS = (0, 6 * MXU_DIM, D_FF)
ATTN_Q = 512
ATTN_LANES = 256
ATTN_GROUP = 3
S5_CHUNK = 16
S5_ROWS = 512
NEG_BIG = -1e30
LOG2E = 1.4426950408889634

BF16 = jnp.bfloat16
F32 = jnp.float32


def _dot(a, b):
    return jnp.dot(a, b, preferred_element_type=F32)


def _dot_nt(a, b):
    return lax.dot_general(a, b, (((1,), (1,)), ((), ())), preferred_element_type=F32)


def _rms(x, g):
    ms = jnp.mean(x * x, axis=-1, keepdims=True)
    return x * lax.rsqrt(ms + NORM_EPS) * g


def _const_spec(shape):
    return pl.BlockSpec(shape, lambda *_: (0,) * len(shape), pipeline_mode=pl.Buffered(1))


def _params(n_axes):
    return pltpu.CompilerParams(dimension_semantics=("arbitrary",) * n_axes,
                                vmem_limit_bytes=VMEM_LIMIT)


def _split3(x):
    p0 = x.astype(BF16)
    r1 = x - p0.astype(F32)
    p1 = r1.astype(BF16)
    p2 = (r1 - p1.astype(F32)).astype(BF16)
    return p0, p1, p2


def _ffn_kernel(chunked, x_ref, g_ref, gn_ref, wg_ref, wu_ref, wd_ref, o_ref, *rest):
    x = x_ref[...]
    h = _rms(x, g_ref[...]).astype(BF16)
    acc = None
    for lo, hi in zip(FFN_SPLITS[:-1], FFN_SPLITS[1:]):
        sl = slice(lo, hi)
        a = _dot(h, wg_ref[:, sl])
        b = _dot(h, wu_ref[:, sl])
        act = (a * jax.nn.sigmoid(a) * b).astype(BF16)
        d = _dot(act, wd_ref[sl, :])
        acc = d if acc is None else acc + d
    xo = x + 0.5 * acc
    o_ref[...] = xo
    if not rest:
        return
    hn = _rms(xo, gn_ref[...])
    if not chunked:
        rest[0][...] = hn.astype(BF16)
        return
    h_ref, stage_ref = rest
    n_rows = h_ref.shape[0]
    for k in range(D_MODEL // LANES):
        stage_ref[k] = hn[:, k * LANES:(k + 1) * LANES]
    for t in range(S5_CHUNK):
        for k in range(D_MODEL // LANES):
            lo = t * D_MODEL + k * LANES
            h_ref[:, lo:lo + LANES] = stage_ref[k, pl.ds(t, n_rows, stride=S5_CHUNK), :].astype(BF16)


def _ffn(x, g, wg, wu, wd, g_next=None, chunked=False):
    t = x.shape[0]
    tm = min(FFN_TILE, t)
    emit = g_next is not None
    gn = g_next if emit else g
    row = pl.BlockSpec((tm, D_MODEL), lambda i: (i, 0))
    out_shape = [jax.ShapeDtypeStruct((t, D_MODEL), F32)]
    out_specs = [row]
    scratch = []
    if emit and chunked:
        out_shape.append(jax.ShapeDtypeStruct((t // S5_CHUNK, S5_CHUNK * D_MODEL), BF16))
        out_specs.append(pl.BlockSpec((tm // S5_CHUNK, S5_CHUNK * D_MODEL), lambda i: (i, 0)))
        scratch = [pltpu.VMEM((D_MODEL // LANES, tm, LANES), F32)]
    elif emit:
        out_shape.append(jax.ShapeDtypeStruct((t, D_MODEL), BF16))
        out_specs.append(row)
    res = pl.pallas_call(
        functools.partial(_ffn_kernel, chunked),
        grid=(t // tm,),
        in_specs=[row, _const_spec((1, D_MODEL)), _const_spec((1, D_MODEL)),
                  _const_spec((D_MODEL, D_FF)), _const_spec((D_MODEL, D_FF)),
                  _const_spec((D_FF, D_MODEL))],
        out_specs=out_specs,
        out_shape=out_shape,
        scratch_shapes=scratch,
        compiler_params=_params(1),
        name="ffn",
    )(x, g.reshape(1, D_MODEL), gn.reshape(1, D_MODEL), wg, wu, wd)
    return (res[0], res[1]) if emit else (res[0], None)


def _inproj_kernel(tiles_per_seq, h_ref, wqk_ref, wv_ref, wf_ref, gains_ref, ones_ref, tri_ref,
                   fgb_ref, sel_ref, aug1_ref, qk_ref, vt_ref, aug_ref, carry_ref):
    i = pl.program_id(0)
    h = h_ref[...]
    ones_blk = ones_ref[...]
    for c in range(QK_COLS // 512):
        y = _dot(h, wqk_ref[:, c * 512:(c + 1) * 512])
        y2 = y * y
        hi = y2.astype(BF16)
        lo = (y2 - hi.astype(F32)).astype(BF16)
        ssq = jnp.concatenate(
            [_dot(hi[:, k * MXU_DIM:(k + 1) * MXU_DIM], ones_blk)
             + _dot(lo[:, k * MXU_DIM:(k + 1) * MXU_DIM], ones_blk)
             for k in range(512 // MXU_DIM)], axis=1)
        y = y * lax.rsqrt(ssq * (1.0 / HEAD_DIM) + NORM_EPS) * gains_ref[c:c + 1, :]
        qk_ref[:, c * 512:(c + 1) * 512] = y.astype(BF16)
    for c in range(V_COLS // 512):
        vt = _dot_nt(wv_ref[c * 512:(c + 1) * 512, :], h)
        vt_ref[c * 512:(c + 1) * 512, :] = vt.astype(BF16)

    fl = _dot(h, wf_ref[...]) + fgb_ref[...]
    logf = -(jnp.maximum(-fl, 0.0) + jnp.log1p(jnp.exp(-jnp.abs(fl))))

    @pl.when(i % tiles_per_seq == 0)
    def _():
        carry_ref[...] = jnp.zeros_like(carry_ref)

    tri = tri_ref[...]
    cum = sum(_dot(tri, p) for p in _split3(logf)) + carry_ref[...]
    carry_ref[...] = cum[cum.shape[0] - 1:, :]
    aug = sum(_dot(p, sel_ref[n]) for n, p in enumerate(_split3(cum * LOG2E))) + aug1_ref[...]
    aug_ref[...] = aug.astype(BF16)


def _decay_lane_maps():
    sel = np.zeros((3, LANES, AUG_COLS), np.float32)
    ones = np.zeros((1, AUG_COLS), np.float32)
    k_off = AUG_COLS // 2
    for head in range(N_FOX_HEADS):
        base = head * 6
        for n in range(3):
            sel[n, head, base + n] = 1.0
            sel[n, head, k_off + base + 3 + n] = -1.0
            ones[0, base + 3 + n] = 1.0
            ones[0, k_off + base + n] = 1.0
    return jnp.asarray(sel, BF16), jnp.asarray(ones, F32)


def _inproj(h, w_qk, w_v, w_f, gains, fg_bias_row, bsz, seq):
    t = h.shape[0]
    tm = min(TOKEN_TILE, seq)
    tps = seq // tm
    ones_blk = jnp.kron(jnp.eye(MXU_DIM // HEAD_DIM, dtype=F32),
                        jnp.ones((HEAD_DIM, HEAD_DIM), F32)).astype(BF16)
    tri = jnp.tril(jnp.ones((tm, tm), F32)).astype(BF16)
    sel, aug_ones = _decay_lane_maps()
    return pl.pallas_call(
        functools.partial(_inproj_kernel, tps),
        grid=(t // tm,),
        in_specs=[pl.BlockSpec((tm, D_MODEL), lambda i: (i, 0)),
                  _const_spec((D_MODEL, QK_COLS)), _const_spec((D_MODEL, V_COLS)),
                  _const_spec((D_MODEL, LANES)),
                  _const_spec((QK_COLS // 512, 512)), _const_spec((MXU_DIM, MXU_DIM)),
                  _const_spec((tm, tm)), _const_spec((1, LANES)),
                  _const_spec((3, LANES, AUG_COLS)), _const_spec((1, AUG_COLS))],
        out_specs=[pl.BlockSpec((tm, QK_COLS), lambda i: (i, 0)),
                   pl.BlockSpec((None, V_COLS, tm), lambda i: (i // tps, 0, i % tps)),
                   pl.BlockSpec((tm, AUG_COLS), lambda i: (i, 0))],
        out_shape=[jax.ShapeDtypeStruct((t, QK_COLS), BF16),
                   jax.ShapeDtypeStruct((bsz, V_COLS, seq), BF16),
                   jax.ShapeDtypeStruct((t, AUG_COLS), BF16)],
        scratch_shapes=[pltpu.VMEM((1, LANES), F32)],
        compiler_params=_params(1),
        name="attn_inproj",
    )(h, w_qk, w_v, w_f, gains, ones_blk, tri, fg_bias_row, sel, aug_ones)


def _attn_sweep(qi, q_both, k_ref, kaug_ref, vt_ref, bias_ref, m_ref, l_ref, acc_ref):
    blk = q_both.shape[0] // 2
    m_ref[...] = jnp.full_like(m_ref, NEG_BIG)
    l_ref[...] = jnp.zeros_like(l_ref)
    acc_ref[...] = jnp.zeros_like(acc_ref)

    def run(items):
        starts = [pl.multiple_of(j * blk, blk) for j, _, _ in items]
        lanes = [slice(c * ATTN_LANES, (c + 1) * ATTN_LANES)
                 for c in range(2 * blk // ATTN_LANES)]

        def n_keys(n, ls):
            first_half = (ls.start % blk) + ATTN_LANES <= blk // 2
            return blk // 2 if items[n][2] and first_half else blk

        scores = {}
        for n, start in enumerate(starts):
            k_blk = k_ref[pl.ds(start, blk), :]
            if kaug_ref is not None:
                k_blk = jnp.concatenate([k_blk, kaug_ref[pl.ds(start, blk), :]], axis=1)
            for c, ls in enumerate(lanes):
                scores[n, c] = _dot_nt(k_blk[:n_keys(n, ls)], q_both[ls, :])
        for n, (_, bias_idx, masked) in enumerate(items):
            for c, ls in enumerate(lanes):
                s = scores.pop((n, c))
                keys = s.shape[0]
                if bias_idx is not None and bias_ref is not None:
                    s = bias_ref[bias_idx, :keys, ls] + s
                if masked:
                    key = lax.broadcasted_iota(jnp.int32, s.shape, 0)
                    qry = (lax.broadcasted_iota(jnp.int32, s.shape, 1) + ls.start) & (blk - 1)
                    s = jnp.where(key <= qry, s, NEG_BIG)
                m_prev = m_ref[:, ls]
                m_new = jnp.maximum(m_prev, jnp.max(s, axis=0, keepdims=True))
                alpha = jnp.exp2(m_prev - m_new)
                p = jnp.exp2(s - m_new)
                l_ref[:, ls] = alpha * l_ref[:, ls] + jnp.sum(p, axis=0, keepdims=True)
                m_ref[:, ls] = m_new
                acc_ref[:, ls] = alpha * acc_ref[:, ls] + _dot(
                    vt_ref[:, pl.ds(starts[n], keys)], p.astype(BF16))

    n_far = jnp.maximum(qi - 1, 0)

    def far_group(j, carry):
        run([(ATTN_GROUP * j + n, None, False) for n in range(ATTN_GROUP)])
        return carry

    lax.fori_loop(0, n_far // ATTN_GROUP, far_group, 0)
    prev, diag = (qi - 1, 1, False), (qi, 0, True)

    @pl.when(qi == 0)
    def _():
        run([diag])

    for left in range(ATTN_GROUP):
        @pl.when((qi >= 1) & (n_far % ATTN_GROUP == left))
        def _():
            run([(n_far - left + n, None, False) for n in range(left)] + [prev, diag])

    halves = (slice(None), slice(0, blk)), (slice(None), slice(blk, 2 * blk))
    return [acc_ref[sl] / l_ref[sl] for sl in halves]


def _diff_attn_kernel(scal_ref, q_ref, k_ref, vt_ref, bias_ref, subln_ref, o_ref,
                      m_ref, l_ref, acc_ref):
    blk = bias_ref.shape[1]
    lam = scal_ref[0]
    out_scale = scal_ref[1]

    def query_block(qi, carry):
        rows = pl.ds(pl.multiple_of(qi * blk, blk), blk)
        q = q_ref[rows, :]
        lane = lax.broadcasted_iota(jnp.int32, q.shape, 1)
        zero = jnp.zeros_like(q)
        q_both = jnp.concatenate([jnp.where(lane < HEAD_DIM, q, zero),
                                  jnp.where(lane < HEAD_DIM, zero, q)], axis=0)
        o1, o2 = _attn_sweep(qi, q_both, k_ref, None, vt_ref, bias_ref, m_ref, l_ref, acc_ref)
        o = o1 - lam * o2
        ms = jnp.mean(o * o, axis=0, keepdims=True)
        o = o * lax.rsqrt(ms + NORM_EPS) * subln_ref[...] * out_scale
        o_ref[rows, :] = o.T.astype(o_ref.dtype)
        return carry

    lax.fori_loop(0, q_ref.shape[0] // blk, query_block, 0)


def _fox_attn_kernel(blk, q_ref, qaug_ref, k_ref, kaug_ref, vt_ref, o_ref, m_ref, l_ref, acc_ref):
    decay = LANES + 12 * pl.program_id(1)

    def query_block(qi, carry):
        rows = pl.ds(pl.multiple_of(qi * blk, blk), blk)
        q = jnp.concatenate([q_ref[rows, :], qaug_ref[rows, :]], axis=1)
        lane = lax.broadcasted_iota(jnp.int32, q.shape, 1)
        zero = jnp.zeros_like(q)
        in_a = (lane < HEAD_DIM) | ((lane >= decay) & (lane < decay + 6))
        in_b = (((lane >= HEAD_DIM) & (lane < LANES))
                | ((lane >= decay + 6) & (lane < decay + 12)))
        q_both = jnp.concatenate([jnp.where(in_a, q, zero), jnp.where(in_b, q, zero)], axis=0)
        o1, o2 = _attn_sweep(qi, q_both, k_ref, kaug_ref, vt_ref, None, m_ref, l_ref, acc_ref)
        row = lax.broadcasted_iota(jnp.int32, o1.shape, 0)
        o_ref[rows, :] = jnp.where(row < HEAD_DIM, o1, o2).T.astype(o_ref.dtype)
        return carry

    lax.fori_loop(0, q_ref.shape[0] // blk, query_block, 0)


def _attn_scratch(tq):
    return [pltpu.VMEM((1, 2 * tq), F32), pltpu.VMEM((1, 2 * tq), F32),
            pltpu.VMEM((LANES, 2 * tq), F32)]


def _seq_block(col):
    return lambda seq: pl.BlockSpec((None, seq, LANES), lambda b, s: (b, 0, col(s)))


def _diff_attn(qk, vt, bias_tiles, scal, subln_col):
    bsz, seq, _ = qk.shape
    tq = min(ATTN_Q, seq)
    return pl.pallas_call(
        _diff_attn_kernel,
        grid=(bsz, N_DIFF_HEADS),
        in_specs=[pl.BlockSpec(memory_space=pltpu.SMEM),
                  _seq_block(lambda s: s)(seq), _seq_block(lambda s: 4 + s)(seq),
                  pl.BlockSpec((None, LANES, seq), lambda b, s: (b, s, 0)),
                  pl.BlockSpec((None, 2, tq, 2 * tq), lambda b, s: (s, 0, 0, 0)),
                  _const_spec((LANES, 1))],
        out_specs=_seq_block(lambda s: s)(seq),
        out_shape=jax.ShapeDtypeStruct((bsz, seq, N_DIFF_HEADS * LANES), BF16),
        scratch_shapes=_attn_scratch(tq),
        compiler_params=_params(2),
        name="diff_attn",
    )(scal, qk, qk, vt, bias_tiles, subln_col)


def _fox_attn(qk, aug, vt):
    bsz, seq, _ = qk.shape
    tq = min(ATTN_Q, seq)
    return pl.pallas_call(
        functools.partial(_fox_attn_kernel, tq),
        grid=(bsz, N_FOX_HEADS // 2),
        in_specs=[_seq_block(lambda s: 8 + s)(seq), _seq_block(lambda s: 0)(seq),
                  _seq_block(lambda s: 12 + s)(seq), _seq_block(lambda s: 1)(seq),
                  pl.BlockSpec((None, LANES, seq), lambda b, s: (b, 4 + s, 0))],
        out_specs=_seq_block(lambda s: s)(seq),
        out_shape=jax.ShapeDtypeStruct((bsz, seq, N_FOX_HEADS * HEAD_DIM), BF16),
        scratch_shapes=_attn_scratch(tq),
        compiler_params=_params(2),
        name="fox_attn",
    )(qk, aug, qk, aug, vt)


def _outproj_kernel(x_ref, od_ref, of_ref, wd_ref, wf_ref, o_ref):
    o_ref[...] = x_ref[...] + _dot(od_ref[...], wd_ref[...]) + _dot(of_ref[...], wf_ref[...])


def _outproj(x, od, of, w_d, w_f):
    t = x.shape[0]
    tm = min(FFN_TILE, t)
    half = od.shape[1]
    return pl.pallas_call(
        _outproj_kernel,
        grid=(t // tm,),
        in_specs=[pl.BlockSpec((tm, D_MODEL), lambda i: (i, 0)),
                  pl.BlockSpec((tm, half), lambda i: (i, 0)),
                  pl.BlockSpec((tm, half), lambda i: (i, 0)),
                  _const_spec((half, D_MODEL)), _const_spec((half, D_MODEL))],
        out_specs=pl.BlockSpec((tm, D_MODEL), lambda i: (i, 0)),
        out_shape=jax.ShapeDtypeStruct((t, D_MODEL), F32),
        compiler_params=_params(1),
        name="attn_outproj",
    )(x, od, of, w_d, w_f)


def _rel_bias_by_distance(rel_bias, n_dist):
    n = jnp.arange(n_dist, dtype=jnp.int32)
    max_exact = N_REL_BUCKETS // 2
    nf = jnp.maximum(n, 1).astype(F32)
    large = max_exact + (jnp.log(nf / max_exact) / math.log(REL_MAX_DIST / max_exact)
                         * (N_REL_BUCKETS - max_exact)).astype(jnp.int32)
    large = jnp.minimum(large, N_REL_BUCKETS - 1)
    return rel_bias[jnp.where(n < max_exact, n, large)]


def _toeplitz(w, n):
    heads, period = w.shape
    flat = jnp.tile(w, (1, n))[:, :n * (period - 1)]
    return flat.reshape(heads, n, period - 1)[:, :, :n]


def _diff_bias_tiles(rel_bias, tq):
    by_dist = _rel_bias_by_distance(rel_bias.astype(F32), 2 * tq)
    by_dist = ((by_dist - by_dist[2 * tq - 1]) * LOG2E).T
    diag = _toeplitz(by_dist, tq)
    prev = _toeplitz(jnp.roll(by_dist, -tq, axis=1), tq)
    tiles = jnp.stack([diag, prev], axis=1)
    return jnp.concatenate([tiles, tiles], axis=-1)


def _attn_layer(x, h, bsz, seq, layer_idx, w_in, w_out, fg_bias, dq_g, dk_g, lq1, lk1, lq2, lk2,
                subln_g, fq_g, fk_g, bias_tiles):
    tq = min(ATTN_Q, seq)
    q_scale = HEAD_DIM ** -0.5 * LOG2E
    w_bf = w_in.astype(BF16)
    w_qk = jnp.concatenate([w_bf[:, 0:1024], w_bf[:, 1536:2560]], axis=1)
    w_v = jnp.concatenate([w_bf[:, 1024:1536], w_bf[:, 2560:3072]], axis=1).T
    w_f = jnp.pad(w_bf[:, 3072:], ((0, 0), (0, LANES - N_FOX_HEADS)))
    fgb = jnp.pad(fg_bias.astype(F32), (0, LANES - N_FOX_HEADS)).reshape(1, LANES)
    gains = jnp.stack([jnp.tile(dq_g, 8) * q_scale, jnp.tile(dk_g, 8),
                       jnp.tile(fq_g, 8) * q_scale, jnp.tile(fk_g, 8)]).astype(F32)
    qk, vt, aug = _inproj(h, w_qk, w_v, w_f, gains, fgb, bsz, seq)
    qk = qk.reshape(bsz, seq, QK_COLS)

    lam_init = 0.8 - 0.6 * math.exp(-0.3 * layer_idx)
    lam = (jnp.exp(jnp.sum(lq1.astype(F32) * lk1.astype(F32)))
           - jnp.exp(jnp.sum(lq2.astype(F32) * lk2.astype(F32))) + lam_init)
    scal = jnp.stack([lam, jnp.asarray(1.0 - lam_init, F32)]).astype(F32)
    od = _diff_attn(qk, vt, bias_tiles, scal, subln_g.astype(F32).reshape(LANES, 1))

    of = _fox_attn(qk, aug.reshape(bsz, seq, AUG_COLS), vt)

    w_o = w_out.astype(BF16)
    half = N_DIFF_HEADS * LANES
    return _outproj(x, od.reshape(bsz * seq, half), of.reshape(bsz * seq, half),
                    w_o[:half], w_o[half:])


def _s5_kernel(n_chunks, *refs):
    x_refs = refs[:S5_CHUNK]
    kblk_ref, p_ref, q_ref, a_ref, y_ref, z_ref, xp_ref, t_ref = refs[S5_CHUNK:]

    @pl.when(pl.program_id(1) == 0)
    def _():
        t_ref[...] = jnp.zeros_like(t_ref)
        for s in range(S5_CHUNK):
            for t in range(s, S5_CHUNK):
                t_ref[s * LANES:(s + 1) * LANES, t * LANES:(t + 1) * LANES] = kblk_ref[t - s]

    x = jnp.concatenate([r[...] for r in x_refs], axis=1)
    n_rows = x.shape[0]
    n_seq = n_rows // n_chunks
    half = z_ref.shape[1] // 2
    z_ref[...] = _dot(x, p_ref[...])
    a_re = a_ref[0:1, :]
    a_im = a_ref[1:2, :]

    def scan(c, carry):
        out = []
        for b in range(n_seq):
            x_re, x_im = carry[2 * b], carry[2 * b + 1]
            row = pl.ds(b * n_chunks + c, 1)
            xp_ref[row, :half] = x_re
            xp_ref[row, half:] = x_im
            z = z_ref[row, :]
            out += [a_re * x_re - a_im * x_im + z[:, :half],
                    a_re * x_im + a_im * x_re + z[:, half:]]
        return tuple(out)

    zero = jnp.zeros((1, half), F32)
    lax.fori_loop(0, n_chunks, scan, (zero,) * (2 * n_seq))
    inter = _dot(xp_ref[...].astype(BF16), q_ref[...])
    for j in range(S5_CHUNK // 2):
        cols = slice(2 * j * LANES, (2 * j + 2) * LANES)
        y = _dot(x[:, :cols.stop], t_ref[:cols.stop, cols]) + inter[:, cols]
        y_ref[pl.ds(2 * j, n_rows, stride=S5_CHUNK), :] = y[:, :LANES]
        y_ref[pl.ds(2 * j + 1, n_rows, stride=S5_CHUNK), :] = y[:, LANES:]


def _s5_operators(a_re, a_im, log_step, b_re, b_im, c_re, c_im):
    hp = lax.Precision.HIGHEST
    L, kb, gb = S5_CHUNK, S5_GROUPS // 8, 8
    step = jnp.exp(log_step.astype(F32))[:, None]
    ar, ai = a_re.astype(F32), a_im.astype(F32)
    mag = jnp.exp(ar * step)
    lr, li = mag * jnp.cos(ai * step), mag * jnp.sin(ai * step)
    den = ar * ar + ai * ai
    cr = (((lr - 1.0) * ar + li * ai) / den)[:, None, :]
    ci = ((li * ar - (lr - 1.0) * ai) / den)[:, None, :]
    br, bi = b_re.astype(F32).transpose(0, 2, 1), b_im.astype(F32).transpose(0, 2, 1)
    bbr, bbi = cr * br - ci * bi, cr * bi + ci * br
    pr, pi = [jnp.ones_like(lr)], [jnp.zeros_like(li)]
    for _ in range(L):
        pr, pi = pr + [pr[-1] * lr - pi[-1] * li], pi + [pr[-1] * li + pi[-1] * lr]
    pr, pi = jnp.stack(pr), jnp.stack(pi)
    wr = pr[:L, :, None, :] * bbr - pi[:L, :, None, :] * bbi
    wi = pr[:L, :, None, :] * bbi + pi[:L, :, None, :] * bbr
    cre, cim = c_re.astype(F32), c_im.astype(F32)
    kern = (jnp.einsum('gcp,ngap->ngac', cre, wr, precision=hp)
            - jnp.einsum('gcp,ngap->ngac', cim, wi, precision=hp))

    def group_mask(rows, rows_per_group, cols, cols_per_group):
        r = lax.broadcasted_iota(jnp.int32, (rows, 1), 0) // rows_per_group
        c = lax.broadcasted_iota(jnp.int32, (1, cols), 1) // cols_per_group
        return r == c

    def spread_channels(m):
        src = lax.broadcasted_iota(jnp.int32, (S5_GROUP, LANES), 0)
        dst = lax.broadcasted_iota(jnp.int32, (S5_GROUP, LANES), 1)
        return jnp.dot(m.astype(BF16), (src == dst % S5_GROUP).astype(BF16),
                       preferred_element_type=BF16)

    zero = jnp.zeros((), BF16)
    k_blk = kern.reshape(L, kb, LANES, S5_GROUP).transpose(1, 0, 2, 3)
    k_blk = jnp.where(group_mask(LANES, S5_GROUP, LANES, S5_GROUP), spread_channels(k_blk), zero)

    def state_in(w):
        w = jnp.tile(w[::-1].astype(BF16).reshape(L, kb, LANES, S5_STATE), (1, 1, 1, gb))
        w = jnp.where(group_mask(LANES, S5_GROUP, gb * S5_STATE, S5_STATE), w, zero)
        return w.transpose(1, 0, 2, 3).reshape(kb, L * LANES, gb * S5_STATE)
    p_big = jnp.concatenate([state_in(wr), state_in(wi)], axis=2)

    cre_t, cim_t = cre.transpose(0, 2, 1), cim.transpose(0, 2, 1)
    def state_out(m):
        m = spread_channels(m.reshape(L, kb, gb * S5_STATE, S5_GROUP))
        m = jnp.where(group_mask(gb * S5_STATE, S5_STATE, LANES, S5_GROUP), m, zero)
        return m.transpose(1, 2, 0, 3).reshape(kb, gb * S5_STATE, L * LANES)
    q_re = cre_t[None] * pr[1:, :, :, None] - cim_t[None] * pi[1:, :, :, None]
    q_im = -(cre_t[None] * pi[1:, :, :, None] + cim_t[None] * pr[1:, :, :, None])
    q_big = jnp.concatenate([state_out(q_re), state_out(q_im)], axis=1)
    a_big = jnp.stack([pr[L].reshape(kb, gb * S5_STATE), pi[L].reshape(kb, gb * S5_STATE)], axis=1)
    return k_blk, p_big, q_big, a_big


def _s5_core(h2, bsz, seq, a_re, a_im, log_step, b_re, b_im, c_re, c_im):
    L = S5_CHUNK
    n_chunks = seq // L
    rows = bsz * n_chunks
    kb = D_MODEL // LANES
    rb = min(S5_ROWS, rows)
    assert rb % n_chunks == 0
    k_blk, p_big, q_big, a_big = _s5_operators(a_re, a_im, log_step, b_re, b_im, c_re, c_im)
    n_state = 8 * S5_STATE
    x_specs = [pl.BlockSpec((rb, LANES), lambda k, r, t=t: (r, t * kb + k)) for t in range(L)]
    return pl.pallas_call(
        functools.partial(_s5_kernel, n_chunks),
        grid=(kb, rows // rb),
        in_specs=x_specs + [
            pl.BlockSpec((None, L, LANES, LANES), lambda k, r: (k, 0, 0, 0)),
            pl.BlockSpec((None, L * LANES, 2 * n_state), lambda k, r: (k, 0, 0)),
            pl.BlockSpec((None, 2 * n_state, L * LANES), lambda k, r: (k, 0, 0)),
            pl.BlockSpec((None, 2, n_state), lambda k, r: (k, 0, 0))],
        out_specs=pl.BlockSpec((rb * L, LANES), lambda k, r: (r, k)),
        out_shape=jax.ShapeDtypeStruct((rows * L, D_MODEL), F32),
        scratch_shapes=[pltpu.VMEM((rb, 2 * n_state), F32), pltpu.VMEM((rb, 2 * n_state), F32),
                        pltpu.VMEM((L * LANES, L * LANES), BF16)],
        compiler_params=_params(2),
        name="s5_core",
    )(*([h2] * L), k_blk, p_big, q_big, a_big)


def _glu_kernel(x_ref, y_ref, g_ref, d_ref, wa_ref, wb_ref, o_ref):
    x = x_ref[...]
    y = y_ref[...] + d_ref[...] * _rms(x, g_ref[...])
    y = jax.nn.gelu(y).astype(BF16)
    o_ref[...] = x + _dot(y, wa_ref[...]) * jax.nn.sigmoid(_dot(y, wb_ref[...]))


def _s5_layer(x, h2, bsz, seq, mix_g, a_re, a_im, log_step, b_re, b_im, c_re, c_im, d_skip,
              w_a, w_b):
    t = x.shape[0]
    tm = min(FFN_TILE, t)
    y = _s5_core(h2, bsz, seq, a_re, a_im, log_step, b_re, b_im, c_re, c_im)
    row = pl.BlockSpec((tm, D_MODEL), lambda i: (i, 0))
    return pl.pallas_call(
        _glu_kernel,
        grid=(t // tm,),
        in_specs=[row, row, _const_spec((1, D_MODEL)), _const_spec((1, D_MODEL)),
                  _const_spec((D_MODEL, D_MODEL)), _const_spec((D_MODEL, D_MODEL))],
        out_specs=row,
        out_shape=jax.ShapeDtypeStruct((t, D_MODEL), F32),
        compiler_params=_params(1),
        name="s5_glu",
    )(x, y, mix_g.astype(F32).reshape(1, D_MODEL), d_skip.astype(F32).reshape(1, D_MODEL),
      w_a.astype(BF16), w_b.astype(BF16))


def kernel(x, ffn1_norm, ffn1_gate, ffn1_up, ffn1_down, mix_norm, ffn2_norm, ffn2_gate, ffn2_up, ffn2_down, attn_w_in, attn_w_out, fg_bias, diff_q_norm, diff_k_norm, diff_lambda_q1, diff_lambda_k1, diff_lambda_q2, diff_lambda_k2, diff_subln, fox_q_norm, fox_k_norm, rel_bias, s5_a_re, s5_a_im, s5_log_step, s5_b_re, s5_b_im, s5_c_re, s5_c_im, s5_d, s5_glu_a, s5_glu_b):
    bsz, seq, _ = x.shape
    depth = ffn1_norm.shape[0]
    assert min(ATTN_Q, seq) >= REL_MAX_DIST, "bias tiles assume the bias saturates within a block"
    bias_tiles = _diff_bias_tiles(rel_bias, min(ATTN_Q, seq))
    xt = x.reshape(bsz * seq, D_MODEL).astype(F32)
    for i in range(depth):
        xt, h = _ffn(xt, ffn1_norm[i], ffn1_gate[i].astype(BF16), ffn1_up[i].astype(BF16),
                     ffn1_down[i].astype(BF16), g_next=mix_norm[i], chunked=i % 2 == 1)
        j = i // 2
        if i % 2 == 0:
            xt = _attn_layer(xt, h, bsz, seq, i, attn_w_in[j], attn_w_out[j], fg_bias[j],
                             diff_q_norm[j], diff_k_norm[j], diff_lambda_q1[j], diff_lambda_k1[j],
                             diff_lambda_q2[j], diff_lambda_k2[j], diff_subln[j], fox_q_norm[j],
                             fox_k_norm[j], bias_tiles)
        else:
            xt = _s5_layer(xt, h, bsz, seq, mix_norm[i], s5_a_re[j], s5_a_im[j], s5_log_step[j],
                           s5_b_re[j], s5_b_im[j], s5_c_re[j], s5_c_im[j], s5_d[j],
                           s5_glu_a[j], s5_glu_b[j])
        xt, _ = _ffn(xt, ffn2_norm[i], ffn2_gate[i].astype(BF16), ffn2_up[i].astype(BF16),
                     ffn2_down[i].astype(BF16))
    return xt.reshape(bsz, seq, D_MODEL).astype(x.dtype)
```

```python
import functools
import math

import jax
import jax.numpy as jnp
import numpy as np
from jax import lax
from jax.experimental import pallas as pl
from jax.experimental.pallas import tpu as pltpu

D_MODEL = 1024
D_FF = 2816
HEAD_DIM = 64
N_DIFF_HEADS = 4
N_FOX_HEADS = 8
N_REL_BUCKETS = 32
REL_MAX_DIST = 128
S5_GROUP = 16
S5_GROUPS = D_MODEL // S5_GROUP
S5_STATE = 64
NORM_EPS = 1e-6

QK_COLS = 2048
V_COLS = 1024
AUG_COLS = 256
LANES = 128
MXU_DIM = 256
VMEM_LIMIT = 56 * 1024 * 1024

TOKEN_TILE = 512
FFN_TILE = 1024
FFN_SPLITS = (0, 6 * MXU_DIM, D_FF)
ATTN_Q = 512
ATTN_LANES = 256
ATTN_GROUP = 4
S5_CHUNK = 16
S5_ROWS = 512
NEG_BIG = -1e30
LOG2E = 1.4426950408889634

BF16 = jnp.bfloat16
F32 = jnp.float32


def _dot(a, b):
    return jnp.dot(a, b, preferred_element_type=F32)


def _dot_nt(a, b):
    return lax.dot_general(a, b, (((1,), (1,)), ((), ())), preferred_element_type=F32)


def _rms(x, g):
    ms = jnp.mean(x * x, axis=-1, keepdims=True)
    return x * lax.rsqrt(ms + NORM_EPS) * g


def _const_spec(shape):
    return pl.BlockSpec(shape, lambda *_: (0,) * len(shape), pipeline_mode=pl.Buffered(1))


def _params(n_axes):
    return pltpu.CompilerParams(dimension_semantics=("arbitrary",) * n_axes,
                                vmem_limit_bytes=VMEM_LIMIT)


def _split3(x):
    p0 = x.astype(BF16)
    r1 = x - p0.astype(F32)
    p1 = r1.astype(BF16)
    p2 = (r1 - p1.astype(F32)).astype(BF16)
    return p0, p1, p2


def _ffn_kernel(chunked, x_ref, g_ref, gn_ref, wg_ref, wu_ref, wd_ref, o_ref, *rest):
    x = x_ref[...]
    h = _rms(x, g_ref[...]).astype(BF16)
    acc = None
    for lo, hi in zip(FFN_SPLITS[:-1], FFN_SPLITS[1:]):
        sl = slice(lo, hi)
        a = _dot(h, wg_ref[:, sl])
        b = _dot(h, wu_ref[:, sl])
        act = (a * jax.nn.sigmoid(a) * b).astype(BF16)
        d = _dot(act, wd_ref[sl, :])
        acc = d if acc is None else acc + d
    xo = x + 0.5 * acc
    o_ref[...] = xo
    if not rest:
        return
    hn = _rms(xo, gn_ref[...])
    if not chunked:
        rest[0][...] = hn.astype(BF16)
        return
    h_ref, stage_ref = rest
    n_rows = h_ref.shape[0]
    for k in range(D_MODEL // LANES):
        stage_ref[k] = hn[:, k * LANES:(k + 1) * LANES]
    for t in range(S5_CHUNK):
        for k in range(D_MODEL // LANES):
            lo = t * D_MODEL + k * LANES
            h_ref[:, lo:lo + LANES] = stage_ref[k, pl.ds(t, n_rows, stride=S5_CHUNK), :].astype(BF16)


def _ffn(x, g, wg, wu, wd, g_next=None, chunked=False):
    t = x.shape[0]
    tm = min(FFN_TILE, t)
    emit = g_next is not None
    gn = g_next if emit else g
    row = pl.BlockSpec((tm, D_MODEL), lambda i: (i, 0))
    out_shape = [jax.ShapeDtypeStruct((t, D_MODEL), F32)]
    out_specs = [row]
    scratch = []
    if emit and chunked:
        out_shape.append(jax.ShapeDtypeStruct((t // S5_CHUNK, S5_CHUNK * D_MODEL), BF16))
        out_specs.append(pl.BlockSpec((tm // S5_CHUNK, S5_CHUNK * D_MODEL), lambda i: (i, 0)))
        scratch = [pltpu.VMEM((D_MODEL // LANES, tm, LANES), F32)]
    elif emit:
        out_shape.append(jax.ShapeDtypeStruct((t, D_MODEL), BF16))
        out_specs.append(row)
    res = pl.pallas_call(
        functools.partial(_ffn_kernel, chunked),
        grid=(t // tm,),
        in_specs=[row, _const_spec((1, D_MODEL)), _const_spec((1, D_MODEL)),
                  _const_spec((D_MODEL, D_FF)), _const_spec((D_MODEL, D_FF)),
                  _const_spec((D_FF, D_MODEL))],
        out_specs=out_specs,
        out_shape=out_shape,
        scratch_shapes=scratch,
        compiler_params=_params(1),
        name="ffn",
    )(x, g.reshape(1, D_MODEL), gn.reshape(1, D_MODEL), wg, wu, wd)
    return (res[0], res[1]) if emit else (res[0], None)


def _inproj_kernel(tiles_per_seq, h_ref, wqk_ref, wv_ref, wf_ref, gains_ref, ones_ref, tri_ref,
                   fgb_ref, sel_ref, aug1_ref, qk_ref, vt_ref, aug_ref, carry_ref):
    i = pl.program_id(0)
    h = h_ref[...]
    ones_blk = ones_ref[...]
    for c in range(QK_COLS // 512):
        y = _dot(h, wqk_ref[:, c * 512:(c + 1) * 512])
        y2 = y * y
        hi = y2.astype(BF16)
        lo = (y2 - hi.astype(F32)).astype(BF16)
        ssq = jnp.concatenate(
            [_dot(hi[:, k * MXU_DIM:(k + 1) * MXU_DIM], ones_blk)
             + _dot(lo[:, k * MXU_DIM:(k + 1) * MXU_DIM], ones_blk)
             for k in range(512 // MXU_DIM)], axis=1)
        y = y * lax.rsqrt(ssq * (1.0 / HEAD_DIM) + NORM_EPS) * gains_ref[c:c + 1, :]
        qk_ref[:, c * 512:(c + 1) * 512] = y.astype(BF16)
    for c in range(V_COLS // 512):
        vt = _dot_nt(wv_ref[c * 512:(c + 1) * 512, :], h)
        vt_ref[c * 512:(c + 1) * 512, :] = vt.astype(BF16)

    fl = _dot(h, wf_ref[...]) + fgb_ref[...]
    logf = -(jnp.maximum(-fl, 0.0) + jnp.log1p(jnp.exp(-jnp.abs(fl))))

    @pl.when(i % tiles_per_seq == 0)
    def _():
        carry_ref[...] = jnp.zeros_like(carry_ref)

    tri = tri_ref[...]
    cum = sum(_dot(tri, p) for p in _split3(logf)) + carry_ref[...]
    carry_ref[...] = cum[cum.shape[0] - 1:, :]
    aug = sum(_dot(p, sel_ref[n]) for n, p in enumerate(_split3(cum * LOG2E))) + aug1_ref[...]
    aug_ref[...] = aug.astype(BF16)


def _decay_lane_maps():
    sel = np.zeros((3, LANES, AUG_COLS), np.float32)
    ones = np.zeros((1, AUG_COLS), np.float32)
    k_off = AUG_COLS // 2
    for head in range(N_FOX_HEADS):
        base = head * 6
        for n in range(3):
            sel[n, head, base + n] = 1.0
            sel[n, head, k_off + base + 3 + n] = -1.0
            ones[0, base + 3 + n] = 1.0
            ones[0, k_off + base + n] = 1.0
    return jnp.asarray(sel, BF16), jnp.asarray(ones, F32)


def _inproj(h, w_qk, w_v, w_f, gains, fg_bias_row, bsz, seq):
    t = h.shape[0]
    tm = min(TOKEN_TILE, seq)
    tps = seq // tm
    ones_blk = jnp.kron(jnp.eye(MXU_DIM // HEAD_DIM, dtype=F32),
                        jnp.ones((HEAD_DIM, HEAD_DIM), F32)).astype(BF16)
    tri = jnp.tril(jnp.ones((tm, tm), F32)).astype(BF16)
    sel, aug_ones = _decay_lane_maps()
    return pl.pallas_call(
        functools.partial(_inproj_kernel, tps),
        grid=(t // tm,),
        in_specs=[pl.BlockSpec((tm, D_MODEL), lambda i: (i, 0)),
                  _const_spec((D_MODEL, QK_COLS)), _const_spec((D_MODEL, V_COLS)),
                  _const_spec((D_MODEL, LANES)),
                  _const_spec((QK_COLS // 512, 512)), _const_spec((MXU_DIM, MXU_DIM)),
                  _const_spec((tm, tm)), _const_spec((1, LANES)),
                  _const_spec((3, LANES, AUG_COLS)), _const_spec((1, AUG_COLS))],
        out_specs=[pl.BlockSpec((tm, QK_COLS), lambda i: (i, 0)),
                   pl.BlockSpec((None, V_COLS, tm), lambda i: (i // tps, 0, i % tps)),
                   pl.BlockSpec((tm, AUG_COLS), lambda i: (i, 0))],
        out_shape=[jax.ShapeDtypeStruct((t, QK_COLS), BF16),
                   jax.ShapeDtypeStruct((bsz, V_COLS, seq), BF16),
                   jax.ShapeDtypeStruct((t, AUG_COLS), BF16)],
        scratch_shapes=[pltpu.VMEM((1, LANES), F32)],
        compiler_params=_params(1),
        name="attn_inproj",
    )(h, w_qk, w_v, w_f, gains, ones_blk, tri, fg_bias_row, sel, aug_ones)


def _attn_sweep(qi, q_both, k_ref, kaug_ref, vt_ref, bias_ref, m_ref, l_ref, acc_ref):
    blk = q_both.shape[0] // 2
    m_ref[...] = jnp.full_like(m_ref, NEG_BIG)
    l_ref[...] = jnp.zeros_like(l_ref)
    acc_ref[...] = jnp.zeros_like(acc_ref)

    def run(items):
        starts = [pl.multiple_of(j * blk, blk) for j, _, _ in items]
        lanes = [slice(c * ATTN_LANES, (c + 1) * ATTN_LANES)
                 for c in range(2 * blk // ATTN_LANES)]

        def n_keys(n, ls):
            first_half = (ls.start % blk) + ATTN_LANES <= blk // 2
            return blk // 2 if items[n][2] and first_half else blk

        scores = {}
        for n, start in enumerate(starts):
            k_blk = k_ref[pl.ds(start, blk), :]
            if kaug_ref is not None:
                k_blk = jnp.concatenate([k_blk, kaug_ref[pl.ds(start, blk), :]], axis=1)
            for c, ls in enumerate(lanes):
                scores[n, c] = _dot_nt(k_blk[:n_keys(n, ls)], q_both[ls, :])
        for n, (_, bias_idx, masked) in enumerate(items):
            for c, ls in enumerate(lanes):
                s = scores.pop((n, c))
                keys = s.shape[0]
                if bias_idx is not None and bias_ref is not None:
                    s = bias_ref[bias_idx, :keys, ls] + s
                if masked:
                    key = lax.broadcasted_iota(jnp.int32, s.shape, 0)
                    qry = (lax.broadcasted_iota(jnp.int32, s.shape, 1) + ls.start) & (blk - 1)
                    s = jnp.where(key <= qry, s, NEG_BIG)
                m_prev = m_ref[:, ls]
                m_new = jnp.maximum(m_prev, jnp.max(s, axis=0, keepdims=True))
                alpha = jnp.exp2(m_prev - m_new)
                p = jnp.exp2(s - m_new)
                l_ref[:, ls] = alpha * l_ref[:, ls] + jnp.sum(p, axis=0, keepdims=True)
                m_ref[:, ls] = m_new
                acc_ref[:, ls] = alpha * acc_ref[:, ls] + _dot(
                    vt_ref[:, pl.ds(starts[n], keys)], p.astype(BF16))

    n_far = jnp.maximum(qi - 1, 0)

    def far_group(j, carry):
        run([(ATTN_GROUP * j + n, None, False) for n in range(ATTN_GROUP)])
        return carry

    lax.fori_loop(0, n_far // ATTN_GROUP, far_group, 0)
    prev, diag = (qi - 1, 1, False), (qi, 0, True)

    @pl.when(qi == 0)
    def _():
        run([diag])

    for left in range(ATTN_GROUP):
        @pl.when((qi >= 1) & (n_far % ATTN_GROUP == left))
        def _():
            run([(n_far - left + n, None, False) for n in range(left)] + [prev, diag])

    halves = (slice(None), slice(0, blk)), (slice(None), slice(blk, 2 * blk))
    return [acc_ref[sl] / l_ref[sl] for sl in halves]


def _diff_attn_kernel(scal_ref, q_ref, k_ref, vt_ref, bias_ref, subln_ref, o_ref,
                      m_ref, l_ref, acc_ref):
    blk = bias_ref.shape[1]
    lam = scal_ref[0]
    out_scale = scal_ref[1]

    def query_block(qi, carry):
        rows = pl.ds(pl.multiple_of(qi * blk, blk), blk)
        q = q_ref[rows, :]
        lane = lax.broadcasted_iota(jnp.int32, q.shape, 1)
        zero = jnp.zeros_like(q)
        q_both = jnp.concatenate([jnp.where(lane < HEAD_DIM, q, zero),
                                  jnp.where(lane < HEAD_DIM, zero, q)], axis=0)
        o1, o2 = _attn_sweep(qi, q_both, k_ref, None, vt_ref, bias_ref, m_ref, l_ref, acc_ref)
        o = o1 - lam * o2
        ms = jnp.mean(o * o, axis=0, keepdims=True)
        o = o * lax.rsqrt(ms + NORM_EPS) * subln_ref[...] * out_scale
        o_ref[rows, :] = o.T.astype(o_ref.dtype)
        return carry

    lax.fori_loop(0, q_ref.shape[0] // blk, query_block, 0)


def _fox_attn_kernel(blk, q_ref, qaug_ref, k_ref, kaug_ref, vt_ref, o_ref, m_ref, l_ref, acc_ref):
    decay = LANES + 12 * pl.program_id(1)

    def query_block(qi, carry):
        rows = pl.ds(pl.multiple_of(qi * blk, blk), blk)
        q = jnp.concatenate([q_ref[rows, :], qaug_ref[rows, :]], axis=1)
        lane = lax.broadcasted_iota(jnp.int32, q.shape, 1)
        zero = jnp.zeros_like(q)
        in_a = (lane < HEAD_DIM) | ((lane >= decay) & (lane < decay + 6))
        in_b = (((lane >= HEAD_DIM) & (lane < LANES))
                | ((lane >= decay + 6) & (lane < decay + 12)))
        q_both = jnp.concatenate([jnp.where(in_a, q, zero), jnp.where(in_b, q, zero)], axis=0)
        o1, o2 = _attn_sweep(qi, q_both, k_ref, kaug_ref, vt_ref, None, m_ref, l_ref, acc_ref)
        row = lax.broadcasted_iota(jnp.int32, o1.shape, 0)
        o_ref[rows, :] = jnp.where(row < HEAD_DIM, o1, o2).T.astype(o_ref.dtype)
        return carry

    lax.fori_loop(0, q_ref.shape[0] // blk, query_block, 0)


def _attn_scratch(tq):
    return [pltpu.VMEM((1, 2 * tq), F32), pltpu.VMEM((1, 2 * tq), F32),
            pltpu.VMEM((LANES, 2 * tq), F32)]


def _seq_block(col):
    return lambda seq: pl.BlockSpec((None, seq, LANES), lambda b, s: (b, 0, col(s)))


def _diff_attn(qk, vt, bias_tiles, scal, subln_col):
    bsz, seq, _ = qk.shape
    tq = min(ATTN_Q, seq)
    return pl.pallas_call(
        _diff_attn_kernel,
        grid=(bsz, N_DIFF_HEADS),
        in_specs=[pl.BlockSpec(memory_space=pltpu.SMEM),
                  _seq_block(lambda s: s)(seq), _seq_block(lambda s: 4 + s)(seq),
                  pl.BlockSpec((None, LANES, seq), lambda b, s: (b, s, 0)),
                  pl.BlockSpec((None, 2, tq, 2 * tq), lambda b, s: (s, 0, 0, 0)),
                  _const_spec((LANES, 1))],
        out_specs=_seq_block(lambda s: s)(seq),
        out_shape=jax.ShapeDtypeStruct((bsz, seq, N_DIFF_HEADS * LANES), BF16),
        scratch_shapes=_attn_scratch(tq),
        compiler_params=_params(2),
        name="diff_attn",
    )(scal, qk, qk, vt, bias_tiles, subln_col)


def _fox_attn(qk, aug, vt):
    bsz, seq, _ = qk.shape
    tq = min(ATTN_Q, seq)
    return pl.pallas_call(
        functools.partial(_fox_attn_kernel, tq),
        grid=(bsz, N_FOX_HEADS // 2),
        in_specs=[_seq_block(lambda s: 8 + s)(seq), _seq_block(lambda s: 0)(seq),
                  _seq_block(lambda s: 12 + s)(seq), _seq_block(lambda s: 1)(seq),
                  pl.BlockSpec((None, LANES, seq), lambda b, s: (b, 4 + s, 0))],
        out_specs=_seq_block(lambda s: s)(seq),
        out_shape=jax.ShapeDtypeStruct((bsz, seq, N_FOX_HEADS * HEAD_DIM), BF16),
        scratch_shapes=_attn_scratch(tq),
        compiler_params=_params(2),
        name="fox_attn",
    )(qk, aug, qk, aug, vt)


def _outproj_kernel(x_ref, od_ref, of_ref, wd_ref, wf_ref, o_ref):
    o_ref[...] = x_ref[...] + _dot(od_ref[...], wd_ref[...]) + _dot(of_ref[...], wf_ref[...])


def _outproj(x, od, of, w_d, w_f):
    t = x.shape[0]
    tm = min(FFN_TILE, t)
    half = od.shape[1]
    return pl.pallas_call(
        _outproj_kernel,
        grid=(t // tm,),
        in_specs=[pl.BlockSpec((tm, D_MODEL), lambda i: (i, 0)),
                  pl.BlockSpec((tm, half), lambda i: (i, 0)),
                  pl.BlockSpec((tm, half), lambda i: (i, 0)),
                  _const_spec((half, D_MODEL)), _const_spec((half, D_MODEL))],
        out_specs=pl.BlockSpec((tm, D_MODEL), lambda i: (i, 0)),
        out_shape=jax.ShapeDtypeStruct((t, D_MODEL), F32),
        compiler_params=_params(1),
        name="attn_outproj",
    )(x, od, of, w_d, w_f)


def _rel_bias_by_distance(rel_bias, n_dist):
    n = jnp.arange(n_dist, dtype=jnp.int32)
    max_exact = N_REL_BUCKETS // 2
    nf = jnp.maximum(n, 1).astype(F32)
    large = max_exact + (jnp.log(nf / max_exact) / math.log(REL_MAX_DIST / max_exact)
                         * (N_REL_BUCKETS - max_exact)).astype(jnp.int32)
    large = jnp.minimum(large, N_REL_BUCKETS - 1)
    return rel_bias[jnp.where(n < max_exact, n, large)]


def _toeplitz(w, n):
    heads, period = w.shape
    flat = jnp.tile(w, (1, n))[:, :n * (period - 1)]
    return flat.reshape(heads, n, period - 1)[:, :, :n]


def _diff_bias_tiles(rel_bias, tq):
    by_dist = _rel_bias_by_distance(rel_bias.astype(F32), 2 * tq)
    by_dist = ((by_dist - by_dist[2 * tq - 1]) * LOG2E).T
    diag = _toeplitz(by_dist, tq)
    prev = _toeplitz(jnp.roll(by_dist, -tq, axis=1), tq)
    tiles = jnp.stack([diag, prev], axis=1)
    return jnp.concatenate([tiles, tiles], axis=-1)


def _attn_layer(x, h, bsz, seq, layer_idx, w_in, w_out, fg_bias, dq_g, dk_g, lq1, lk1, lq2, lk2,
                subln_g, fq_g, fk_g, bias_tiles):
    tq = min(ATTN_Q, seq)
    q_scale = HEAD_DIM ** -0.5 * LOG2E
    w_bf = w_in.astype(BF16)
    w_qk = jnp.concatenate([w_bf[:, 0:1024], w_bf[:, 1536:2560]], axis=1)
    w_v = jnp.concatenate([w_bf[:, 1024:1536], w_bf[:, 2560:3072]], axis=1).T
    w_f = jnp.pad(w_bf[:, 3072:], ((0, 0), (0, LANES - N_FOX_HEADS)))
    fgb = jnp.pad(fg_bias.astype(F32), (0, LANES - N_FOX_HEADS)).reshape(1, LANES)
    gains = jnp.stack([jnp.tile(dq_g, 8) * q_scale, jnp.tile(dk_g, 8),
                       jnp.tile(fq_g, 8) * q_scale, jnp.tile(fk_g, 8)]).astype(F32)
    qk, vt, aug = _inproj(h, w_qk, w_v, w_f, gains, fgb, bsz, seq)
    qk = qk.reshape(bsz, seq, QK_COLS)

    lam_init = 0.8 - 0.6 * math.exp(-0.3 * layer_idx)
    lam = (jnp.exp(jnp.sum(lq1.astype(F32) * lk1.astype(F32)))
           - jnp.exp(jnp.sum(lq2.astype(F32) * lk2.astype(F32))) + lam_init)
    scal = jnp.stack([lam, jnp.asarray(1.0 - lam_init, F32)]).astype(F32)
    od = _diff_attn(qk, vt, bias_tiles, scal, subln_g.astype(F32).reshape(LANES, 1))

    of = _fox_attn(qk, aug.reshape(bsz, seq, AUG_COLS), vt)

    w_o = w_out.astype(BF16)
    half = N_DIFF_HEADS * LANES
    return _outproj(x, od.reshape(bsz * seq, half), of.reshape(bsz * seq, half),
                    w_o[:half], w_o[half:])


def _s5_kernel(n_chunks, *refs):
    x_refs = refs[:S5_CHUNK]
    kblk_ref, p_ref, q_ref, a_ref, y_ref, z_ref, xp_ref, t_ref = refs[S5_CHUNK:]

    @pl.when(pl.program_id(1) == 0)
    def _():
        t_ref[...] = jnp.zeros_like(t_ref)
        for s in range(S5_CHUNK):
            for t in range(s, S5_CHUNK):
                t_ref[s * LANES:(s + 1) * LANES, t * LANES:(t + 1) * LANES] = kblk_ref[t - s]

    x = jnp.concatenate([r[...] for r in x_refs], axis=1)
    n_rows = x.shape[0]
    n_seq = n_rows // n_chunks
    half = z_ref.shape[1] // 2
    z_ref[...] = _dot(x, p_ref[...])
    a_re = a_ref[0:1, :]
    a_im = a_ref[1:2, :]

    def scan(c, carry):
        out = []
        for b in range(n_seq):
            x_re, x_im = carry[2 * b], carry[2 * b + 1]
            row = pl.ds(b * n_chunks + c, 1)
            xp_ref[row, :half] = x_re
            xp_ref[row, half:] = x_im
            z = z_ref[row, :]
            out += [a_re * x_re - a_im * x_im + z[:, :half],
                    a_re * x_im + a_im * x_re + z[:, half:]]
        return tuple(out)

    zero = jnp.zeros((1, half), F32)
    lax.fori_loop(0, n_chunks, scan, (zero,) * (2 * n_seq))
    inter = _dot(xp_ref[...].astype(BF16), q_ref[...])
    for j in range(S5_CHUNK // 2):
        cols = slice(2 * j * LANES, (2 * j + 2) * LANES)
        y = _dot(x[:, :cols.stop], t_ref[:cols.stop, cols]) + inter[:, cols]
        y_ref[pl.ds(2 * j, n_rows, stride=S5_CHUNK), :] = y[:, :LANES]
        y_ref[pl.ds(2 * j + 1, n_rows, stride=S5_CHUNK), :] = y[:, LANES:]


def _s5_operators(a_re, a_im, log_step, b_re, b_im, c_re, c_im):
    hp = lax.Precision.HIGHEST
    L, kb, gb = S5_CHUNK, S5_GROUPS // 8, 8
    step = jnp.exp(log_step.astype(F32))[:, None]
    ar, ai = a_re.astype(F32), a_im.astype(F32)
    mag = jnp.exp(ar * step)
    lr, li = mag * jnp.cos(ai * step), mag * jnp.sin(ai * step)
    den = ar * ar + ai * ai
    cr = (((lr - 1.0) * ar + li * ai) / den)[:, None, :]
    ci = ((li * ar - (lr - 1.0) * ai) / den)[:, None, :]
    br, bi = b_re.astype(F32).transpose(0, 2, 1), b_im.astype(F32).transpose(0, 2, 1)
    bbr, bbi = cr * br - ci * bi, cr * bi + ci * br
    pr, pi = [jnp.ones_like(lr)], [jnp.zeros_like(li)]
    for _ in range(L):
        pr, pi = pr + [pr[-1] * lr - pi[-1] * li], pi + [pr[-1] * li + pi[-1] * lr]
    pr, pi = jnp.stack(pr), jnp.stack(pi)
    wr = pr[:L, :, None, :] * bbr - pi[:L, :, None, :] * bbi
    wi = pr[:L, :, None, :] * bbi + pi[:L, :, None, :] * bbr
    cre, cim = c_re.astype(F32), c_im.astype(F32)
    kern = (jnp.einsum('gcp,ngap->ngac', cre, wr, precision=hp)
            - jnp.einsum('gcp,ngap->ngac', cim, wi, precision=hp))

    def group_mask(rows, rows_per_group, cols, cols_per_group):
        r = lax.broadcasted_iota(jnp.int32, (rows, 1), 0) // rows_per_group
        c = lax.broadcasted_iota(jnp.int32, (1, cols), 1) // cols_per_group
        return r == c

    def spread_channels(m):
        src = lax.broadcasted_iota(jnp.int32, (S5_GROUP, LANES), 0)
        dst = lax.broadcasted_iota(jnp.int32, (S5_GROUP, LANES), 1)
        return jnp.dot(m.astype(BF16), (src == dst % S5_GROUP).astype(BF16),
                       preferred_element_type=BF16)

    zero = jnp.zeros((), BF16)
    k_blk = kern.reshape(L, kb, LANES, S5_GROUP).transpose(1, 0, 2, 3)
    k_blk = jnp.where(group_mask(LANES, S5_GROUP, LANES, S5_GROUP), spread_channels(k_blk), zero)

    def state_in(w):
        w = jnp.tile(w[::-1].astype(BF16).reshape(L, kb, LANES, S5_STATE), (1, 1, 1, gb))
        w = jnp.where(group_mask(LANES, S5_GROUP, gb * S5_STATE, S5_STATE), w, zero)
        return w.transpose(1, 0, 2, 3).reshape(kb, L * LANES, gb * S5_STATE)
    p_big = jnp.concatenate([state_in(wr), state_in(wi)], axis=2)

    cre_t, cim_t = cre.transpose(0, 2, 1), cim.transpose(0, 2, 1)
    def state_out(m):
        m = spread_channels(m.reshape(L, kb, gb * S5_STATE, S5_GROUP))
        m = jnp.where(group_mask(gb * S5_STATE, S5_STATE, LANES, S5_GROUP), m, zero)
        return m.transpose(1, 2, 0, 3).reshape(kb, gb * S5_STATE, L * LANES)
    q_re = cre_t[None] * pr[1:, :, :, None] - cim_t[None] * pi[1:, :, :, None]
    q_im = -(cre_t[None] * pi[1:, :, :, None] + cim_t[None] * pr[1:, :, :, None])
    q_big = jnp.concatenate([state_out(q_re), state_out(q_im)], axis=1)
    a_big = jnp.stack([pr[L].reshape(kb, gb * S5_STATE), pi[L].reshape(kb, gb * S5_STATE)], axis=1)
    return k_blk, p_big, q_big, a_big


def _s5_core(h2, bsz, seq, a_re, a_im, log_step, b_re, b_im, c_re, c_im):
    L = S5_CHUNK
    n_chunks = seq // L
    rows = bsz * n_chunks
    kb = D_MODEL // LANES
    rb = min(S5_ROWS, rows)
    assert rb % n_chunks == 0
    k_blk, p_big, q_big, a_big = _s5_operators(a_re, a_im, log_step, b_re, b_im, c_re, c_im)
    n_state = 8 * S5_STATE
    x_specs = [pl.BlockSpec((rb, LANES), lambda k, r, t=t: (r, t * kb + k)) for t in range(L)]
    return pl.pallas_call(
        functools.partial(_s5_kernel, n_chunks),
        grid=(kb, rows // rb),
        in_specs=x_specs + [
            pl.BlockSpec((None, L, LANES, LANES), lambda k, r: (k, 0, 0, 0)),
            pl.BlockSpec((None, L * LANES, 2 * n_state), lambda k, r: (k, 0, 0)),
            pl.BlockSpec((None, 2 * n_state, L * LANES), lambda k, r: (k, 0, 0)),
            pl.BlockSpec((None, 2, n_state), lambda k, r: (k, 0, 0))],
        out_specs=pl.BlockSpec((rb * L, LANES), lambda k, r: (r, k)),
        out_shape=jax.ShapeDtypeStruct((rows * L, D_MODEL), F32),
        scratch_shapes=[pltpu.VMEM((rb, 2 * n_state), F32), pltpu.VMEM((rb, 2 * n_state), F32),
                        pltpu.VMEM((L * LANES, L * LANES), BF16)],
        compiler_params=_params(2),
        name="s5_core",
    )(*([h2] * L), k_blk, p_big, q_big, a_big)


def _glu_kernel(x_ref, y_ref, g_ref, d_ref, wa_ref, wb_ref, o_ref):
    x = x_ref[...]
    y = y_ref[...] + d_ref[...] * _rms(x, g_ref[...])
    y = jax.nn.gelu(y).astype(BF16)
    o_ref[...] = x + _dot(y, wa_ref[...]) * jax.nn.sigmoid(_dot(y, wb_ref[...]))


def _s5_layer(x, h2, bsz, seq, mix_g, a_re, a_im, log_step, b_re, b_im, c_re, c_im, d_skip,
              w_a, w_b):
    t = x.shape[0]
    tm = min(FFN_TILE, t)
    y = _s5_core(h2, bsz, seq, a_re, a_im, log_step, b_re, b_im, c_re, c_im)
    row = pl.BlockSpec((tm, D_MODEL), lambda i: (i, 0))
    return pl.pallas_call(
        _glu_kernel,
        grid=(t // tm,),
        in_specs=[row, row, _const_spec((1, D_MODEL)), _const_spec((1, D_MODEL)),
                  _const_spec((D_MODEL, D_MODEL)), _const_spec((D_MODEL, D_MODEL))],
        out_specs=row,
        out_shape=jax.ShapeDtypeStruct((t, D_MODEL), F32),
        compiler_params=_params(1),
        name="s5_glu",
    )(x, y, mix_g.astype(F32).reshape(1, D_MODEL), d_skip.astype(F32).reshape(1, D_MODEL),
      w_a.astype(BF16), w_b.astype(BF16))


def kernel(x, ffn1_norm, ffn1_gate, ffn1_up, ffn1_down, mix_norm, ffn2_norm, ffn2_gate, ffn2_up, ffn2_down, attn_w_in, attn_w_out, fg_bias, diff_q_norm, diff_k_norm, diff_lambda_q1, diff_lambda_k1, diff_lambda_q2, diff_lambda_k2, diff_subln, fox_q_norm, fox_k_norm, rel_bias, s5_a_re, s5_a_im, s5_log_step, s5_b_re, s5_b_im, s5_c_re, s5_c_im, s5_d, s5_glu_a, s5_glu_b):
    bsz, seq, _ = x.shape
    depth = ffn1_norm.shape[0]
    assert min(ATTN_Q, seq) >= REL_MAX_DIST, "bias tiles assume the bias saturates within a block"
    bias_tiles = _diff_bias_tiles(rel_bias, min(ATTN_Q, seq))
    xt = x.reshape(bsz * seq, D_MODEL).astype(F32)
    for i in range(depth):
        xt, h = _ffn(xt, ffn1_norm[i], ffn1_gate[i].astype(BF16), ffn1_up[i].astype(BF16),
                     ffn1_down[i].astype(BF16), g_next=mix_norm[i], chunked=i % 2 == 1)
        j = i // 2
        if i % 2 == 0:
            xt = _attn_layer(xt, h, bsz, seq, i, attn_w_in[j], attn_w_out[j], fg_bias[j],
                             diff_q_norm[j], diff_k_norm[j], diff_lambda_q1[j], diff_lambda_k1[j],
                             diff_lambda_q2[j], diff_lambda_k2[j], diff_subln[j], fox_q_norm[j],
                             fox_k_norm[j], bias_tiles)
        else:
            xt = _s5_layer(xt, h, bsz, seq, mix_norm[i], s5_a_re[j], s5_a_im[j], s5_log_step[j],
                           s5_b_re[j], s5_b_im[j], s5_c_re[j], s5_c_im[j], s5_d[j],
                           s5_glu_a[j], s5_glu_b[j])
        xt, _ = _ffn(xt, ffn2_norm[i], ffn2_gate[i].astype(BF16), ffn2_up[i].astype(BF16),
                     ffn2_down[i].astype(BF16))
    return xt.reshape(bsz, seq, D_MODEL).astype(x.dtype)
```

```python
import functools
import math

import jax
import jax.numpy as jnp
import numpy as np
from jax import lax
from jax.experimental import pallas as pl
from jax.experimental.pallas import tpu as pltpu

D_MODEL = 1024
D_FF = 2816
HEAD_DIM = 64
N_DIFF_HEADS = 4
N_FOX_HEADS = 8
N_REL_BUCKETS = 32
REL_MAX_DIST = 128
S5_GROUP = 16
S5_GROUPS = D_MODEL // S5_GROUP
S5_STATE = 64
NORM_EPS = 1e-6

QK_COLS = 2048
V_COLS = 1024
AUG_COLS = 256
LANES = 128
MXU_DIM = 256
VMEM_LIMIT = 56 * 1024 * 1024

TOKEN_TILE = 512
FFN_TILE = 1024
FFN_SPLITS = (0, 6 * MXU_DIM, D_FF)
ATTN_Q = 512
ATTN_LANES = 256
ATTN_GROUP = 6
S5_CHUNK = 16
S5_ROWS = 512
NEG_BIG = -1e30
LOG2E = 1.4426950408889634

BF16 = jnp.bfloat16
F32 = jnp.float32


def _dot(a, b):
    return jnp.dot(a, b, preferred_element_type=F32)


def _dot_nt(a, b):
    return lax.dot_general(a, b, (((1,), (1,)), ((), ())), preferred_element_type=F32)


def _rms(x, g):
    ms = jnp.mean(x * x, axis=-1, keepdims=True)
    return x * lax.rsqrt(ms + NORM_EPS) * g


def _const_spec(shape):
    return pl.BlockSpec(shape, lambda *_: (0,) * len(shape), pipeline_mode=pl.Buffered(1))


def _params(n_axes):
    return pltpu.CompilerParams(dimension_semantics=("arbitrary",) * n_axes,
                                vmem_limit_bytes=VMEM_LIMIT)


def _split3(x):
    p0 = x.astype(BF16)
    r1 = x - p0.astype(F32)
    p1 = r1.astype(BF16)
    p2 = (r1 - p1.astype(F32)).astype(BF16)
    return p0, p1, p2


def _ffn_kernel(chunked, x_ref, g_ref, gn_ref, wg_ref, wu_ref, wd_ref, o_ref, *rest):
    x = x_ref[...]
    h = _rms(x, g_ref[...]).astype(BF16)
    acc = None
    for lo, hi in zip(FFN_SPLITS[:-1], FFN_SPLITS[1:]):
        sl = slice(lo, hi)
        a = _dot(h, wg_ref[:, sl])
        b = _dot(h, wu_ref[:, sl])
        act = (a * jax.nn.sigmoid(a) * b).astype(BF16)
        d = _dot(act, wd_ref[sl, :])
        acc = d if acc is None else acc + d
    xo = x + 0.5 * acc
    o_ref[...] = xo
    if not rest:
        return
    hn = _rms(xo, gn_ref[...])
    if not chunked:
        rest[0][...] = hn.astype(BF16)
        return
    h_ref, stage_ref = rest
    n_rows = h_ref.shape[0]
    for k in range(D_MODEL // LANES):
        stage_ref[k] = hn[:, k * LANES:(k + 1) * LANES]
    for t in range(S5_CHUNK):
        for k in range(D_MODEL // LANES):
            lo = t * D_MODEL + k * LANES
            h_ref[:, lo:lo + LANES] = stage_ref[k, pl.ds(t, n_rows, stride=S5_CHUNK), :].astype(BF16)


def _ffn(x, g, wg, wu, wd, g_next=None, chunked=False):
    t = x.shape[0]
    tm = min(FFN_TILE, t)
    emit = g_next is not None
    gn = g_next if emit else g
    row = pl.BlockSpec((tm, D_MODEL), lambda i: (i, 0))
    out_shape = [jax.ShapeDtypeStruct((t, D_MODEL), F32)]
    out_specs = [row]
    scratch = []
    if emit and chunked:
        out_shape.append(jax.ShapeDtypeStruct((t // S5_CHUNK, S5_CHUNK * D_MODEL), BF16))
        out_specs.append(pl.BlockSpec((tm // S5_CHUNK, S5_CHUNK * D_MODEL), lambda i: (i, 0)))
        scratch = [pltpu.VMEM((D_MODEL // LANES, tm, LANES), F32)]
    elif emit:
        out_shape.append(jax.ShapeDtypeStruct((t, D_MODEL), BF16))
        out_specs.append(row)
    res = pl.pallas_call(
        functools.partial(_ffn_kernel, chunked),
        grid=(t // tm,),
        in_specs=[row, _const_spec((1, D_MODEL)), _const_spec((1, D_MODEL)),
                  _const_spec((D_MODEL, D_FF)), _const_spec((D_MODEL, D_FF)),
                  _const_spec((D_FF, D_MODEL))],
        out_specs=out_specs,
        out_shape=out_shape,
        scratch_shapes=scratch,
        compiler_params=_params(1),
        name="ffn",
    )(x, g.reshape(1, D_MODEL), gn.reshape(1, D_MODEL), wg, wu, wd)
    return (res[0], res[1]) if emit else (res[0], None)


def _inproj_kernel(tiles_per_seq, h_ref, wqk_ref, wv_ref, wf_ref, gains_ref, ones_ref, tri_ref,
                   fgb_ref, sel_ref, aug1_ref, qk_ref, vt_ref, aug_ref, carry_ref):
    i = pl.program_id(0)
    h = h_ref[...]
    ones_blk = ones_ref[...]
    for c in range(QK_COLS // 512):
        y = _dot(h, wqk_ref[:, c * 512:(c + 1) * 512])
        y2 = y * y
        hi = y2.astype(BF16)
        lo = (y2 - hi.astype(F32)).astype(BF16)
        ssq = jnp.concatenate(
            [_dot(hi[:, k * MXU_DIM:(k + 1) * MXU_DIM], ones_blk)
             + _dot(lo[:, k * MXU_DIM:(k + 1) * MXU_DIM], ones_blk)
             for k in range(512 // MXU_DIM)], axis=1)
        y = y * lax.rsqrt(ssq * (1.0 / HEAD_DIM) + NORM_EPS) * gains_ref[c:c + 1, :]
        qk_ref[:, c * 512:(c + 1) * 512] = y.astype(BF16)
    for c in range(V_COLS // 512):
        vt = _dot_nt(wv_ref[c * 512:(c + 1) * 512, :], h)
        vt_ref[c * 512:(c + 1) * 512, :] = vt.astype(BF16)

    fl = _dot(h, wf_ref[...]) + fgb_ref[...]
    logf = -(jnp.maximum(-fl, 0.0) + jnp.log1p(jnp.exp(-jnp.abs(fl))))

    @pl.when(i % tiles_per_seq == 0)
    def _():
        carry_ref[...] = jnp.zeros_like(carry_ref)

    tri = tri_ref[...]
    cum = sum(_dot(tri, p) for p in _split3(logf)) + carry_ref[...]
    carry_ref[...] = cum[cum.shape[0] - 1:, :]
    aug = sum(_dot(p, sel_ref[n]) for n, p in enumerate(_split3(cum * LOG2E))) + aug1_ref[...]
    aug_ref[...] = aug.astype(BF16)


def _decay_lane_maps():
    sel = np.zeros((3, LANES, AUG_COLS), np.float32)
    ones = np.zeros((1, AUG_COLS), np.float32)
    k_off = AUG_COLS // 2
    for head in range(N_FOX_HEADS):
        base = head * 6
        for n in range(3):
            sel[n, head, base + n] = 1.0
            sel[n, head, k_off + base + 3 + n] = -1.0
            ones[0, base + 3 + n] = 1.0
            ones[0, k_off + base + n] = 1.0
    return jnp.asarray(sel, BF16), jnp.asarray(ones, F32)


def _inproj(h, w_qk, w_v, w_f, gains, fg_bias_row, bsz, seq):
    t = h.shape[0]
    tm = min(TOKEN_TILE, seq)
    tps = seq // tm
    ones_blk = jnp.kron(jnp.eye(MXU_DIM // HEAD_DIM, dtype=F32),
                        jnp.ones((HEAD_DIM, HEAD_DIM), F32)).astype(BF16)
    tri = jnp.tril(jnp.ones((tm, tm), F32)).astype(BF16)
    sel, aug_ones = _decay_lane_maps()
    return pl.pallas_call(
        functools.partial(_inproj_kernel, tps),
        grid=(t // tm,),
        in_specs=[pl.BlockSpec((tm, D_MODEL), lambda i: (i, 0)),
                  _const_spec((D_MODEL, QK_COLS)), _const_spec((D_MODEL, V_COLS)),
                  _const_spec((D_MODEL, LANES)),
                  _const_spec((QK_COLS // 512, 512)), _const_spec((MXU_DIM, MXU_DIM)),
                  _const_spec((tm, tm)), _const_spec((1, LANES)),
                  _const_spec((3, LANES, AUG_COLS)), _const_spec((1, AUG_COLS))],
        out_specs=[pl.BlockSpec((tm, QK_COLS), lambda i: (i, 0)),
                   pl.BlockSpec((None, V_COLS, tm), lambda i: (i // tps, 0, i % tps)),
                   pl.BlockSpec((tm, AUG_COLS), lambda i: (i, 0))],
        out_shape=[jax.ShapeDtypeStruct((t, QK_COLS), BF16),
                   jax.ShapeDtypeStruct((bsz, V_COLS, seq), BF16),
                   jax.ShapeDtypeStruct((t, AUG_COLS), BF16)],
        scratch_shapes=[pltpu.VMEM((1, LANES), F32)],
        compiler_params=_params(1),
        name="attn_inproj",
    )(h, w_qk, w_v, w_f, gains, ones_blk, tri, fg_bias_row, sel, aug_ones)


def _attn_sweep(qi, q_both, k_ref, kaug_ref, vt_ref, bias_ref, m_ref, l_ref, acc_ref):
    blk = q_both.shape[0] // 2
    m_ref[...] = jnp.full_like(m_ref, NEG_BIG)
    l_ref[...] = jnp.zeros_like(l_ref)
    acc_ref[...] = jnp.zeros_like(acc_ref)

    def run(items):
        starts = [pl.multiple_of(j * blk, blk) for j, _, _ in items]
        lanes = [slice(c * ATTN_LANES, (c + 1) * ATTN_LANES)
                 for c in range(2 * blk // ATTN_LANES)]

        def n_keys(n, ls):
            first_half = (ls.start % blk) + ATTN_LANES <= blk // 2
            return blk // 2 if items[n][2] and first_half else blk

        scores = {}
        for n, start in enumerate(starts):
            k_blk = k_ref[pl.ds(start, blk), :]
            if kaug_ref is not None:
                k_blk = jnp.concatenate([k_blk, kaug_ref[pl.ds(start, blk), :]], axis=1)
            for c, ls in enumerate(lanes):
                scores[n, c] = _dot_nt(k_blk[:n_keys(n, ls)], q_both[ls, :])
        for n, (_, bias_idx, masked) in enumerate(items):
            for c, ls in enumerate(lanes):
                s = scores.pop((n, c))
                keys = s.shape[0]
                if bias_idx is not None and bias_ref is not None:
                    s = bias_ref[bias_idx, :keys, ls] + s
                if masked:
                    key = lax.broadcasted_iota(jnp.int32, s.shape, 0)
                    qry = (lax.broadcasted_iota(jnp.int32, s.shape, 1) + ls.start) & (blk - 1)
                    s = jnp.where(key <= qry, s, NEG_BIG)
                m_prev = m_ref[:, ls]
                m_new = jnp.maximum(m_prev, jnp.max(s, axis=0, keepdims=True))
                alpha = jnp.exp2(m_prev - m_new)
                p = jnp.exp2(s - m_new)
                l_ref[:, ls] = alpha * l_ref[:, ls] + jnp.sum(p, axis=0, keepdims=True)
                m_ref[:, ls] = m_new
                acc_ref[:, ls] = alpha * acc_ref[:, ls] + _dot(
                    vt_ref[:, pl.ds(starts[n], keys)], p.astype(BF16))

    n_far = jnp.maximum(qi - 1, 0)

    def far_group(j, carry):
        run([(ATTN_GROUP * j + n, None, False) for n in range(ATTN_GROUP)])
        return carry

    lax.fori_loop(0, n_far // ATTN_GROUP, far_group, 0)
    prev, diag = (qi - 1, 1, False), (qi, 0, True)

    @pl.when(qi == 0)
    def _():
        run([diag])

    for left in range(ATTN_GROUP):
        @pl.when((qi >= 1) & (n_far % ATTN_GROUP == left))
        def _():
            run([(n_far - left + n, None, False) for n in range(left)] + [prev, diag])

    halves = (slice(None), slice(0, blk)), (slice(None), slice(blk, 2 * blk))
    return [acc_ref[sl] / l_ref[sl] for sl in halves]


def _diff_attn_kernel(scal_ref, q_ref, k_ref, vt_ref, bias_ref, subln_ref, o_ref,
                      m_ref, l_ref, acc_ref):
    blk = bias_ref.shape[1]
    lam = scal_ref[0]
    out_scale = scal_ref[1]

    def query_block(qi, carry):
        rows = pl.ds(pl.multiple_of(qi * blk, blk), blk)
        q = q_ref[rows, :]
        lane = lax.broadcasted_iota(jnp.int32, q.shape, 1)
        zero = jnp.zeros_like(q)
        q_both = jnp.concatenate([jnp.where(lane < HEAD_DIM, q, zero),
                                  jnp.where(lane < HEAD_DIM, zero, q)], axis=0)
        o1, o2 = _attn_sweep(qi, q_both, k_ref, None, vt_ref, bias_ref, m_ref, l_ref, acc_ref)
        o = o1 - lam * o2
        ms = jnp.mean(o * o, axis=0, keepdims=True)
        o = o * lax.rsqrt(ms + NORM_EPS) * subln_ref[...] * out_scale
        o_ref[rows, :] = o.T.astype(o_ref.dtype)
        return carry

    lax.fori_loop(0, q_ref.shape[0] // blk, query_block, 0)


def _fox_attn_kernel(blk, q_ref, qaug_ref, k_ref, kaug_ref, vt_ref, o_ref, m_ref, l_ref, acc_ref):
    decay = LANES + 12 * pl.program_id(1)

    def query_block(qi, carry):
        rows = pl.ds(pl.multiple_of(qi * blk, blk), blk)
        q = jnp.concatenate([q_ref[rows, :], qaug_ref[rows, :]], axis=1)
        lane = lax.broadcasted_iota(jnp.int32, q.shape, 1)
        zero = jnp.zeros_like(q)
        in_a = (lane < HEAD_DIM) | ((lane >= decay) & (lane < decay + 6))
        in_b = (((lane >= HEAD_DIM) & (lane < LANES))
                | ((lane >= decay + 6) & (lane < decay + 12)))
        q_both = jnp.concatenate([jnp.where(in_a, q, zero), jnp.where(in_b, q, zero)], axis=0)
        o1, o2 = _attn_sweep(qi, q_both, k_ref, kaug_ref, vt_ref, None, m_ref, l_ref, acc_ref)
        row = lax.broadcasted_iota(jnp.int32, o1.shape, 0)
        o_ref[rows, :] = jnp.where(row < HEAD_DIM, o1, o2).T.astype(o_ref.dtype)
        return carry

    lax.fori_loop(0, q_ref.shape[0] // blk, query_block, 0)


def _attn_scratch(tq):
    return [pltpu.VMEM((1, 2 * tq), F32), pltpu.VMEM((1, 2 * tq), F32),
            pltpu.VMEM((LANES, 2 * tq), F32)]


def _seq_block(col):
    return lambda seq: pl.BlockSpec((None, seq, LANES), lambda b, s: (b, 0, col(s)))


def _diff_attn(qk, vt, bias_tiles, scal, subln_col):
    bsz, seq, _ = qk.shape
    tq = min(ATTN_Q, seq)
    return pl.pallas_call(
        _diff_attn_kernel,
        grid=(bsz, N_DIFF_HEADS),
        in_specs=[pl.BlockSpec(memory_space=pltpu.SMEM),
                  _seq_block(lambda s: s)(seq), _seq_block(lambda s: 4 + s)(seq),
                  pl.BlockSpec((None, LANES, seq), lambda b, s: (b, s, 0)),
                  pl.BlockSpec((None, 2, tq, 2 * tq), lambda b, s: (s, 0, 0, 0)),
                  _const_spec((LANES, 1))],
        out_specs=_seq_block(lambda s: s)(seq),
        out_shape=jax.ShapeDtypeStruct((bsz, seq, N_DIFF_HEADS * LANES), BF16),
        scratch_shapes=_attn_scratch(tq),
        compiler_params=_params(2),
        name="diff_attn",
    )(scal, qk, qk, vt, bias_tiles, subln_col)


def _fox_attn(qk, aug, vt):
    bsz, seq, _ = qk.shape
    tq = min(ATTN_Q, seq)
    return pl.pallas_call(
        functools.partial(_fox_attn_kernel, tq),
        grid=(bsz, N_FOX_HEADS // 2),
        in_specs=[_seq_block(lambda s: 8 + s)(seq), _seq_block(lambda s: 0)(seq),
                  _seq_block(lambda s: 12 + s)(seq), _seq_block(lambda s: 1)(seq),
                  pl.BlockSpec((None, LANES, seq), lambda b, s: (b, 4 + s, 0))],
        out_specs=_seq_block(lambda s: s)(seq),
        out_shape=jax.ShapeDtypeStruct((bsz, seq, N_FOX_HEADS * HEAD_DIM), BF16),
        scratch_shapes=_attn_scratch(tq),
        compiler_params=_params(2),
        name="fox_attn",
    )(qk, aug, qk, aug, vt)


def _outproj_kernel(x_ref, od_ref, of_ref, wd_ref, wf_ref, o_ref):
    o_ref[...] = x_ref[...] + _dot(od_ref[...], wd_ref[...]) + _dot(of_ref[...], wf_ref[...])


def _outproj(x, od, of, w_d, w_f):
    t = x.shape[0]
    tm = min(FFN_TILE, t)
    half = od.shape[1]
    return pl.pallas_call(
        _outproj_kernel,
        grid=(t // tm,),
        in_specs=[pl.BlockSpec((tm, D_MODEL), lambda i: (i, 0)),
                  pl.BlockSpec((tm, half), lambda i: (i, 0)),
                  pl.BlockSpec((tm, half), lambda i: (i, 0)),
                  _const_spec((half, D_MODEL)), _const_spec((half, D_MODEL))],
        out_specs=pl.BlockSpec((tm, D_MODEL), lambda i: (i, 0)),
        out_shape=jax.ShapeDtypeStruct((t, D_MODEL), F32),
        compiler_params=_params(1),
        name="attn_outproj",
    )(x, od, of, w_d, w_f)


def _rel_bias_by_distance(rel_bias, n_dist):
    n = jnp.arange(n_dist, dtype=jnp.int32)
    max_exact = N_REL_BUCKETS // 2
    nf = jnp.maximum(n, 1).astype(F32)
    large = max_exact + (jnp.log(nf / max_exact) / math.log(REL_MAX_DIST / max_exact)
                         * (N_REL_BUCKETS - max_exact)).astype(jnp.int32)
    large = jnp.minimum(large, N_REL_BUCKETS - 1)
    return rel_bias[jnp.where(n < max_exact, n, large)]


def _toeplitz(w, n):
    heads, period = w.shape
    flat = jnp.tile(w, (1, n))[:, :n * (period - 1)]
    return flat.reshape(heads, n, period - 1)[:, :, :n]


def _diff_bias_tiles(rel_bias, tq):
    by_dist = _rel_bias_by_distance(rel_bias.astype(F32), 2 * tq)
    by_dist = ((by_dist - by_dist[2 * tq - 1]) * LOG2E).T
    diag = _toeplitz(by_dist, tq)
    prev = _toeplitz(jnp.roll(by_dist, -tq, axis=1), tq)
    tiles = jnp.stack([diag, prev], axis=1)
    return jnp.concatenate([tiles, tiles], axis=-1)


def _attn_layer(x, h, bsz, seq, layer_idx, w_in, w_out, fg_bias, dq_g, dk_g, lq1, lk1, lq2, lk2,
                subln_g, fq_g, fk_g, bias_tiles):
    tq = min(ATTN_Q, seq)
    q_scale = HEAD_DIM ** -0.5 * LOG2E
    w_bf = w_in.astype(BF16)
    w_qk = jnp.concatenate([w_bf[:, 0:1024], w_bf[:, 1536:2560]], axis=1)
    w_v = jnp.concatenate([w_bf[:, 1024:1536], w_bf[:, 2560:3072]], axis=1).T
    w_f = jnp.pad(w_bf[:, 3072:], ((0, 0), (0, LANES - N_FOX_HEADS)))
    fgb = jnp.pad(fg_bias.astype(F32), (0, LANES - N_FOX_HEADS)).reshape(1, LANES)
    gains = jnp.stack([jnp.tile(dq_g, 8) * q_scale, jnp.tile(dk_g, 8),
                       jnp.tile(fq_g, 8) * q_scale, jnp.tile(fk_g, 8)]).astype(F32)
    qk, vt, aug = _inproj(h, w_qk, w_v, w_f, gains, fgb, bsz, seq)
    qk = qk.reshape(bsz, seq, QK_COLS)

    lam_init = 0.8 - 0.6 * math.exp(-0.3 * layer_idx)
    lam = (jnp.exp(jnp.sum(lq1.astype(F32) * lk1.astype(F32)))
           - jnp.exp(jnp.sum(lq2.astype(F32) * lk2.astype(F32))) + lam_init)
    scal = jnp.stack([lam, jnp.asarray(1.0 - lam_init, F32)]).astype(F32)
    od = _diff_attn(qk, vt, bias_tiles, scal, subln_g.astype(F32).reshape(LANES, 1))

    of = _fox_attn(qk, aug.reshape(bsz, seq, AUG_COLS), vt)

    w_o = w_out.astype(BF16)
    half = N_DIFF_HEADS * LANES
    return _outproj(x, od.reshape(bsz * seq, half), of.reshape(bsz * seq, half),
                    w_o[:half], w_o[half:])


def _s5_kernel(n_chunks, *refs):
    x_refs = refs[:S5_CHUNK]
    kblk_ref, p_ref, q_ref, a_ref, y_ref, z_ref, xp_ref, t_ref = refs[S5_CHUNK:]

    @pl.when(pl.program_id(1) == 0)
    def _():
        t_ref[...] = jnp.zeros_like(t_ref)
        for s in range(S5_CHUNK):
            for t in range(s, S5_CHUNK):
                t_ref[s * LANES:(s + 1) * LANES, t * LANES:(t + 1) * LANES] = kblk_ref[t - s]

    x = jnp.concatenate([r[...] for r in x_refs], axis=1)
    n_rows = x.shape[0]
    n_seq = n_rows // n_chunks
    half = z_ref.shape[1] // 2
    z_ref[...] = _dot(x, p_ref[...])
    a_re = a_ref[0:1, :]
    a_im = a_ref[1:2, :]

    def scan(c, carry):
        out = []
        for b in range(n_seq):
            x_re, x_im = carry[2 * b], carry[2 * b + 1]
            row = pl.ds(b * n_chunks + c, 1)
            xp_ref[row, :half] = x_re
            xp_ref[row, half:] = x_im
            z = z_ref[row, :]
            out += [a_re * x_re - a_im * x_im + z[:, :half],
                    a_re * x_im + a_im * x_re + z[:, half:]]
        return tuple(out)

    zero = jnp.zeros((1, half), F32)
    lax.fori_loop(0, n_chunks, scan, (zero,) * (2 * n_seq))
    inter = _dot(xp_ref[...].astype(BF16), q_ref[...])
    for j in range(S5_CHUNK // 2):
        cols = slice(2 * j * LANES, (2 * j + 2) * LANES)
        y = _dot(x[:, :cols.stop], t_ref[:cols.stop, cols]) + inter[:, cols]
        y_ref[pl.ds(2 * j, n_rows, stride=S5_CHUNK), :] = y[:, :LANES]
        y_ref[pl.ds(2 * j + 1, n_rows, stride=S5_CHUNK), :] = y[:, LANES:]


def _s5_operators(a_re, a_im, log_step, b_re, b_im, c_re, c_im):
    hp = lax.Precision.HIGHEST
    L, kb, gb = S5_CHUNK, S5_GROUPS // 8, 8
    step = jnp.exp(log_step.astype(F32))[:, None]
    ar, ai = a_re.astype(F32), a_im.astype(F32)
    mag = jnp.exp(ar * step)
    lr, li = mag * jnp.cos(ai * step), mag * jnp.sin(ai * step)
    den = ar * ar + ai * ai
    cr = (((lr - 1.0) * ar + li * ai) / den)[:, None, :]
    ci = ((li * ar - (lr - 1.0) * ai) / den)[:, None, :]
    br, bi = b_re.astype(F32).transpose(0, 2, 1), b_im.astype(F32).transpose(0, 2, 1)
    bbr, bbi = cr * br - ci * bi, cr * bi + ci * br
    pr, pi = [jnp.ones_like(lr)], [jnp.zeros_like(li)]
    for _ in range(L):
        pr, pi = pr + [pr[-1] * lr - pi[-1] * li], pi + [pr[-1] * li + pi[-1] * lr]
    pr, pi = jnp.stack(pr), jnp.stack(pi)
    wr = pr[:L, :, None, :] * bbr - pi[:L, :, None, :] * bbi
    wi = pr[:L, :, None, :] * bbi + pi[:L, :, None, :] * bbr
    cre, cim = c_re.astype(F32), c_im.astype(F32)
    kern = (jnp.einsum('gcp,ngap->ngac', cre, wr, precision=hp)
            - jnp.einsum('gcp,ngap->ngac', cim, wi, precision=hp))

    def group_mask(rows, rows_per_group, cols, cols_per_group):
        r = lax.broadcasted_iota(jnp.int32, (rows, 1), 0) // rows_per_group
        c = lax.broadcasted_iota(jnp.int32, (1, cols), 1) // cols_per_group
        return r == c

    def spread_channels(m):
        src = lax.broadcasted_iota(jnp.int32, (S5_GROUP, LANES), 0)
        dst = lax.broadcasted_iota(jnp.int32, (S5_GROUP, LANES), 1)
        return jnp.dot(m.astype(BF16), (src == dst % S5_GROUP).astype(BF16),
                       preferred_element_type=BF16)

    zero = jnp.zeros((), BF16)
    k_blk = kern.reshape(L, kb, LANES, S5_GROUP).transpose(1, 0, 2, 3)
    k_blk = jnp.where(group_mask(LANES, S5_GROUP, LANES, S5_GROUP), spread_channels(k_blk), zero)

    def state_in(w):
        w = jnp.tile(w[::-1].astype(BF16).reshape(L, kb, LANES, S5_STATE), (1, 1, 1, gb))
        w = jnp.where(group_mask(LANES, S5_GROUP, gb * S5_STATE, S5_STATE), w, zero)
        return w.transpose(1, 0, 2, 3).reshape(kb, L * LANES, gb * S5_STATE)
    p_big = jnp.concatenate([state_in(wr), state_in(wi)], axis=2)

    cre_t, cim_t = cre.transpose(0, 2, 1), cim.transpose(0, 2, 1)
    def state_out(m):
        m = spread_channels(m.reshape(L, kb, gb * S5_STATE, S5_GROUP))
        m = jnp.where(group_mask(gb * S5_STATE, S5_STATE, LANES, S5_GROUP), m, zero)
        return m.transpose(1, 2, 0, 3).reshape(kb, gb * S5_STATE, L * LANES)
    q_re = cre_t[None] * pr[1:, :, :, None] - cim_t[None] * pi[1:, :, :, None]
    q_im = -(cre_t[None] * pi[1:, :, :, None] + cim_t[None] * pr[1:, :, :, None])
    q_big = jnp.concatenate([state_out(q_re), state_out(q_im)], axis=1)
    a_big = jnp.stack([pr[L].reshape(kb, gb * S5_STATE), pi[L].reshape(kb, gb * S5_STATE)], axis=1)
    return k_blk, p_big, q_big, a_big


def _s5_core(h2, bsz, seq, a_re, a_im, log_step, b_re, b_im, c_re, c_im):
    L = S5_CHUNK
    n_chunks = seq // L
    rows = bsz * n_chunks
    kb = D_MODEL // LANES
    rb = min(S5_ROWS, rows)
    assert rb % n_chunks == 0
    k_blk, p_big, q_big, a_big = _s5_operators(a_re, a_im, log_step, b_re, b_im, c_re, c_im)
    n_state = 8 * S5_STATE
    x_specs = [pl.BlockSpec((rb, LANES), lambda k, r, t=t: (r, t * kb + k)) for t in range(L)]
    return pl.pallas_call(
        functools.partial(_s5_kernel, n_chunks),
        grid=(kb, rows // rb),
        in_specs=x_specs + [
            pl.BlockSpec((None, L, LANES, LANES), lambda k, r: (k, 0, 0, 0)),
            pl.BlockSpec((None, L * LANES, 2 * n_state), lambda k, r: (k, 0, 0)),
            pl.BlockSpec((None, 2 * n_state, L * LANES), lambda k, r: (k, 0, 0)),
            pl.BlockSpec((None, 2, n_state), lambda k, r: (k, 0, 0))],
        out_specs=pl.BlockSpec((rb * L, LANES), lambda k, r: (r, k)),
        out_shape=jax.ShapeDtypeStruct((rows * L, D_MODEL), F32),
        scratch_shapes=[pltpu.VMEM((rb, 2 * n_state), F32), pltpu.VMEM((rb, 2 * n_state), F32),
                        pltpu.VMEM((L * LANES, L * LANES), BF16)],
        compiler_params=_params(2),
        name="s5_core",
    )(*([h2] * L), k_blk, p_big, q_big, a_big)


def _glu_kernel(x_ref, y_ref, g_ref, d_ref, wa_ref, wb_ref, o_ref):
    x = x_ref[...]
    y = y_ref[...] + d_ref[...] * _rms(x, g_ref[...])
    y = jax.nn.gelu(y).astype(BF16)
    o_ref[...] = x + _dot(y, wa_ref[...]) * jax.nn.sigmoid(_dot(y, wb_ref[...]))


def _s5_layer(x, h2, bsz, seq, mix_g, a_re, a_im, log_step, b_re, b_im, c_re, c_im, d_skip,
              w_a, w_b):
    t = x.shape[0]
    tm = min(FFN_TILE, t)
    y = _s5_core(h2, bsz, seq, a_re, a_im, log_step, b_re, b_im, c_re, c_im)
    row = pl.BlockSpec((tm, D_MODEL), lambda i: (i, 0))
    return pl.pallas_call(
        _glu_kernel,
        grid=(t // tm,),
        in_specs=[row, row, _const_spec((1, D_MODEL)), _const_spec((1, D_MODEL)),
                  _const_spec((D_MODEL, D_MODEL)), _const_spec((D_MODEL, D_MODEL))],
        out_specs=row,
        out_shape=jax.ShapeDtypeStruct((t, D_MODEL), F32),
        compiler_params=_params(1),
        name="s5_glu",
    )(x, y, mix_g.astype(F32).reshape(1, D_MODEL), d_skip.astype(F32).reshape(1, D_MODEL),
      w_a.astype(BF16), w_b.astype(BF16))


def kernel(x, ffn1_norm, ffn1_gate, ffn1_up, ffn1_down, mix_norm, ffn2_norm, ffn2_gate, ffn2_up, ffn2_down, attn_w_in, attn_w_out, fg_bias, diff_q_norm, diff_k_norm, diff_lambda_q1, diff_lambda_k1, diff_lambda_q2, diff_lambda_k2, diff_subln, fox_q_norm, fox_k_norm, rel_bias, s5_a_re, s5_a_im, s5_log_step, s5_b_re, s5_b_im, s5_c_re, s5_c_im, s5_d, s5_glu_a, s5_glu_b):
    bsz, seq, _ = x.shape
    depth = ffn1_norm.shape[0]
    assert min(ATTN_Q, seq) >= REL_MAX_DIST, "bias tiles assume the bias saturates within a block"
    bias_tiles = _diff_bias_tiles(rel_bias, min(ATTN_Q, seq))
    xt = x.reshape(bsz * seq, D_MODEL).astype(F32)
    for i in range(depth):
        xt, h = _ffn(xt, ffn1_norm[i], ffn1_gate[i].astype(BF16), ffn1_up[i].astype(BF16),
                     ffn1_down[i].astype(BF16), g_next=mix_norm[i], chunked=i % 2 == 1)
        j = i // 2
        if i % 2 == 0:
            xt = _attn_layer(xt, h, bsz, seq, i, attn_w_in[j], attn_w_out[j], fg_bias[j],
                             diff_q_norm[j], diff_k_norm[j], diff_lambda_q1[j], diff_lambda_k1[j],
                             diff_lambda_q2[j], diff_lambda_k2[j], diff_subln[j], fox_q_norm[j],
                             fox_k_norm[j], bias_tiles)
        else:
            xt = _s5_layer(xt, h, bsz, seq, mix_norm[i], s5_a_re[j], s5_a_im[j], s5_log_step[j],
                           s5_b_re[j], s5_b_im[j], s5_c_re[j], s5_c_im[j], s5_d[j],
                           s5_glu_a[j], s5_glu_b[j])
        xt, _ = _ffn(xt, ffn2_norm[i], ffn2_gate[i].astype(BF16), ffn2_up[i].astype(BF16),
                     ffn2_down[i].astype(BF16))
    return xt.reshape(bsz, seq, D_MODEL).astype(x.dtype)
```
